```python
import math
import jax
import jax.numpy as jnp
from jax import lax
import numpy as np

D_MODEL = 2048
BATCH = 32
SEQ = 256
DEPTH = 4
DEC_BATCH = 8
DEC_SEQ = 1024
PAST_LEN = 256

GRID_W = 64
EPS = 1e-6
MIX_W = D_MODEL
HEAD_DIM = 128
ATTN_W = D_MODEL // 2
N_HEADS = ATTN_W // HEAD_DIM
N_KV_HEADS = N_HEADS // 4
Q_PER_KV = N_HEADS // N_KV_HEADS
KV_W = N_KV_HEADS * HEAD_DIM
ATTN_WINDOW = 128
ATTN_BLOCK = 128
RET_W = D_MODEL // 4
RET_DK = 128
N_RET_HEADS = RET_W // RET_DK
RET_CHUNK = 128
SSM_WIDTH = D_MODEL // 4
SSM_GROUP = 16
N_SSM_GROUPS = SSM_WIDTH // SSM_GROUP
SSM_STATE = 64
N_EXPERTS = 16
EXPERT_FF = D_MODEL // 2
EC_CAPACITY = 2
ROPE_BASE = 10000.0
IN_SPLITS = (ATTN_W, KV_W, KV_W, RET_W, RET_W, RET_W, RET_W, SSM_WIDTH)
IN_W = ATTN_W + 2 * KV_W + 4 * RET_W + SSM_WIDTH

kernel_name = 'hybrid_dit_prefix_step'

F32 = jnp.float32


def rms_norm(x, g):
    xf = x.astype(F32)
    y = xf * lax.rsqrt(jnp.mean(xf * xf, axis=-1, keepdims=True) + EPS)
    return (y * g.astype(F32)).astype(x.dtype)


def ada_params(cvec, mod_w, mod_b):
    m = jnp.einsum('bd,de->be', jax.nn.silu(cvec), mod_w) + mod_b
    return jnp.split(m[:, None, :], 6, axis=-1)


def modulate(h, shift, scale):
    return h * (1.0 + scale) + shift


def axial_rope_tables(n_tokens):
    n_rows = n_tokens // GRID_W
    row = jnp.repeat(jnp.arange(n_rows), GRID_W).astype(F32)
    col = jnp.tile(jnp.arange(GRID_W), n_rows).astype(F32)
    axis_dim = HEAD_DIM // 2
    inv_freq = ROPE_BASE ** (-jnp.arange(0, axis_dim, 2, dtype=F32) / axis_dim)
    ang_r = row[:, None] * inv_freq[None, :]
    ang_c = col[:, None] * inv_freq[None, :]
    return (jnp.cos(ang_r), jnp.sin(ang_r), jnp.cos(ang_c), jnp.sin(ang_c))


def _rotate(x, cos, sin):
    m = x.shape[-1] // 2
    x1, x2 = x[..., :m], x[..., m:]
    cs = cos[None, :, None, :]
    sn = sin[None, :, None, :]
    return jnp.concatenate([x1 * cs - x2 * sn, x1 * sn + x2 * cs], axis=-1)


def apply_axial_rope(x, tables):
    cr, sr, cc, sc = tables
    xf = x.astype(F32)
    half = x.shape[-1] // 2
    out = jnp.concatenate([_rotate(xf[..., :half], cr, sr), _rotate(xf[..., half:], cc, sc)], axis=-1)
    return out.astype(x.dtype)


def sink_softmax(s, sink):
    sk = sink[None, :, :, None, None]
    m = jnp.maximum(jnp.max(s, axis=-1, keepdims=True), sk)
    e = jnp.exp(s - m)
    return e / (jnp.sum(e, axis=-1, keepdims=True) + jnp.exp(sk - m))


def attention_context(q, k, v, sink):
    B, S = q.shape[0], q.shape[1]
    nb = S // ATTN_BLOCK
    scale = HEAD_DIM ** -0.5
    sink_f = sink.astype(F32).reshape(N_KV_HEADS, Q_PER_KV)
    qb = q.reshape(B, nb, ATTN_BLOCK, N_KV_HEADS, Q_PER_KV, HEAD_DIM).transpose(1, 0, 2, 3, 4, 5)

    def one_block(qblk):
        s = jnp.einsum('bqkgd,bskd->bkgqs', qblk, k).astype(F32) * scale
        p = sink_softmax(s, sink_f).astype(v.dtype)
        return jnp.einsum('bkgqs,bskd->bqkgd', p, v)

    out = lax.map(one_block, qb)
    return out.transpose(1, 0, 2, 3, 4, 5).reshape(B, S, ATTN_W)


def attention_latent(q, k, v, k_ctx, v_ctx, sink):
    B, L = q.shape[0], q.shape[1]
    nb = L // ATTN_BLOCK
    scale = HEAD_DIM ** -0.5
    sink_f = sink.astype(F32).reshape(N_KV_HEADS, Q_PER_KV)
    pad = ((0, 0), (ATTN_BLOCK, ATTN_BLOCK), (0, 0), (0, 0))
    kp = jnp.pad(k, pad).reshape(B, nb + 2, ATTN_BLOCK, N_KV_HEADS, HEAD_DIM)
    vp = jnp.pad(v, pad).reshape(B, nb + 2, ATTN_BLOCK, N_KV_HEADS, HEAD_DIM)
    kw = jnp.concatenate([kp[:, :-2], kp[:, 1:-1], kp[:, 2:]], axis=2)
    vw = jnp.concatenate([vp[:, :-2], vp[:, 1:-1], vp[:, 2:]], axis=2)
    blk = jnp.arange(nb)[:, None, None]
    qpos = blk * ATTN_BLOCK + jnp.arange(ATTN_BLOCK)[None, :, None]
    kpos = (blk - 1) * ATTN_BLOCK + jnp.arange(3 * ATTN_BLOCK)[None, None, :]
    valid = (jnp.abs(qpos - kpos) <= ATTN_WINDOW) & (kpos >= 0) & (kpos < L)
    qb = q.reshape(B, nb, ATTN_BLOCK, N_KV_HEADS, Q_PER_KV, HEAD_DIM).transpose(1, 0, 2, 3, 4, 5)
    n_loc = 3 * ATTN_BLOCK

    def one_block(args):
        qblk, kblk, vblk, vmask = args
        s_loc = jnp.einsum('bqkgd,bskd->bkgqs', qblk, kblk).astype(F32) * scale
        s_loc = jnp.where(vmask[None, None, None], s_loc, -jnp.inf)
        s_ctx = jnp.einsum('bqkgd,bskd->bkgqs', qblk, k_ctx).astype(F32) * scale
        p = sink_softmax(jnp.concatenate([s_loc, s_ctx], axis=-1), sink_f).astype(v.dtype)
        o_loc = jnp.einsum('bkgqs,bskd->bqkgd', p[..., :n_loc], vblk)
        o_ctx = jnp.einsum('bkgqs,bskd->bqkgd', p[..., n_loc:], v_ctx.astype(v.dtype))
        return o_loc + o_ctx

    out = lax.map(one_block, (qb, kw.transpose(1, 0, 2, 3, 4), vw.transpose(1, 0, 2, 3, 4), valid))
    return out.transpose(1, 0, 2, 3, 4, 5).reshape(B, L, ATTN_W)


def retention_dir(q, k, v, log_gamma, r0):
    B, L, H, dk = q.shape
    n = L // RET_CHUNK
    C = RET_CHUNK
    qc = q.reshape(B, n, C, H, dk)
    kc = k.reshape(B, n, C, H, dk)
    vc = v.reshape(B, n, C, H, v.shape[-1])
    lg = log_gamma.astype(F32)
    pos = jnp.arange(C, dtype=F32)
    diff = pos[:, None] - pos[None, :]
    decay = jnp.exp(jnp.maximum(diff, 0.0)[None] * lg[:, None, None]) * (diff >= 0)[None]
    s = jnp.einsum('bnihd,bnjhd->bnhij', qc, kc).astype(F32) * decay[None, None]
    intra = jnp.einsum('bnhij,bnjhe->bnihe', s.astype(v.dtype), vc).astype(F32)
    q_dec = jnp.exp((pos + 1.0)[:, None] * lg[None, :])
    k_dec = jnp.exp((C - 1.0 - pos)[:, None] * lg[None, :])
    kv = jnp.einsum('bnjhd,bnjhe->nbhde', kc.astype(F32) * k_dec[:, :, None], vc.astype(F32))
    chunk_decay = jnp.exp(C * lg)[None, :, None, None]

    def step(r, kv_n):
        return chunk_decay * r + kv_n, r

    r_final, r_before = lax.scan(step, r0.astype(F32), kv)
    cross = jnp.einsum('bnihd,nbhde->bnihe', qc.astype(F32) * q_dec[:, :, None], r_before)
    return (intra + cross).reshape(B, L, H, v.shape[-1]), r_final


def retention_bidir(rq, rk, rv, rg, log_gamma2, gn_g, r0):
    B, L = rq.shape[0], rq.shape[1]
    of, rf = retention_dir(rq, rk, rv, log_gamma2[0], r0[:, 0])
    ob, rb = retention_dir(rq[:, ::-1], rk[:, ::-1], rv[:, ::-1], log_gamma2[1], r0[:, 1])
    o = of + ob[:, ::-1]
    mu = jnp.mean(o, axis=-1, keepdims=True)
    var = jnp.mean(jnp.square(o - mu), axis=-1, keepdims=True)
    o = ((o - mu) * lax.rsqrt(var + EPS)).reshape(B, L, RET_W) * gn_g.astype(F32)
    return (jax.nn.silu(rg.astype(F32)) * o).astype(rg.dtype), rf, rb


def ssm_scan(bu, lam_bar, h0):
    bu = bu.at[:, 0].add(lam_bar * h0)
    a = jnp.broadcast_to(lam_bar, bu.shape)

    def combine(e1, e2):
        a1, b1 = e1
        a2, b2 = e2
        return a1 * a2, a2 * b1 + b2

    _, hs = lax.associative_scan(combine, (a, bu), axis=1)
    return hs


def ssm_bidir(u, p, h0):
    B, L = u.shape[0], u.shape[1]
    uf = u.astype(F32).reshape(B, L, N_SSM_GROUPS, SSM_GROUP)
    ys = []
    finals = []
    for d in range(2):
        lam = lax.complex(p['ssm_lam_re'][d].astype(F32), p['ssm_lam_im'][d].astype(F32))
        dt = jnp.exp(p['ssm_log_step'][d].astype(F32))[:, None]
        lam_bar = jnp.exp(lam * dt)
        b = lax.complex(p['ssm_b_re'][d].astype(F32), p['ssm_b_im'][d].astype(F32))
        b_bar = ((lam_bar - 1.0) / lam)[..., None] * b
        cmat = lax.complex(p['ssm_c_re'][d].astype(F32), p['ssm_c_im'][d].astype(F32))
        ud = uf if d == 0 else uf[:, ::-1]
        bu = jnp.einsum('blgh,gph->blgp', ud.astype(jnp.complex64), b_bar)
        hs = ssm_scan(bu, lam_bar, h0[:, d])
        yd = jnp.einsum('blgp,ghp->blgh', hs, cmat).real
        ys.append(yd if d == 0 else yd[:, ::-1])
        finals.append(hs[:, -1])
    y = ys[0] + ys[1] + p['ssm_d'].astype(F32).reshape(N_SSM_GROUPS, SSM_GROUP) * uf
    z = jax.nn.gelu(y.reshape(B, L, SSM_WIDTH))
    out = z * jax.nn.sigmoid(z @ p['ssm_glu_w'].astype(F32) + p['ssm_glu_b'].astype(F32))
    return out.astype(u.dtype), finals[0], finals[1]


def split_projection(h, w_in):
    z = jnp.einsum('bld,de->ble', h, w_in)
    parts = []
    start = 0
    for width in IN_SPLITS:
        parts.append(z[..., start:start + width])
        start += width
    return parts


def token_mixers(h, p, rope, ctx):
    B, L = h.shape[0], h.shape[1]
    q, k, v, rq, rk, rv, rg, u = split_projection(h, p['w_in'])
    q = q.reshape(B, L, N_HEADS, HEAD_DIM)
    k = k.reshape(B, L, N_KV_HEADS, HEAD_DIM)
    v = v.reshape(B, L, N_KV_HEADS, HEAD_DIM)
    rq = rq.reshape(B, L, N_RET_HEADS, RET_DK)
    rk = rk.reshape(B, L, N_RET_HEADS, RET_DK) * (RET_DK ** -0.5)
    rv = rv.reshape(B, L, N_RET_HEADS, RET_DK)
    if rope is None:
        attn = attention_context(q, k, v, p['attn_sink'])
        r0 = jnp.zeros((B, 2, N_RET_HEADS, RET_DK, RET_DK), F32)
        h0 = jnp.zeros((B, 2, N_SSM_GROUPS, SSM_STATE), jnp.complex64)
    else:
        q = apply_axial_rope(q, rope)
        k = apply_axial_rope(k, rope)
        rq = apply_axial_rope(rq, rope)
        rk = apply_axial_rope(rk, rope)
        attn = attention_latent(q, k, v, ctx['k'], ctx['v'], p['attn_sink'])
        r0 = ctx['ret']
        h0 = ctx['ssm']
    ret, rf, rb = retention_bidir(rq, rk, rv, rg, p['ret_log_gamma'], p['ret_gn_g'], r0)
    ssm, hf, hb = ssm_bidir(u, p, h0)
    mix = jnp.einsum('ble,ed->bld', jnp.concatenate([attn, ret, ssm], axis=-1), p['w_out'])
    ctx_out = None
    if rope is None:
        ctx_out = (k, v, jnp.stack([rf, rb], axis=1), jnp.stack([hf, hb], axis=1))
    return mix, ctx_out


def expert_choice_ffn(h, p):
    B, T, D = h.shape
    cap = EC_CAPACITY * T // N_EXPERTS
    aff = jax.nn.softmax(jnp.einsum('btd,de->bte', h, p['router_w']).astype(F32), axis=-1)
    gate, idx = lax.top_k(jnp.swapaxes(aff, 1, 2), cap)
    xs = jax.vmap(lambda hb, ib: hb[ib])(h, idx)
    a = jnp.einsum('becd,edf->becf', xs, p['moe_w_gate'])
    b = jnp.einsum('becd,edf->becf', xs, p['moe_w_up'])
    o = jnp.einsum('becf,efd->becd', jax.nn.silu(a) * b, p['moe_w_down']) * gate[..., None].astype(h.dtype)
    return jax.vmap(lambda ib, ob: jnp.zeros((T, D), ob.dtype).at[ib.reshape(-1)].add(ob.reshape(-1, D)))(idx, o)


def trunk_layer(x, cvec, p, rope, ctx):
    sh1, sc1, g1, sh2, sc2, g2 = ada_params(cvec, p['mod_w'], p['mod_b'])
    h = modulate(rms_norm(x, p['norm1_g']), sh1, sc1)
    mix, ctx_out = token_mixers(h, p, rope, ctx)
    x = x + g1 * mix
    h = modulate(rms_norm(x, p['norm2_g']), sh2, sc2)
    x = x + g2 * expert_choice_ffn(h, p)
    return x, ctx_out


def _normal(k, shape, scale):
    return jax.random.normal(k, shape, F32) * scale


def setup_inputs(seed: int = 0) -> dict:
    key = jax.random.key(seed)
    ks = iter(jax.random.split(key, 40))
    G, P, H = N_SSM_GROUPS, SSM_STATE, SSM_GROUP
    base_lg = jnp.log1p(-(2.0 ** (-5.0 - jnp.arange(N_RET_HEADS, dtype=F32))))
    return {
        'x_prompt': _normal(next(ks), (BATCH, SEQ, D_MODEL), 1.0),
        'x_sample': _normal(next(ks), (DEC_BATCH, DEC_SEQ, D_MODEL), 1.0),
        'cache_attn_k': _normal(next(ks), (DEC_BATCH, DEPTH, PAST_LEN, N_KV_HEADS, HEAD_DIM), 1.0),
        'cache_attn_v': _normal(next(ks), (DEC_BATCH, DEPTH, PAST_LEN, N_KV_HEADS, HEAD_DIM), 1.0),
        'state_ret': _normal(next(ks), (DEC_BATCH, DEPTH, 2, N_RET_HEADS, RET_DK, RET_DK), 0.5),
        'state_ssm_re': _normal(next(ks), (DEC_BATCH, DEPTH, 2, G, P), 0.1),
        'state_ssm_im': _normal(next(ks), (DEC_BATCH, DEPTH, 2, G, P), 0.1),
        'c': _normal(next(ks), (DEC_BATCH, D_MODEL), 1.0),
        'c_ctx': _normal(next(ks), (D_MODEL,), 1.0),
        'mod_w': _normal(next(ks), (DEPTH, D_MODEL, 6 * D_MODEL), 0.3 * D_MODEL ** -0.5),
        'mod_b': _normal(next(ks), (DEPTH, 6 * D_MODEL), 0.02),
        'norm1_g': 1.0 + _normal(next(ks), (DEPTH, D_MODEL), 0.02),
        'norm2_g': 1.0 + _normal(next(ks), (DEPTH, D_MODEL), 0.02),
        'w_in': _normal(next(ks), (DEPTH, D_MODEL, IN_W), D_MODEL ** -0.5),
        'w_out': _normal(next(ks), (DEPTH, MIX_W, D_MODEL), MIX_W ** -0.5),
        'attn_sink': _normal(next(ks), (DEPTH, N_HEADS), 0.5),
        'ret_log_gamma': base_lg * (1.0 + _normal(next(ks), (DEPTH, 2, N_RET_HEADS), 0.05)),
        'ret_gn_g': 1.0 + _normal(next(ks), (DEPTH, RET_W), 0.02),
        'ssm_lam_re': -0.5 + _normal(next(ks), (DEPTH, 2, G, P), 0.01),
        'ssm_lam_im': jnp.pi * jnp.arange(P, dtype=F32) + _normal(next(ks), (DEPTH, 2, G, P), 0.01),
        'ssm_log_step': jax.random.uniform(next(ks), (DEPTH, 2, G), F32, math.log(1e-3), math.log(1e-1)),
        'ssm_b_re': _normal(next(ks), (DEPTH, 2, G, P, H), (2.0 * H) ** -0.5),
        'ssm_b_im': _normal(next(ks), (DEPTH, 2, G, P, H), (2.0 * H) ** -0.5),
        'ssm_c_re': _normal(next(ks), (DEPTH, 2, G, H, P), P ** -0.5),
        'ssm_c_im': _normal(next(ks), (DEPTH, 2, G, H, P), P ** -0.5),
        'ssm_d': _normal(next(ks), (DEPTH, SSM_WIDTH), 1.0),
        'ssm_glu_w': _normal(next(ks), (DEPTH, SSM_WIDTH, SSM_WIDTH), SSM_WIDTH ** -0.5),
        'ssm_glu_b': _normal(next(ks), (DEPTH, SSM_WIDTH), 0.02),
        'router_w': _normal(next(ks), (DEPTH, D_MODEL, N_EXPERTS), D_MODEL ** -0.5),
        'moe_w_gate': _normal(next(ks), (DEPTH, N_EXPERTS, D_MODEL, EXPERT_FF), D_MODEL ** -0.5),
        'moe_w_up': _normal(next(ks), (DEPTH, N_EXPERTS, D_MODEL, EXPERT_FF), D_MODEL ** -0.5),
        'moe_w_down': _normal(next(ks), (DEPTH, N_EXPERTS, EXPERT_FF, D_MODEL), EXPERT_FF ** -0.5),
        'final_norm_g': 1.0 + _normal(next(ks), (D_MODEL,), 0.02),
    }


def reference(x_prompt, x_sample, cache_attn_k, cache_attn_v, state_ret, state_ssm_re, state_ssm_im, c, c_ctx,
              mod_w, mod_b, norm1_g, norm2_g, w_in, w_out, attn_sink, ret_log_gamma, ret_gn_g,
              ssm_lam_re, ssm_lam_im, ssm_log_step, ssm_b_re, ssm_b_im, ssm_c_re, ssm_c_im, ssm_d,
              ssm_glu_w, ssm_glu_b, router_w, moe_w_gate, moe_w_up, moe_w_down, final_norm_g):
    rope = axial_rope_tables(x_sample.shape[1])
    xp = x_prompt
    xs = x_sample
    ks_out, vs_out, ret_out, ssm_out = [], [], [], []
    for l in range(DEPTH):
        p = {
            'mod_w': mod_w[l], 'mod_b': mod_b[l], 'norm1_g': norm1_g[l], 'norm2_g': norm2_g[l],
            'w_in': w_in[l], 'w_out': w_out[l], 'attn_sink': attn_sink[l],
            'ret_log_gamma': ret_log_gamma[l], 'ret_gn_g': ret_gn_g[l],
            'ssm_lam_re': ssm_lam_re[l], 'ssm_lam_im': ssm_lam_im[l], 'ssm_log_step': ssm_log_step[l],
            'ssm_b_re': ssm_b_re[l], 'ssm_b_im': ssm_b_im[l], 'ssm_c_re': ssm_c_re[l], 'ssm_c_im': ssm_c_im[l],
            'ssm_d': ssm_d[l], 'ssm_glu_w': ssm_glu_w[l], 'ssm_glu_b': ssm_glu_b[l],
            'router_w': router_w[l], 'moe_w_gate': moe_w_gate[l], 'moe_w_up': moe_w_up[l],
            'moe_w_down': moe_w_down[l],
        }
        xp, (k_l, v_l, r_l, h_l) = trunk_layer(xp, c_ctx[None, :], p, None, None)
        ks_out.append(k_l)
        vs_out.append(v_l)
        ret_out.append(r_l)
        ssm_out.append(h_l)
        ctx = {
            'k': cache_attn_k[:, l], 'v': cache_attn_v[:, l], 'ret': state_ret[:, l],
            'ssm': lax.complex(state_ssm_re[:, l].astype(F32), state_ssm_im[:, l].astype(F32)),
        }
        xs, _ = trunk_layer(xs, c, p, rope, ctx)
    y_prompt = rms_norm(xp, final_norm_g)
    y_sample = rms_norm(xs, final_norm_g)
    new_attn_k = jnp.stack(ks_out, axis=1)
    new_attn_v = jnp.stack(vs_out, axis=1)
    new_ret = jnp.stack(ret_out, axis=1)
    ssm_all = jnp.stack(ssm_out, axis=1)
    new_ssm_re = ssm_all.real
    new_ssm_im = ssm_all.imag
    return (y_prompt, y_sample, new_attn_k, new_attn_v, new_ret, new_ssm_re, new_ssm_im)
```

```python
import functools
import math

import jax
import jax.numpy as jnp
from jax import lax
from jax.experimental import pallas as pl
from jax.experimental.pallas import tpu as pltpu

F32 = jnp.float32
BF16 = jnp.bfloat16
HIGHEST = lax.Precision.HIGHEST

D_MODEL = 2048
GRID_W = 64
EPS = 1e-6
HEAD_DIM = 128
ATTN_W = D_MODEL // 2
N_HEADS = ATTN_W // HEAD_DIM
N_KV_HEADS = N_HEADS // 4
Q_PER_KV = N_HEADS // N_KV_HEADS
KV_W = N_KV_HEADS * HEAD_DIM
ATTN_BLOCK = 128
RET_W = D_MODEL // 4
RET_DK = 128
N_RET_HEADS = RET_W // RET_DK
RET_CHUNK = 128
SSM_WIDTH = D_MODEL // 4
SSM_GROUP = 16
N_SSM_GROUPS = SSM_WIDTH // SSM_GROUP
SSM_STATE = 64
N_EXPERTS = 16
EXPERT_FF = D_MODEL // 2
EC_CAPACITY = 2
ROPE_BASE = 10000.0
IN_W = ATTN_W + 2 * KV_W + 4 * RET_W + SSM_WIDTH

LANES = 128
HEAD_Q0 = 0
HEAD_K0 = ATTN_W // LANES
HEAD_V0 = HEAD_K0 + KV_W // LANES
HEAD_RQ0 = HEAD_V0 + KV_W // LANES
HEAD_RK0 = HEAD_RQ0 + RET_W // LANES
HEAD_RV0 = HEAD_RK0 + RET_W // LANES
HEAD_RG0 = HEAD_RV0 + RET_W // LANES
HEAD_U0 = HEAD_RG0 + RET_W // LANES

SSM_CHUNK = 8
SSM_GL = LANES // SSM_GROUP
SSM_NQ = SSM_WIDTH // LANES
SSM_FLAT = SSM_CHUNK * LANES
SSM_SW = SSM_GL * 2 * SSM_STATE

MOD_ROWS = 16
ROW_TILE = 1024
VMEM_LIMIT = 56 * 1024 * 1024


def _cparams(sem):
    return pltpu.CompilerParams(dimension_semantics=sem, vmem_limit_bytes=VMEM_LIMIT)


def _silu(x):
    return x * jax.nn.sigmoid(x)


def _nt_dot(a, b):
    return lax.dot_general(a, b, (((1,), (1,)), ((), ())), preferred_element_type=F32)


def _tn_dot(a, b):
    return lax.dot_general(a, b, (((0,), (0,)), ((), ())), preferred_element_type=F32)


def _mod_kernel(cv_ref, w_ref, b_ref, o_ref):
    a = _silu(cv_ref[...])
    o_ref[0] = jnp.dot(a, w_ref[0], preferred_element_type=F32, precision=HIGHEST) + b_ref[0]


def _mod_call(cv, mod_w, mod_b):
    depth, d, e = mod_w.shape
    tn = 1024
    return pl.pallas_call(
        _mod_kernel,
        grid=(depth, e // tn),
        in_specs=[
            pl.BlockSpec((MOD_ROWS, d), lambda l, j: (0, 0)),
            pl.BlockSpec((1, d, tn), lambda l, j: (l, 0, j)),
            pl.BlockSpec((1, 1, tn), lambda l, j: (l, 0, j)),
        ],
        out_specs=pl.BlockSpec((1, MOD_ROWS, tn), lambda l, j: (l, 0, j)),
        out_shape=jax.ShapeDtypeStruct((depth, MOD_ROWS, e), F32),
        compiler_params=_cparams(("parallel", "parallel")),
        name="mod",
    )(cv, mod_w, mod_b.reshape(depth, 1, e))


def _rope_head(z, cos, sin_signed):
    lane = lax.broadcasted_iota(jnp.int32, z.shape, 1)
    partner = jnp.where((lane % 64) < 32, pltpu.roll(z, LANES - 32, 1), pltpu.roll(z, 32, 1))
    return z * cos + partner * sin_signed


def _inproj_kernel(x_ref, g_ref, sh_ref, sc_ref, w_ref, cos_ref, sin_ref, z_ref, u_ref, h_scr, *, use_rope, tn):
    j = pl.program_id(1)
    last = pl.num_programs(1) - 1

    @pl.when(j == 0)
    def _():
        x = x_ref[...]
        ms = jnp.mean(x * x, axis=-1, keepdims=True)
        hn = x * lax.rsqrt(ms + EPS) * g_ref[...]
        h_scr[...] = (hn * (1.0 + sc_ref[0]) + sh_ref[0]).astype(BF16)

    acc = jnp.dot(h_scr[...], w_ref[0].astype(BF16), preferred_element_type=F32)

    @pl.when(j == last)
    def _():
        u_ref[...] = acc

    @pl.when(j < last)
    def _():
        heads_per_tile = tn // LANES
        for k in range(heads_per_tile):
            head = j * heads_per_tile + k
            sl = slice(k * LANES, (k + 1) * LANES)
            is_rk = (head >= HEAD_RK0) & (head < HEAD_RV0)
            zk = acc[:, sl] * jnp.where(is_rk, RET_DK ** -0.5, 1.0)
            if use_rope:
                is_rope = (head < HEAD_V0) | ((head >= HEAD_RQ0) & (head < HEAD_RV0))

                @pl.when(is_rope)
                def _():
                    z_ref[:, sl] = _rope_head(zk, cos_ref[...], sin_ref[...]).astype(z_ref.dtype)

                @pl.when(jnp.logical_not(is_rope))
                def _():
                    z_ref[:, sl] = zk.astype(z_ref.dtype)
            else:
                z_ref[:, sl] = zk.astype(z_ref.dtype)


def _inproj_call(x2d, g, shift, scale, w_in, layer, cos, sin_signed, use_rope):
    n, d = x2d.shape
    tm, tn = ROW_TILE, SSM_WIDTH
    z_w = IN_W - SSM_WIDTH
    nz = z_w // tn
    nb = shift.shape[0]
    mod_idx = (lambda i, j: (i, 0, 0)) if nb > 1 else (lambda i, j: (0, 0, 0))
    return pl.pallas_call(
        functools.partial(_inproj_kernel, use_rope=use_rope, tn=tn),
        grid=(n // tm, nz + 1),
        in_specs=[
            pl.BlockSpec((tm, d), lambda i, j: (i, 0)),
            pl.BlockSpec((1, d), lambda i, j: (0, 0)),
            pl.BlockSpec((1, 1, d), mod_idx),
            pl.BlockSpec((1, 1, d), mod_idx),
            pl.BlockSpec((1, d, tn), lambda i, j: (layer, 0, j)),
            pl.BlockSpec((tm, LANES), lambda i, j: (0, 0)),
            pl.BlockSpec((tm, LANES), lambda i, j: (0, 0)),
        ],
        out_specs=[
            pl.BlockSpec((tm, tn), lambda i, j: (i, jnp.minimum(j, nz - 1))),
            pl.BlockSpec((tm, tn), lambda i, j: (i, 0)),
        ],
        out_shape=[jax.ShapeDtypeStruct((n, z_w), BF16), jax.ShapeDtypeStruct((n, SSM_WIDTH), F32)],
        scratch_shapes=[pltpu.VMEM((tm, d), BF16)],
        compiler_params=_cparams(("parallel", "arbitrary")),
        name="inproj",
    )(x2d, g, shift, scale, w_in, cos, sin_signed)


def _sink_softmax_rows(s, sink_col):
    m = jnp.maximum(jnp.max(s, axis=-1, keepdims=True), sink_col)
    e = jnp.exp(s - m)
    return e / (jnp.sum(e, axis=-1, keepdims=True) + jnp.exp(sink_col - m))


def _stack_q(q_ref, kh):
    return jnp.concatenate(
        [q_ref[:, (kh * Q_PER_KV + g) * HEAD_DIM:(kh * Q_PER_KV + g + 1) * HEAD_DIM] for g in range(Q_PER_KV)], axis=0)


def _sink_col(sink_ref, kh, rows):
    return jnp.concatenate(
        [jnp.full((rows, 1), sink_ref[kh * Q_PER_KV + g], F32) for g in range(Q_PER_KV)], axis=0)


def _attn_ctx_kernel(sink_ref, q_ref, k_ref, v_ref, o_ref, kn_ref, vn_ref):
    rows = q_ref.shape[0]
    scale = HEAD_DIM ** -0.5
    for kh in range(N_KV_HEADS):
        hs = slice(kh * HEAD_DIM, (kh + 1) * HEAD_DIM)
        s = _nt_dot(_stack_q(q_ref, kh), k_ref[:, hs]) * scale
        p = _sink_softmax_rows(s, _sink_col(sink_ref, kh, rows)).astype(BF16)
        o = jnp.dot(p, v_ref[:, hs], preferred_element_type=F32)
        for g in range(Q_PER_KV):
            c0 = (kh * Q_PER_KV + g) * HEAD_DIM
            o_ref[:, c0:c0 + HEAD_DIM] = o[g * rows:(g + 1) * rows].astype(o_ref.dtype)
    kn_ref[0] = k_ref[...].astype(F32)
    vn_ref[0] = v_ref[...].astype(F32)


def _attn_ctx_call(z, sink, batch, seq):
    n = z.shape[0]
    kcol, vcol = HEAD_K0 * LANES // KV_W, HEAD_V0 * LANES // KV_W
    return pl.pallas_call(
        _attn_ctx_kernel,
        grid=(batch,),
        in_specs=[
            pl.BlockSpec(memory_space=pltpu.SMEM),
            pl.BlockSpec((seq, ATTN_W), lambda b: (b, 0)),
            pl.BlockSpec((seq, KV_W), lambda b: (b, kcol)),
            pl.BlockSpec((seq, KV_W), lambda b: (b, vcol)),
        ],
        out_specs=[
            pl.BlockSpec((seq, ATTN_W), lambda b: (b, 0)),
            pl.BlockSpec((1, seq, KV_W), lambda b: (b, 0, 0)),
            pl.BlockSpec((1, seq, KV_W), lambda b: (b, 0, 0)),
        ],
        out_shape=[
            jax.ShapeDtypeStruct((n, ATTN_W), BF16),
            jax.ShapeDtypeStruct((batch, seq, KV_W), F32),
            jax.ShapeDtypeStruct((batch, seq, KV_W), F32),
        ],
        compiler_params=_cparams(("parallel",)),
        name="attn_ctx",
    )(sink, z, z, z)


def _attn_lat_kernel(sink_ref, q_ref, kp_ref, kc_ref, kn_ref, vp_ref, vc_ref, vn_ref, kx_ref, vx_ref, o_ref, *, nblk):
    i = pl.program_id(1)
    rows = q_ref.shape[0]
    scale = HEAD_DIM ** -0.5
    r = lax.broadcasted_iota(jnp.int32, (rows, ATTN_BLOCK), 0)
    c = lax.broadcasted_iota(jnp.int32, (rows, ATTN_BLOCK), 1)
    neg = jnp.float32(-jnp.inf)
    bias_prev = jnp.where(c >= r, jnp.where(i > 0, 0.0, neg), neg)
    bias_next = jnp.where(c <= r, jnp.where(i < nblk - 1, 0.0, neg), neg)
    n_ctx = kx_ref.shape[2]
    bias = jnp.concatenate([bias_prev, jnp.zeros((rows, ATTN_BLOCK), F32), bias_next,
                            jnp.zeros((rows, n_ctx), F32)], axis=1)
    bias = jnp.concatenate([bias] * Q_PER_KV, axis=0)
    for kh in range(N_KV_HEADS):
        hs = slice(kh * HEAD_DIM, (kh + 1) * HEAD_DIM)
        kcat = jnp.concatenate([kp_ref[:, hs], kc_ref[:, hs], kn_ref[:, hs], kx_ref[0, 0, :, hs].astype(BF16)], axis=0)
        vcat = jnp.concatenate([vp_ref[:, hs], vc_ref[:, hs], vn_ref[:, hs], vx_ref[0, 0, :, hs].astype(BF16)], axis=0)
        s = _nt_dot(_stack_q(q_ref, kh), kcat) * scale + bias
        p = _sink_softmax_rows(s, _sink_col(sink_ref, kh, rows)).astype(BF16)
        o = jnp.dot(p, vcat, preferred_element_type=F32)
        for g in range(Q_PER_KV):
            c0 = (kh * Q_PER_KV + g) * HEAD_DIM
            o_ref[:, c0:c0 + HEAD_DIM] = o[g * rows:(g + 1) * rows].astype(o_ref.dtype)


def _attn_lat_call(z, sink, cache_k, cache_v, layer, batch, seq):
    n = z.shape[0]
    nblk = seq // ATTN_BLOCK
    kcol, vcol = HEAD_K0 * LANES // KV_W, HEAD_V0 * LANES // KV_W
    past = cache_k.shape[2]

    def kv_spec(col, off):
        return pl.BlockSpec((ATTN_BLOCK, KV_W),
                            lambda b, i: (b * nblk + jnp.clip(i + off, 0, nblk - 1), col))

    cache_spec = pl.BlockSpec((1, 1, past, KV_W), lambda b, i: (b, layer, 0, 0))
    return pl.pallas_call(
        functools.partial(_attn_lat_kernel, nblk=nblk),
        grid=(batch, nblk),
        in_specs=[
            pl.BlockSpec(memory_space=pltpu.SMEM),
            pl.BlockSpec((ATTN_BLOCK, ATTN_W), lambda b, i: (b * nblk + i, 0)),
            kv_spec(kcol, -1), kv_spec(kcol, 0), kv_spec(kcol, 1),
            kv_spec(vcol, -1), kv_spec(vcol, 0), kv_spec(vcol, 1),
            cache_spec, cache_spec,
        ],
        out_specs=pl.BlockSpec((ATTN_BLOCK, ATTN_W), lambda b, i: (b * nblk + i, 0)),
        out_shape=jax.ShapeDtypeStruct((n, ATTN_W), BF16),
        compiler_params=_cparams(("parallel", "parallel")),
        name="attn_lat",
    )(sink, z, z, z, z, z, z, z, cache_k, cache_v)


def _ret_kernel(lg_ref, q_ref, k_ref, v_ref, rg_ref, gn_ref, r0_ref, o_ref, rfin_ref,
                of_scr, ob_scr, r_scr, dec_scr, qd_scr, kd_scr, cd_scr, *, n_chunks):
    C, dk = RET_CHUNK, RET_DK
    ii = lax.broadcasted_iota(jnp.int32, (C, C), 0).astype(F32)
    jj = lax.broadcasted_iota(jnp.int32, (C, C), 1).astype(F32)
    pos = lax.broadcasted_iota(jnp.int32, (C, dk), 0).astype(F32)
    for h in range(N_RET_HEADS):
        for d in range(2):
            lg = lg_ref[d, h]
            diff = (ii - jj) if d == 0 else (jj - ii)
            dec_scr[d, h] = jnp.where(diff >= 0, jnp.exp(jnp.maximum(diff, 0.0) * lg), 0.0)
            if d == 0:
                qd_scr[d, h] = jnp.exp((pos + 1.0) * lg)
                kd_scr[d, h] = jnp.exp((C - 1.0 - pos) * lg)
            else:
                qd_scr[d, h] = jnp.exp((C - pos) * lg)
                kd_scr[d, h] = jnp.exp(pos * lg)
            cd_scr[d, h] = jnp.exp(jnp.full((8, dk), float(C), F32) * lg)
            r_scr[d, h] = r0_ref[0, 0, d, h]

    def body(t, carry):
        for h in range(N_RET_HEADS):
            hs = slice(h * dk, (h + 1) * dk)
            for d in range(2):
                n = t if d == 0 else n_chunks - 1 - t
                rows = pl.ds(pl.multiple_of(n * C, C), C)
                q, k, v = q_ref[rows, hs], k_ref[rows, hs], v_ref[rows, hs]
                r = r_scr[d, h]
                s = _nt_dot(q, k) * dec_scr[d, h]
                intra = jnp.dot(s.astype(BF16), v, preferred_element_type=F32)
                cross = jnp.dot((q.astype(F32) * qd_scr[d, h]).astype(BF16), r.astype(BF16),
                                preferred_element_type=F32)
                kv = _tn_dot((k.astype(F32) * kd_scr[d, h]).astype(BF16), v)
                (of_scr if d == 0 else ob_scr)[rows, hs] = intra + cross
                r_scr[d, h] = cd_scr[d, h, 0:1, :] * r + kv
        return carry

    lax.fori_loop(0, n_chunks, body, 0)
    rfin_ref[0] = r_scr[...]
    for h in range(N_RET_HEADS):
        hs = slice(h * dk, (h + 1) * dk)
        o = of_scr[:, hs] + ob_scr[:, hs]
        mu = jnp.mean(o, axis=-1, keepdims=True)
        var = jnp.mean(jnp.square(o - mu), axis=-1, keepdims=True)
        on = (o - mu) * lax.rsqrt(var + EPS) * gn_ref[:, hs]
        o_ref[:, hs] = (_silu(rg_ref[:, hs].astype(F32)) * on).astype(o_ref.dtype)


def _ret_call(z, log_gamma, gn_g, r0, state_layer, batch, seq):
    n = z.shape[0]
    dk, H = RET_DK, N_RET_HEADS

    def col(head0):
        return pl.BlockSpec((seq, RET_W), lambda b: (b, head0 * LANES // RET_W))

    return pl.pallas_call(
        functools.partial(_ret_kernel, n_chunks=seq // RET_CHUNK),
        grid=(batch,),
        in_specs=[
            pl.BlockSpec(memory_space=pltpu.SMEM),
            col(HEAD_RQ0), col(HEAD_RK0), col(HEAD_RV0), col(HEAD_RG0),
            pl.BlockSpec((1, RET_W), lambda b: (0, 0)),
            pl.BlockSpec((1, 1, 2, H, dk, dk), lambda b: (b, state_layer, 0, 0, 0, 0)),
        ],
        out_specs=[
            pl.BlockSpec((seq, RET_W), lambda b: (b, 0)),
            pl.BlockSpec((1, 2, H, dk, dk), lambda b: (b, 0, 0, 0, 0)),
        ],
        out_shape=[
            jax.ShapeDtypeStruct((n, RET_W), BF16),
            jax.ShapeDtypeStruct((batch, 2, H, dk, dk), F32),
        ],
        scratch_shapes=[
            pltpu.VMEM((seq, RET_W), F32), pltpu.VMEM((seq, RET_W), F32),
            pltpu.VMEM((2, H, dk, dk), F32), pltpu.VMEM((2, H, RET_CHUNK, RET_CHUNK), F32),
            pltpu.VMEM((2, H, RET_CHUNK, dk), F32), pltpu.VMEM((2, H, RET_CHUNK, dk), F32),
            pltpu.VMEM((2, H, 8, dk), F32),
        ],
        compiler_params=_cparams(("parallel",)),
        name="ret",
    )(log_gamma, z, z, z, z, gn_g, r0)


def _ssm_group_matrices(lam_re, lam_im, log_step, b_re, b_im, c_re, c_im, d_skip):
    depth, _, G, P = lam_re.shape
    H, T = SSM_GROUP, SSM_CHUNK
    dt = jnp.exp(log_step)[..., None]
    ar, ai = lam_re * dt, lam_im * dt
    mag = jnp.exp(ar)
    lbr, lbi = mag * jnp.cos(ai), mag * jnp.sin(ai)
    den = lam_re * lam_re + lam_im * lam_im
    fr = ((lbr - 1.0) * lam_re + lbi * lam_im) / den
    fi = (lbi * lam_re - (lbr - 1.0) * lam_im) / den
    bbr = fr[..., None] * b_re - fi[..., None] * b_im
    bbi = fr[..., None] * b_im + fi[..., None] * b_re
    m = jnp.arange(T + 1, dtype=F32)[:, None]
    pmag = jnp.exp(ar[..., None, :] * m)
    pwr, pwi = pmag * jnp.cos(ai[..., None, :] * m), pmag * jnp.sin(ai[..., None, :] * m)
    xr = pwr[..., None] * bbr[..., None, :, :] - pwi[..., None] * bbi[..., None, :, :]
    xi = pwr[..., None] * bbi[..., None, :, :] + pwi[..., None] * bbr[..., None, :, :]
    kern = (jnp.einsum('ldghp,ldgmpi->ldgmhi', c_re, xr[..., :T, :, :], precision=HIGHEST)
            - jnp.einsum('ldghp,ldgmpi->ldgmhi', c_im, xi[..., :T, :, :], precision=HIGHEST))
    tok = jnp.arange(T)
    lag = tok[None, :] - tok[:, None]
    kf = kern[:, 0][:, :, jnp.clip(lag, 0, T - 1)] * (lag >= 0)[None, None, :, :, None, None].astype(F32)
    kb = kern[:, 1][:, :, jnp.clip(-lag, 0, T - 1)] * (lag <= 0)[None, None, :, :, None, None].astype(F32)
    tm = (kf + kb).transpose(0, 1, 2, 5, 3, 4)
    skip = (jnp.eye(T, dtype=F32)[None, None, :, None, :, None]
            * jnp.eye(H, dtype=F32)[None, None, None, :, None, :]
            * d_skip.reshape(depth, G, 1, H, 1, 1))
    tmat = (tm + skip).reshape(depth, G, T * H, T * H)
    rev = T - 1 - tok
    ef_r, ef_i = xr[:, 0][:, :, rev], xi[:, 0][:, :, rev]
    eb_r, eb_i = xr[:, 1][:, :, tok], xi[:, 1][:, :, tok]
    bmat = jnp.concatenate([e.transpose(0, 1, 2, 4, 3) for e in (ef_r, ef_i, eb_r, eb_i)], axis=-1)
    bmat = bmat.reshape(depth, G, T * H, 4 * P)

    def entry(d, powers):
        pr = pwr[:, d][:, :, powers][:, :, :, None, :]
        pi = pwi[:, d][:, :, powers][:, :, :, None, :]
        cr, ci = c_re[:, d][:, :, None], c_im[:, d][:, :, None]
        wr, wi = cr * pr - ci * pi, cr * pi + ci * pr
        return wr.transpose(0, 1, 4, 2, 3), -wi.transpose(0, 1, 4, 2, 3)

    cf_r, cf_i = entry(0, tok + 1)
    cb_r, cb_i = entry(1, T - tok)
    cmat = jnp.concatenate([cf_r, cf_i, cb_r, cb_i], axis=2).reshape(depth, G, 4 * P, T * H)
    lr, li = pwr[..., T, :], pwi[..., T, :]
    lam_a = jnp.concatenate([lr[:, 0], lr[:, 0], lr[:, 1], lr[:, 1]], axis=-1)
    lam_b = jnp.concatenate([-li[:, 0], li[:, 0], -li[:, 1], li[:, 1]], axis=-1)
    lamc = jnp.stack([lam_a, lam_b], axis=2)
    return tmat, bmat, cmat, lamc


def _ssm_tile_matrices(tmat, bmat, cmat, lamc):
    depth = tmat.shape[0]
    T, H, GL, NQ, P2 = SSM_CHUNK, SSM_GROUP, SSM_GL, SSM_NQ, 2 * SSM_STATE
    eye2 = jnp.eye(GL, dtype=F32)
    eye_g3 = eye2[None, None, None, :, None, None, :, None]
    eye_g4 = eye2[None, None, None, None, :, None, :, None]
    tg = tmat.reshape(depth, NQ, GL, T, H, T, H).transpose(0, 1, 3, 2, 4, 5, 6)
    tq = (tg[..., None, :] * eye_g3).reshape(depth, NQ, SSM_FLAT, SSM_FLAT)
    bg = bmat.reshape(depth, NQ, GL, T, H, 2, P2).transpose(0, 1, 5, 3, 2, 4, 6)
    bq = (bg[..., None, :] * eye_g4).reshape(depth, NQ, 2, SSM_FLAT, SSM_SW)
    cg = cmat.reshape(depth, NQ, GL, 2, P2, T, H).transpose(0, 1, 3, 2, 4, 5, 6)
    cq = (cg[..., None, :] * eye_g3).reshape(depth, NQ, 2, SSM_SW, SSM_FLAT)
    lamq = lamc.reshape(depth, NQ, GL, 2, 2, P2).transpose(0, 1, 4, 3, 2, 5).reshape(depth, NQ, 2, 2, SSM_SW)
    return tq.astype(BF16), bq.astype(BF16), cq.astype(BF16), lamq


def _gelu_tanh(x):
    return 0.5 * x * (1.0 + jnp.tanh(math.sqrt(2.0 / math.pi) * (x + 0.044715 * (x * x * x))))


def _ssm_kernel(u_ref, t_ref, b_ref, c_ref, lam_ref, h0_ref, y_ref, hfin_ref, xf_scr, hb_scr, st_scr, y_scr,
                *, n_chunks, batch, seq):
    d = pl.program_id(1)
    T = SSM_CHUNK

    @pl.when(d == 0)
    def _():
        for s in range(batch):
            for tau in range(T):
                xf_scr[tau, pl.ds(s, n_chunks, stride=batch), :] = u_ref[pl.ds(s * seq + tau, n_chunks, stride=T), :]

    xf = jnp.concatenate([xf_scr[tau] for tau in range(T)], axis=1).astype(BF16)
    hb_scr[...] = jnp.dot(xf, b_ref[0, 0, 0], preferred_element_type=F32)
    lam_a, lam_b = lam_ref[0, 0, 0, 0:1, :], lam_ref[0, 0, 0, 1:2, :]

    def body(t, s):
        c = jnp.where(d == 0, t, n_chunks - 1 - t)
        rows = pl.ds(pl.multiple_of(c * batch, batch), batch)
        st_scr[rows, :] = s
        swapped = jnp.concatenate(
            [pltpu.roll(s[:, k * LANES:(k + 1) * LANES], SSM_STATE, 1) for k in range(SSM_GL)], axis=1)
        return lam_a * s + lam_b * swapped + hb_scr[rows, :]

    hfin_ref[0, 0] = lax.fori_loop(0, n_chunks, body, h0_ref[0, 0, 0])
    part = jnp.dot(st_scr[...].astype(BF16), c_ref[0, 0, 0], preferred_element_type=F32)

    @pl.when(d == 0)
    def _():
        y_scr[...] = part + jnp.dot(xf, t_ref[0, 0], preferred_element_type=F32)

    @pl.when(d == 1)
    def _():
        y = _gelu_tanh(y_scr[...] + part)
        for tau in range(T):
            xf_scr[tau] = y[:, tau * LANES:(tau + 1) * LANES]
        for s in range(batch):
            for tau in range(T):
                y_ref[pl.ds(s * seq + tau, n_chunks, stride=T), :] = xf_scr[tau, pl.ds(s, n_chunks, stride=batch), :]


def _ssm_call(u, tq, bq, cq, lamq, h0, layer, state_layer, batch, seq):
    n = batch * seq
    n_chunks = seq // SSM_CHUNK
    m = batch * n_chunks
    return pl.pallas_call(
        functools.partial(_ssm_kernel, n_chunks=n_chunks, batch=batch, seq=seq),
        grid=(SSM_NQ, 2),
        in_specs=[
            pl.BlockSpec((n, LANES), lambda q, d: (0, q), pipeline_mode=pl.Buffered(1)),
            pl.BlockSpec((1, 1, SSM_FLAT, SSM_FLAT), lambda q, d: (layer, q, 0, 0)),
            pl.BlockSpec((1, 1, 1, SSM_FLAT, SSM_SW), lambda q, d: (layer, q, d, 0, 0)),
            pl.BlockSpec((1, 1, 1, SSM_SW, SSM_FLAT), lambda q, d: (layer, q, d, 0, 0)),
            pl.BlockSpec((1, 1, 1, 2, SSM_SW), lambda q, d: (layer, q, d, 0, 0)),
            pl.BlockSpec((1, 1, 1, batch, SSM_SW), lambda q, d: (state_layer, q, d, 0, 0)),
        ],
        out_specs=[
            pl.BlockSpec((n, LANES), lambda q, d: (0, q)),
            pl.BlockSpec((1, 1, batch, SSM_SW), lambda q, d: (q, d, 0, 0)),
        ],
        out_shape=[
            jax.ShapeDtypeStruct((n, SSM_WIDTH), F32),
            jax.ShapeDtypeStruct((SSM_NQ, 2, batch, SSM_SW), F32),
        ],
        scratch_shapes=[pltpu.VMEM((SSM_CHUNK, m, LANES), F32), pltpu.VMEM((m, SSM_SW), F32),
                        pltpu.VMEM((m, SSM_SW), F32), pltpu.VMEM((m, SSM_FLAT), F32)],
        compiler_params=_cparams(("arbitrary", "arbitrary")),
        name="ssm",
    )(u, tq, bq, cq, lamq, h0)


def _outproj_kernel(attn_ref, ret_ref, zs_ref, gw_ref, gb_ref, w_ref, x_ref, g1_ref, o_ref, mix_scr):
    j = pl.program_id(1)

    @pl.when(j == 0)
    def _():
        zs = zs_ref[...]
        gl = jnp.dot(zs.astype(BF16), gw_ref[0].astype(BF16), preferred_element_type=F32) + gb_ref[0]
        mix_scr[:, 0:ATTN_W] = attn_ref[...]
        mix_scr[:, ATTN_W:ATTN_W + RET_W] = ret_ref[...]
        mix_scr[:, ATTN_W + RET_W:] = (zs * jax.nn.sigmoid(gl)).astype(BF16)

    acc = jnp.dot(mix_scr[...], w_ref[0].astype(BF16), preferred_element_type=F32)
    o_ref[...] = x_ref[...] + g1_ref[0] * acc


def _outproj_call(attn, ret, zs, glu_w, glu_b, w_out, x2d, gate, layer):
    n, d = x2d.shape
    tm, tn = ROW_TILE, 512
    nb = gate.shape[0]
    gate_idx = (lambda i, j: (i, 0, j)) if nb > 1 else (lambda i, j: (0, 0, j))
    return pl.pallas_call(
        _outproj_kernel,
        grid=(n // tm, d // tn),
        in_specs=[
            pl.BlockSpec((tm, ATTN_W), lambda i, j: (i, 0)),
            pl.BlockSpec((tm, RET_W), lambda i, j: (i, 0)),
            pl.BlockSpec((tm, SSM_WIDTH), lambda i, j: (i, 0)),
            pl.BlockSpec((1, SSM_WIDTH, SSM_WIDTH), lambda i, j: (layer, 0, 0)),
            pl.BlockSpec((1, 1, SSM_WIDTH), lambda i, j: (layer, 0, 0)),
            pl.BlockSpec((1, d, tn), lambda i, j: (layer, 0, j)),
            pl.BlockSpec((tm, tn), lambda i, j: (i, j)),
            pl.BlockSpec((1, 1, tn), gate_idx),
        ],
        out_specs=pl.BlockSpec((tm, tn), lambda i, j: (i, j)),
        out_shape=jax.ShapeDtypeStruct((n, d), F32),
        scratch_shapes=[pltpu.VMEM((tm, d), BF16)],
        compiler_params=_cparams(("parallel", "arbitrary")),
        name="outproj",
    )(attn, ret, zs, glu_w, glu_b, w_out, x2d, gate)


GATHER_ROWS = 512
TOKEN_SPLIT_BITS = 5
TOKEN_SPLIT = 1 << TOKEN_SPLIT_BITS


def _route_kernel(x_ref, g_ref, sh_ref, sc_ref, rw_ref, xg_ref, gs_ref, tok_ref, h_scr, *, cap):
    t = x_ref.shape[1]
    E = N_EXPERTS
    rc = 256
    logits = []
    for c0 in range(0, t, rc):
        x = x_ref[0, c0:c0 + rc, :]
        ms = jnp.mean(x * x, axis=-1, keepdims=True)
        h = x * lax.rsqrt(ms + EPS) * g_ref[...]
        h = h * (1.0 + sc_ref[0]) + sh_ref[0]
        h_scr[c0:c0 + rc, :] = h.astype(BF16)
        logits.append(lax.dot_general(rw_ref[0], h, (((1,), (1,)), ((), ())),
                                      preferred_element_type=F32, precision=HIGHEST))
    lg = jnp.concatenate(logits, axis=1)
    ex = jnp.exp(lg - jnp.max(lg, axis=0, keepdims=True))
    aff = ex / jnp.sum(ex, axis=0, keepdims=True)

    def count_ge(v):
        return jnp.sum(jnp.where(aff >= v, 1.0, 0.0), axis=1, keepdims=True)

    def bisect_bits(_, lohi):
        lo, hi = lohi
        mid = lo + ((hi - lo + 1) >> 1)
        ok = count_ge(lax.bitcast_convert_type(mid, F32)) >= float(cap)
        return jnp.where(ok, mid, lo), jnp.where(ok, hi, mid - 1)

    lo_b, _ = lax.fori_loop(0, 31, bisect_bits,
                            (jnp.zeros((E, 1), jnp.int32), jnp.full((E, 1), 0x7F800000, jnp.int32)))

    def bisect_val(_, lohi):
        lo, hi = lohi
        mid = lo + (hi - lo) * 0.5
        ok = count_ge(mid) >= float(cap)
        return jnp.where(ok, mid, lo), jnp.where(ok, hi, mid)

    _, hi_v = lax.fori_loop(0, 8, bisect_val,
                            (lax.bitcast_convert_type(lo_b, F32), lax.bitcast_convert_type(lo_b + 1, F32)))
    thr = jnp.max(jnp.where(aff < hi_v, aff, 0.0), axis=1, keepdims=True)
    gt = aff > thr
    eq = aff == thr
    need = float(cap) - jnp.sum(jnp.where(gt, 1.0, 0.0), axis=1, keepdims=True)
    tri = (lax.broadcasted_iota(jnp.int32, (t, t), 0) < lax.broadcasted_iota(jnp.int32, (t, t), 1)).astype(BF16)
    eq_rank = jnp.dot(jnp.where(eq, 1.0, 0.0).astype(BF16), tri, preferred_element_type=F32)
    sel = gt | (eq & (eq_rank < need))
    pos = jnp.dot(jnp.where(sel, 1.0, 0.0).astype(BF16), tri, preferred_element_type=F32)
    base = (lax.broadcasted_iota(jnp.int32, (E, t), 0) * cap).astype(F32)
    gpos = jnp.where(sel, pos + base, -1.0)
    tok_i = lax.broadcasted_iota(jnp.int32, (8, t), 1)
    row_i = lax.broadcasted_iota(jnp.int32, (8, t), 0)
    tok_parts = jnp.where(row_i == 0, tok_i >> TOKEN_SPLIT_BITS,
                          jnp.where(row_i == 1, tok_i & (TOKEN_SPLIT - 1), 0))
    tok_parts = tok_parts.astype(F32).astype(BF16)
    n_grp = max(1, GATHER_ROWS // cap)
    for e0 in range(0, E, n_grp):
        onehots = []
        for e in range(e0, e0 + n_grp):
            slot = (e * cap + lax.broadcasted_iota(jnp.int32, (cap, t), 0)).astype(F32)
            hit = gpos[e:e + 1, :] == slot
            gs_ref[e, 0] = jnp.sum(jnp.where(hit, aff[e:e + 1, :], 0.0), axis=1, keepdims=True)
            onehots.append(jnp.where(hit, 1.0, 0.0).astype(BF16))
        onehot = jnp.concatenate(onehots, axis=0)
        xg = jnp.dot(onehot, h_scr[...], preferred_element_type=F32)
        for k in range(n_grp):
            xg_ref[e0 + k, 0] = xg[k * cap:(k + 1) * cap].astype(BF16)
        tk = _nt_dot(tok_parts, onehot)
        tok_ref[0, :, e0 * cap:(e0 + n_grp) * cap] = TOKEN_SPLIT * tk[0:1, :] + tk[1:2, :]


def _route_call(x3d, g, shift, scale, router_wt, layer):
    batch, t, d = x3d.shape
    cap = EC_CAPACITY * t // N_EXPERTS
    nb = shift.shape[0]
    mod_idx = (lambda b: (b, 0, 0)) if nb > 1 else (lambda b: (0, 0, 0))
    return pl.pallas_call(
        functools.partial(_route_kernel, cap=cap),
        grid=(batch,),
        in_specs=[
            pl.BlockSpec((1, t, d), lambda b: (b, 0, 0)),
            pl.BlockSpec((1, d), lambda b: (0, 0)),
            pl.BlockSpec((1, 1, d), mod_idx),
            pl.BlockSpec((1, 1, d), mod_idx),
            pl.BlockSpec((1, N_EXPERTS, d), lambda b: (layer, 0, 0)),
        ],
        out_specs=[
            pl.BlockSpec((N_EXPERTS, 1, cap, d), lambda b: (0, b, 0, 0)),
            pl.BlockSpec((N_EXPERTS, 1, cap, 1), lambda b: (0, b, 0, 0)),
            pl.BlockSpec((1, 1, N_EXPERTS * cap), lambda b: (b, 0, 0)),
        ],
        out_shape=[
            jax.ShapeDtypeStruct((N_EXPERTS, batch, cap, d), BF16),
            jax.ShapeDtypeStruct((N_EXPERTS, batch, cap, 1), F32),
            jax.ShapeDtypeStruct((batch, 1, N_EXPERTS * cap), F32),
        ],
        scratch_shapes=[pltpu.VMEM((t, d), BF16)],
        compiler_params=_cparams(("parallel",)),
        name="route",
    )(x3d, g, shift, scale, router_wt)


EXPERT_UP_TILE = 512
EXPERT_DOWN_TILE = 1024


def _expert_kernel(x_ref, wg_ref, wu_ref, wd_ref, gs_ref, o_ref, h_scr, *, n_up):
    s = pl.program_id(1)
    tf = EXPERT_UP_TILE
    for k in range(n_up):
        @pl.when(s == k)
        def _(k=k):
            x = x_ref[0]
            a = jnp.dot(x, wg_ref[0, 0].astype(BF16), preferred_element_type=F32)
            b = jnp.dot(x, wu_ref[0, 0].astype(BF16), preferred_element_type=F32)
            h_scr[:, k * tf:(k + 1) * tf] = (_silu(a) * b).astype(BF16)

    @pl.when(s >= n_up)
    def _():
        y = jnp.dot(h_scr[...], wd_ref[0, 0].astype(BF16), preferred_element_type=F32)
        o_ref[0] = (y * gs_ref[0]).astype(o_ref.dtype)


def _expert_call(xg, gate_slot, w_gate, w_up, w_down, layer):
    E, m, d = xg.shape
    ff = w_gate.shape[-1]
    tf, tn = EXPERT_UP_TILE, EXPERT_DOWN_TILE
    n_up, n_down = ff // tf, d // tn
    return pl.pallas_call(
        functools.partial(_expert_kernel, n_up=n_up),
        grid=(E, n_up + n_down),
        in_specs=[
            pl.BlockSpec((1, m, d), lambda e, s: (e, 0, 0)),
            pl.BlockSpec((1, 1, d, tf), lambda e, s: (layer, e, 0, jnp.minimum(s, n_up - 1))),
            pl.BlockSpec((1, 1, d, tf), lambda e, s: (layer, e, 0, jnp.minimum(s, n_up - 1))),
            pl.BlockSpec((1, 1, ff, tn), lambda e, s: (layer, e, 0, jnp.maximum(s - n_up, 0))),
            pl.BlockSpec((1, m, 1), lambda e, s: (e, 0, 0)),
        ],
        out_specs=pl.BlockSpec((1, m, tn), lambda e, s: (e, 0, jnp.maximum(s - n_up, 0))),
        out_shape=jax.ShapeDtypeStruct((E, m, d), BF16),
        scratch_shapes=[pltpu.VMEM((m, ff), BF16)],
        compiler_params=_cparams(("parallel", "arbitrary")),
        name="expert",
    )(xg, w_gate, w_up, w_down, gate_slot)


def _combine_kernel(y_ref, tok_ref, x_ref, g2_ref, o_ref, oh_scr):
    j = pl.program_id(1)
    t = x_ref.shape[1]
    n_slots = tok_ref.shape[2]

    @pl.when(j == 0)
    def _():
        tok_i = lax.broadcasted_iota(jnp.int32, (t, n_slots), 0).astype(F32)
        oh_scr[...] = jnp.where(tok_i == tok_ref[0], 1.0, 0.0).astype(BF16)

    y = jnp.concatenate([y_ref[e, 0] for e in range(N_EXPERTS)], axis=0)
    moe = jnp.dot(oh_scr[...], y, preferred_element_type=F32)
    o_ref[0] = x_ref[0] + g2_ref[0] * moe


def _combine_call(y4d, tok, x3d, gate):
    batch, t, d = x3d.shape
    cap = y4d.shape[2]
    tn = 512
    nb = gate.shape[0]
    gate_idx = (lambda b, j: (b, 0, j)) if nb > 1 else (lambda b, j: (0, 0, j))
    return pl.pallas_call(
        _combine_kernel,
        grid=(batch, d // tn),
        in_specs=[
            pl.BlockSpec((N_EXPERTS, 1, cap, tn), lambda b, j: (0, b, 0, j)),
            pl.BlockSpec((1, 1, N_EXPERTS * cap), lambda b, j: (b, 0, 0)),
            pl.BlockSpec((1, t, tn), lambda b, j: (b, 0, j)),
            pl.BlockSpec((1, 1, tn), gate_idx),
        ],
        out_specs=pl.BlockSpec((1, t, tn), lambda b, j: (b, 0, j)),
        out_shape=jax.ShapeDtypeStruct((batch, t, d), F32),
        scratch_shapes=[pltpu.VMEM((t, N_EXPERTS * cap), BF16)],
        compiler_params=_cparams(("parallel", "arbitrary")),
        name="combine",
    )(y4d, tok, x3d, gate)


def _norm_kernel(x_ref, g_ref, o_ref):
    x = x_ref[...]
    o_ref[...] = x * lax.rsqrt(jnp.mean(x * x, axis=-1, keepdims=True) + EPS) * g_ref[...]


def _norm_call(x2d, g):
    n, d = x2d.shape
    tm = 512
    return pl.pallas_call(
        _norm_kernel,
        grid=(n // tm,),
        in_specs=[pl.BlockSpec((tm, d), lambda i: (i, 0)), pl.BlockSpec((1, d), lambda i: (0, 0))],
        out_specs=pl.BlockSpec((tm, d), lambda i: (i, 0)),
        out_shape=jax.ShapeDtypeStruct((n, d), F32),
        compiler_params=_cparams(("parallel",)),
        name="final_norm",
    )(x2d, g)


def _rope_tables(n_tokens):
    n_rows = n_tokens // GRID_W
    row = jnp.repeat(jnp.arange(n_rows), GRID_W).astype(F32)
    col = jnp.tile(jnp.arange(GRID_W), n_rows).astype(F32)
    axis_dim = HEAD_DIM // 2
    inv_freq = ROPE_BASE ** (-jnp.arange(0, axis_dim, 2, dtype=F32) / axis_dim)
    ang_r, ang_c = row[:, None] * inv_freq[None, :], col[:, None] * inv_freq[None, :]
    cos = jnp.concatenate([jnp.cos(ang_r), jnp.cos(ang_r), jnp.cos(ang_c), jnp.cos(ang_c)], axis=1)
    sin = jnp.concatenate([-jnp.sin(ang_r), jnp.sin(ang_r), -jnp.sin(ang_c), jnp.sin(ang_c)], axis=1)
    return cos, sin


def _stream_layer(x3d, layer, mods, w, ssm_mats, rope, ctx):
    batch, seq, d = x3d.shape
    n = batch * seq
    sh1, sc1, g1, sh2, sc2, g2 = mods
    x2d = x3d.reshape(n, d)
    latent = ctx is not None
    z, u = _inproj_call(x2d, w['norm1_g'][layer][None], sh1, sc1, w['w_in'], layer, rope[0], rope[1], latent)
    if latent:
        attn = _attn_lat_call(z, w['attn_sink'][layer], ctx['k'], ctx['v'], layer, batch, seq)
        k_new = v_new = None
        r0, h0, state_layer = ctx['ret'], ctx['ssm'], layer
    else:
        attn, k_new, v_new = _attn_ctx_call(z, w['attn_sink'][layer], batch, seq)
        r0 = jnp.zeros((batch, 1, 2, N_RET_HEADS, RET_DK, RET_DK), F32)
        h0 = jnp.zeros((1, SSM_NQ, 2, batch, SSM_SW), F32)
        state_layer = 0
    ret, r_fin = _ret_call(z, w['ret_log_gamma'][layer], w['ret_gn_g'][layer][None], r0, state_layer, batch, seq)
    zs, h_fin = _ssm_call(u, *ssm_mats, h0, layer, state_layer, batch, seq)
    x1 = _outproj_call(attn, ret, zs, w['ssm_glu_w'], w['ssm_glu_b'], w['w_out'], x2d, g1, layer)
    x1 = x1.reshape(batch, seq, d)
    xg, gate_slot, tok = _route_call(x1, w['norm2_g'][layer][None], sh2, sc2, w['router_wt'], layer)
    cap = xg.shape[2]
    y = _expert_call(xg.reshape(N_EXPERTS, batch * cap, d), gate_slot.reshape(N_EXPERTS, batch * cap, 1),
                     w['moe_w_gate'], w['moe_w_up'], w['moe_w_down'], layer)
    x2 = _combine_call(y.reshape(N_EXPERTS, batch, cap, d), tok, x1, g2)
    return x2, (k_new, v_new, r_fin, h_fin)


def kernel(x_prompt, x_sample, cache_attn_k, cache_attn_v, state_ret, state_ssm_re, state_ssm_im, c, c_ctx, mod_w, mod_b, norm1_g, norm2_g, w_in, w_out, attn_sink, ret_log_gamma, ret_gn_g, ssm_lam_re, ssm_lam_im, ssm_log_step, ssm_b_re, ssm_b_im, ssm_c_re, ssm_c_im, ssm_d, ssm_glu_w, ssm_glu_b, router_w, moe_w_gate, moe_w_up, moe_w_down, final_norm_g):
    depth = w_in.shape[0]
    batch, seq, d = x_prompt.shape
    dec_batch, dec_seq, _ = x_sample.shape
    past = cache_attn_k.shape[2]
    G, P = N_SSM_GROUPS, SSM_STATE
    assert dec_seq == ROW_TILE and ROW_TILE % seq == 0 and (batch * seq) % ROW_TILE == 0

    cv = jnp.concatenate([c_ctx[None, :], c, jnp.zeros((MOD_ROWS - 1 - dec_batch, d), F32)], axis=0)
    mod = _mod_call(cv, mod_w, mod_b)
    ssm_mats = _ssm_tile_matrices(*_ssm_group_matrices(
        ssm_lam_re, ssm_lam_im, ssm_log_step, ssm_b_re, ssm_b_im, ssm_c_re, ssm_c_im, ssm_d))
    rope = _rope_tables(dec_seq)
    no_rope = (jnp.ones((ROW_TILE, LANES), F32), jnp.zeros((ROW_TILE, LANES), F32))
    w = {
        'norm1_g': norm1_g, 'norm2_g': norm2_g, 'w_in': w_in, 'w_out': w_out, 'attn_sink': attn_sink,
        'ret_log_gamma': ret_log_gamma, 'ret_gn_g': ret_gn_g, 'ssm_glu_w': ssm_glu_w,
        'ssm_glu_b': ssm_glu_b.reshape(depth, 1, SSM_WIDTH), 'router_wt': router_w.transpose(0, 2, 1),
        'moe_w_gate': moe_w_gate, 'moe_w_up': moe_w_up, 'moe_w_down': moe_w_down,
    }
    h0_lat = jnp.concatenate([state_ssm_re, state_ssm_im], axis=-1)
    h0_lat = h0_lat.reshape(dec_batch, depth, 2, SSM_NQ, SSM_SW).transpose(1, 3, 2, 0, 4)
    ctx = {
        'k': cache_attn_k.reshape(dec_batch, depth, past, KV_W),
        'v': cache_attn_v.reshape(dec_batch, depth, past, KV_W),
        'ret': state_ret, 'ssm': h0_lat,
    }
    xp, xs = x_prompt, x_sample
    ks, vs, rets, ssms = [], [], [], []
    for layer in range(depth):
        m = mod[layer]
        mods_p = [m[0:1, i * d:(i + 1) * d][:, None, :] for i in range(6)]
        mods_s = [m[1:1 + dec_batch, i * d:(i + 1) * d][:, None, :] for i in range(6)]
        xp, (k_l, v_l, r_l, h_l) = _stream_layer(xp, layer, mods_p, w, ssm_mats, no_rope, None)
        xs, _ = _stream_layer(xs, layer, mods_s, w, ssm_mats, rope, ctx)
        ks.append(k_l.reshape(batch, seq, N_KV_HEADS, HEAD_DIM))
        vs.append(v_l.reshape(batch, seq, N_KV_HEADS, HEAD_DIM))
        rets.append(r_l)
        ssms.append(h_l)
    y_prompt = _norm_call(xp.reshape(batch * seq, d), final_norm_g[None]).reshape(batch, seq, d)
    y_sample = _norm_call(xs.reshape(dec_batch * dec_seq, d), final_norm_g[None]).reshape(dec_batch, dec_seq, d)
    h_all = jnp.stack(ssms, axis=0).reshape(depth, SSM_NQ, 2, batch, SSM_GL, 2 * P)
    h_all = h_all.transpose(3, 0, 2, 1, 4, 5).reshape(batch, depth, 2, G, 2 * P)
    return (y_prompt, y_sample, jnp.stack(ks, axis=1), jnp.stack(vs, axis=1), jnp.stack(rets, axis=1),
            h_all[..., :P], h_all[..., P:])
```

```python
import functools
import math

import jax
import jax.numpy as jnp
from jax import lax
from jax.experimental import pallas as pl
from jax.experimental.pallas import tpu as pltpu

F32 = jnp.float32
BF16 = jnp.bfloat16
HIGHEST = lax.Precision.HIGHEST

D_MODEL = 2048
GRID_W = 64
EPS = 1e-6
HEAD_DIM = 128
ATTN_W = D_MODEL // 2
N_HEADS = ATTN_W // HEAD_DIM
N_KV_HEADS = N_HEADS // 4
Q_PER_KV = N_HEADS // N_KV_HEADS
KV_W = N_KV_HEADS * HEAD_DIM
ATTN_BLOCK = 128
RET_W = D_MODEL // 4
RET_DK = 128
N_RET_HEADS = RET_W // RET_DK
RET_CHUNK = 128
SSM_WIDTH = D_MODEL // 4
SSM_GROUP = 16
N_SSM_GROUPS = SSM_WIDTH // SSM_GROUP
SSM_STATE = 64
N_EXPERTS = 16
EXPERT_FF = D_MODEL // 2
EC_CAPACITY = 2
ROPE_BASE = 10000.0
IN_W = ATTN_W + 2 * KV_W + 4 * RET_W + SSM_WIDTH

LANES = 128
HEAD_Q0 = 0
HEAD_K0 = ATTN_W // LANES
HEAD_V0 = HEAD_K0 + KV_W // LANES
HEAD_RQ0 = HEAD_V0 + KV_W // LANES
HEAD_RK0 = HEAD_RQ0 + RET_W // LANES
HEAD_RV0 = HEAD_RK0 + RET_W // LANES
HEAD_RG0 = HEAD_RV0 + RET_W // LANES
HEAD_U0 = HEAD_RG0 + RET_W // LANES

SSM_CHUNK = 8
SSM_GL = LANES // SSM_GROUP
SSM_NQ = SSM_WIDTH // LANES
SSM_FLAT = SSM_CHUNK * LANES
SSM_SW = SSM_GL * 2 * SSM_STATE

MOD_ROWS = 16
PROJ_ROWS = 512
PROJ_COLS = 512
VMEM_LIMIT = 56 * 1024 * 1024


def _cparams(sem):
    return pltpu.CompilerParams(dimension_semantics=sem, vmem_limit_bytes=VMEM_LIMIT)


def _silu(x):
    return x * jax.nn.sigmoid(x)


def _nt_dot(a, b):
    return lax.dot_general(a, b, (((1,), (1,)), ((), ())), preferred_element_type=F32)


def _tn_dot(a, b):
    return lax.dot_general(a, b, (((0,), (0,)), ((), ())), preferred_element_type=F32)


def _mod_kernel(cv_ref, w_ref, b_ref, o_ref):
    a = _silu(cv_ref[...])
    o_ref[0] = jnp.dot(a, w_ref[0], preferred_element_type=F32, precision=HIGHEST) + b_ref[0]


def _mod_call(cv, mod_w, mod_b):
    depth, d, e = mod_w.shape
    tn = 1024
    return pl.pallas_call(
        _mod_kernel,
        grid=(depth, e // tn),
        in_specs=[
            pl.BlockSpec((MOD_ROWS, d), lambda l, j: (0, 0)),
            pl.BlockSpec((1, d, tn), lambda l, j: (l, 0, j)),
            pl.BlockSpec((1, 1, tn), lambda l, j: (l, 0, j)),
        ],
        out_specs=pl.BlockSpec((1, MOD_ROWS, tn), lambda l, j: (l, 0, j)),
        out_shape=jax.ShapeDtypeStruct((depth, MOD_ROWS, e), F32),
        compiler_params=_cparams(("parallel", "parallel")),
        name="mod",
    )(cv, mod_w, mod_b.reshape(depth, 1, e))


def _rope_head(z, cos, sin_signed):
    lane = lax.broadcasted_iota(jnp.int32, z.shape, 1)
    partner = jnp.where((lane % 64) < 32, pltpu.roll(z, LANES - 32, 1), pltpu.roll(z, 32, 1))
    return z * cos + partner * sin_signed


def _inproj_kernel(x_ref, g_ref, sh_ref, sc_ref, w_ref, cos_ref, sin_ref, z_ref, u_ref, *, use_rope):
    x = x_ref[...]
    ms = jnp.mean(x * x, axis=-1, keepdims=True)
    hn = x * lax.rsqrt(ms + EPS) * g_ref[...]
    h = (hn * (1.0 + sc_ref[0]) + sh_ref[0]).astype(BF16)
    tn = PROJ_COLS
    heads_per_tile = tn // LANES
    for jt in range(IN_W // tn):
        acc = jnp.dot(h, w_ref[0, :, jt * tn:(jt + 1) * tn], preferred_element_type=F32)
        head0 = jt * heads_per_tile
        if head0 >= HEAD_U0:
            u_ref[:, (head0 - HEAD_U0) * LANES:(head0 - HEAD_U0) * LANES + tn] = acc
            continue
        for k in range(heads_per_tile):
            head = head0 + k
            zk = acc[:, k * LANES:(k + 1) * LANES]
            if HEAD_RK0 <= head < HEAD_RV0:
                zk = zk * RET_DK ** -0.5
            if use_rope and (head < HEAD_V0 or HEAD_RQ0 <= head < HEAD_RV0):
                zk = _rope_head(zk, cos_ref[...], sin_ref[...])
            z_ref[:, head * LANES:(head + 1) * LANES] = zk.astype(z_ref.dtype)


def _inproj_call(x2d, g, shift, scale, w_in_bf16, layer, cos, sin_signed, use_rope, seq):
    n, d = x2d.shape
    tm = PROJ_ROWS
    z_w = IN_W - SSM_WIDTH
    nb = shift.shape[0]
    rows_per_mod = n // nb
    tiles_per_seq = seq // tm
    mod_idx = lambda i: (i * tm // rows_per_mod, 0, 0)
    pos_idx = (lambda i: (i % tiles_per_seq, 0)) if use_rope else (lambda i: (0, 0))
    return pl.pallas_call(
        functools.partial(_inproj_kernel, use_rope=use_rope),
        grid=(n // tm,),
        in_specs=[
            pl.BlockSpec((tm, d), lambda i: (i, 0)),
            pl.BlockSpec((1, d), lambda i: (0, 0)),
            pl.BlockSpec((1, 1, d), mod_idx),
            pl.BlockSpec((1, 1, d), mod_idx),
            pl.BlockSpec((1, d, IN_W), lambda i: (layer, 0, 0), pipeline_mode=pl.Buffered(1)),
            pl.BlockSpec((tm, LANES), pos_idx),
            pl.BlockSpec((tm, LANES), pos_idx),
        ],
        out_specs=[
            pl.BlockSpec((tm, z_w), lambda i: (i, 0)),
            pl.BlockSpec((tm, SSM_WIDTH), lambda i: (i, 0)),
        ],
        out_shape=[jax.ShapeDtypeStruct((n, z_w), BF16), jax.ShapeDtypeStruct((n, SSM_WIDTH), F32)],
        compiler_params=_cparams(("parallel",)),
        name="inproj",
    )(x2d, g, shift, scale, w_in_bf16, cos, sin_signed)


def _sink_softmax_rows(s, sink_col):
    m = jnp.maximum(jnp.max(s, axis=-1, keepdims=True), sink_col)
    e = jnp.exp(s - m)
    return e / (jnp.sum(e, axis=-1, keepdims=True) + jnp.exp(sink_col - m))


def _stack_q(q_ref, kh):
    return jnp.concatenate(
        [q_ref[:, (kh * Q_PER_KV + g) * HEAD_DIM:(kh * Q_PER_KV + g + 1) * HEAD_DIM] for g in range(Q_PER_KV)], axis=0)


def _sink_col(sink_ref, kh, rows):
    return jnp.concatenate(
        [jnp.full((rows, 1), sink_ref[kh * Q_PER_KV + g], F32) for g in range(Q_PER_KV)], axis=0)


def _attn_ctx_kernel(sink_ref, q_ref, k_ref, v_ref, o_ref, kn_ref, vn_ref):
    rows = q_ref.shape[0]
    scale = HEAD_DIM ** -0.5
    for kh in range(N_KV_HEADS):
        hs = slice(kh * HEAD_DIM, (kh + 1) * HEAD_DIM)
        s = _nt_dot(_stack_q(q_ref, kh), k_ref[:, hs]) * scale
        p = _sink_softmax_rows(s, _sink_col(sink_ref, kh, rows)).astype(BF16)
        o = jnp.dot(p, v_ref[:, hs], preferred_element_type=F32)
        for g in range(Q_PER_KV):
            c0 = (kh * Q_PER_KV + g) * HEAD_DIM
            o_ref[:, c0:c0 + HEAD_DIM] = o[g * rows:(g + 1) * rows].astype(o_ref.dtype)
    kn_ref[0] = k_ref[...].astype(F32)
    vn_ref[0] = v_ref[...].astype(F32)


def _attn_ctx_call(z, sink, batch, seq):
    n = z.shape[0]
    kcol, vcol = HEAD_K0 * LANES // KV_W, HEAD_V0 * LANES // KV_W
    return pl.pallas_call(
        _attn_ctx_kernel,
        grid=(batch,),
        in_specs=[
            pl.BlockSpec(memory_space=pltpu.SMEM),
            pl.BlockSpec((seq, ATTN_W), lambda b: (b, 0)),
            pl.BlockSpec((seq, KV_W), lambda b: (b, kcol)),
            pl.BlockSpec((seq, KV_W), lambda b: (b, vcol)),
        ],
        out_specs=[
            pl.BlockSpec((seq, ATTN_W), lambda b: (b, 0)),
            pl.BlockSpec((1, seq, KV_W), lambda b: (b, 0, 0)),
            pl.BlockSpec((1, seq, KV_W), lambda b: (b, 0, 0)),
        ],
        out_shape=[
            jax.ShapeDtypeStruct((n, ATTN_W), BF16),
            jax.ShapeDtypeStruct((batch, seq, KV_W), F32),
            jax.ShapeDtypeStruct((batch, seq, KV_W), F32),
        ],
        compiler_params=_cparams(("parallel",)),
        name="attn_ctx",
    )(sink, z, z, z)


def _attn_lat_kernel(sink_ref, q_ref, kp_ref, kc_ref, kn_ref, vp_ref, vc_ref, vn_ref, kx_ref, vx_ref, o_ref, *, nblk):
    i = pl.program_id(1)
    rows = q_ref.shape[0]
    scale = HEAD_DIM ** -0.5
    r = lax.broadcasted_iota(jnp.int32, (rows, ATTN_BLOCK), 0)
    c = lax.broadcasted_iota(jnp.int32, (rows, ATTN_BLOCK), 1)
    neg = jnp.float32(-jnp.inf)
    bias_prev = jnp.where(c >= r, jnp.where(i > 0, 0.0, neg), neg)
    bias_next = jnp.where(c <= r, jnp.where(i < nblk - 1, 0.0, neg), neg)
    n_ctx = kx_ref.shape[2]
    bias = jnp.concatenate([bias_prev, jnp.zeros((rows, ATTN_BLOCK), F32), bias_next,
                            jnp.zeros((rows, n_ctx), F32)], axis=1)
    bias = jnp.concatenate([bias] * Q_PER_KV, axis=0)
    for kh in range(N_KV_HEADS):
        hs = slice(kh * HEAD_DIM, (kh + 1) * HEAD_DIM)
        kcat = jnp.concatenate([kp_ref[:, hs], kc_ref[:, hs], kn_ref[:, hs], kx_ref[0, 0, :, hs].astype(BF16)], axis=0)
        vcat = jnp.concatenate([vp_ref[:, hs], vc_ref[:, hs], vn_ref[:, hs], vx_ref[0, 0, :, hs].astype(BF16)], axis=0)
        s = _nt_dot(_stack_q(q_ref, kh), kcat) * scale + bias
        p = _sink_softmax_rows(s, _sink_col(sink_ref, kh, rows)).astype(BF16)
        o = jnp.dot(p, vcat, preferred_element_type=F32)
        for g in range(Q_PER_KV):
            c0 = (kh * Q_PER_KV + g) * HEAD_DIM
            o_ref[:, c0:c0 + HEAD_DIM] = o[g * rows:(g + 1) * rows].astype(o_ref.dtype)


def _attn_lat_call(z, sink, cache_k, cache_v, layer, batch, seq):
    n = z.shape[0]
    nblk = seq // ATTN_BLOCK
    kcol, vcol = HEAD_K0 * LANES // KV_W, HEAD_V0 * LANES // KV_W
    past = cache_k.shape[2]

    def kv_spec(col, off):
        return pl.BlockSpec((ATTN_BLOCK, KV_W),
                            lambda b, i: (b * nblk + jnp.clip(i + off, 0, nblk - 1), col))

    cache_spec = pl.BlockSpec((1, 1, past, KV_W), lambda b, i: (b, layer, 0, 0))
    return pl.pallas_call(
        functools.partial(_attn_lat_kernel, nblk=nblk),
        grid=(batch, nblk),
        in_specs=[
            pl.BlockSpec(memory_space=pltpu.SMEM),
            pl.BlockSpec((ATTN_BLOCK, ATTN_W), lambda b, i: (b * nblk + i, 0)),
            kv_spec(kcol, -1), kv_spec(kcol, 0), kv_spec(kcol, 1),
            kv_spec(vcol, -1), kv_spec(vcol, 0), kv_spec(vcol, 1),
            cache_spec, cache_spec,
        ],
        out_specs=pl.BlockSpec((ATTN_BLOCK, ATTN_W), lambda b, i: (b * nblk + i, 0)),
        out_shape=jax.ShapeDtypeStruct((n, ATTN_W), BF16),
        compiler_params=_cparams(("parallel", "parallel")),
        name="attn_lat",
    )(sink, z, z, z, z, z, z, z, cache_k, cache_v)


def _ret_kernel(lg_ref, q_ref, k_ref, v_ref, rg_ref, gn_ref, *rest, n_chunks, has_r0):
    r0_ref = rest[0] if has_r0 else None
    o_ref, rfin_ref, of_scr, ob_scr, r_scr, dec_scr, qd_scr, kd_scr, cd_scr = rest[1:] if has_r0 else rest
    C, dk = RET_CHUNK, RET_DK
    ii = lax.broadcasted_iota(jnp.int32, (C, C), 0).astype(F32)
    jj = lax.broadcasted_iota(jnp.int32, (C, C), 1).astype(F32)
    pos = lax.broadcasted_iota(jnp.int32, (C, dk), 0).astype(F32)
    for h in range(N_RET_HEADS):
        for d in range(2):
            lg = lg_ref[d, h]
            diff = (ii - jj) if d == 0 else (jj - ii)
            dec_scr[d, h] = jnp.where(diff >= 0, jnp.exp(jnp.maximum(diff, 0.0) * lg), 0.0)
            if d == 0:
                qd_scr[d, h] = jnp.exp((pos + 1.0) * lg)
                kd_scr[d, h] = jnp.exp((C - 1.0 - pos) * lg)
            else:
                qd_scr[d, h] = jnp.exp((C - pos) * lg)
                kd_scr[d, h] = jnp.exp(pos * lg)
            cd_scr[d, h] = jnp.exp(jnp.full((8, dk), float(C), F32) * lg)
            r_scr[d, h] = r0_ref[0, 0, d, h] if has_r0 else jnp.zeros((dk, dk), F32)

    def body(t, carry):
        for h in range(N_RET_HEADS):
            hs = slice(h * dk, (h + 1) * dk)
            for d in range(2):
                n = t if d == 0 else n_chunks - 1 - t
                rows = pl.ds(pl.multiple_of(n * C, C), C)
                q, k, v = q_ref[rows, hs], k_ref[rows, hs], v_ref[rows, hs]
                r = r_scr[d, h]
                s = _nt_dot(q, k) * dec_scr[d, h]
                intra = jnp.dot(s.astype(BF16), v, preferred_element_type=F32)
                cross = jnp.dot((q.astype(F32) * qd_scr[d, h]).astype(BF16), r.astype(BF16),
                                preferred_element_type=F32)
                kv = _tn_dot((k.astype(F32) * kd_scr[d, h]).astype(BF16), v)
                (of_scr if d == 0 else ob_scr)[rows, hs] = intra + cross
                r_scr[d, h] = cd_scr[d, h, 0:1, :] * r + kv
        return carry

    lax.fori_loop(0, n_chunks, body, 0)
    rfin_ref[0] = r_scr[...]
    for h in range(N_RET_HEADS):
        hs = slice(h * dk, (h + 1) * dk)
        o = of_scr[:, hs] + ob_scr[:, hs]
        mu = jnp.mean(o, axis=-1, keepdims=True)
        var = jnp.mean(jnp.square(o - mu), axis=-1, keepdims=True)
        on = (o - mu) * lax.rsqrt(var + EPS) * gn_ref[:, hs]
        o_ref[:, hs] = (_silu(rg_ref[:, hs].astype(F32)) * on).astype(o_ref.dtype)


def _ret_call(z, log_gamma, gn_g, r0, state_layer, batch, seq):
    n = z.shape[0]
    dk, H = RET_DK, N_RET_HEADS

    def col(head0):
        return pl.BlockSpec((seq, RET_W), lambda b: (b, head0 * LANES // RET_W))

    has_r0 = r0 is not None
    state_specs = [pl.BlockSpec((1, 1, 2, H, dk, dk), lambda b: (b, state_layer, 0, 0, 0, 0))] if has_r0 else []
    state_args = (r0,) if has_r0 else ()
    return pl.pallas_call(
        functools.partial(_ret_kernel, n_chunks=seq // RET_CHUNK, has_r0=has_r0),
        grid=(batch,),
        in_specs=[
            pl.BlockSpec(memory_space=pltpu.SMEM),
            col(HEAD_RQ0), col(HEAD_RK0), col(HEAD_RV0), col(HEAD_RG0),
            pl.BlockSpec((1, RET_W), lambda b: (0, 0)),
        ] + state_specs,
        out_specs=[
            pl.BlockSpec((seq, RET_W), lambda b: (b, 0)),
            pl.BlockSpec((1, 2, H, dk, dk), lambda b: (b, 0, 0, 0, 0)),
        ],
        out_shape=[
            jax.ShapeDtypeStruct((n, RET_W), BF16),
            jax.ShapeDtypeStruct((batch, 2, H, dk, dk), F32),
        ],
        scratch_shapes=[
            pltpu.VMEM((seq, RET_W), F32), pltpu.VMEM((seq, RET_W), F32),
            pltpu.VMEM((2, H, dk, dk), F32), pltpu.VMEM((2, H, RET_CHUNK, RET_CHUNK), F32),
            pltpu.VMEM((2, H, RET_CHUNK, dk), F32), pltpu.VMEM((2, H, RET_CHUNK, dk), F32),
            pltpu.VMEM((2, H, 8, dk), F32),
        ],
        compiler_params=_cparams(("parallel",)),
        name="ret",
    )(log_gamma, z, z, z, z, gn_g, *state_args)


def _ssm_group_matrices(lam_re, lam_im, log_step, b_re, b_im, c_re, c_im, d_skip):
    depth, _, G, P = lam_re.shape
    H, T = SSM_GROUP, SSM_CHUNK
    dt = jnp.exp(log_step)[..., None]
    ar, ai = lam_re * dt, lam_im * dt
    mag = jnp.exp(ar)
    lbr, lbi = mag * jnp.cos(ai), mag * jnp.sin(ai)
    den = lam_re * lam_re + lam_im * lam_im
    fr = ((lbr - 1.0) * lam_re + lbi * lam_im) / den
    fi = (lbi * lam_re - (lbr - 1.0) * lam_im) / den
    bbr = fr[..., None] * b_re - fi[..., None] * b_im
    bbi = fr[..., None] * b_im + fi[..., None] * b_re
    m = jnp.arange(T + 1, dtype=F32)[:, None]
    pmag = jnp.exp(ar[..., None, :] * m)
    pwr, pwi = pmag * jnp.cos(ai[..., None, :] * m), pmag * jnp.sin(ai[..., None, :] * m)
    xr = pwr[..., None] * bbr[..., None, :, :] - pwi[..., None] * bbi[..., None, :, :]
    xi = pwr[..., None] * bbi[..., None, :, :] + pwi[..., None] * bbr[..., None, :, :]
    kern = (jnp.einsum('ldghp,ldgmpi->ldgmhi', c_re, xr[..., :T, :, :], precision=HIGHEST)
            - jnp.einsum('ldghp,ldgmpi->ldgmhi', c_im, xi[..., :T, :, :], precision=HIGHEST))
    tok = jnp.arange(T)
    lag = tok[None, :] - tok[:, None]
    kf = kern[:, 0][:, :, jnp.clip(lag, 0, T - 1)] * (lag >= 0)[None, None, :, :, None, None].astype(F32)
    kb = kern[:, 1][:, :, jnp.clip(-lag, 0, T - 1)] * (lag <= 0)[None, None, :, :, None, None].astype(F32)
    tm = (kf + kb).transpose(0, 1, 2, 5, 3, 4)
    skip = (jnp.eye(T, dtype=F32)[None, None, :, None, :, None]
            * jnp.eye(H, dtype=F32)[None, None, None, :, None, :]
            * d_skip.reshape(depth, G, 1, H, 1, 1))
    tmat = (tm + skip).reshape(depth, G, T * H, T * H)
    rev = T - 1 - tok
    ef_r, ef_i = xr[:, 0][:, :, rev], xi[:, 0][:, :, rev]
    eb_r, eb_i = xr[:, 1][:, :, tok], xi[:, 1][:, :, tok]
    bmat = jnp.concatenate([e.transpose(0, 1, 2, 4, 3) for e in (ef_r, ef_i, eb_r, eb_i)], axis=-1)
    bmat = bmat.reshape(depth, G, T * H, 4 * P)

    def entry(d, powers):
        pr = pwr[:, d][:, :, powers][:, :, :, None, :]
        pi = pwi[:, d][:, :, powers][:, :, :, None, :]
        cr, ci = c_re[:, d][:, :, None], c_im[:, d][:, :, None]
        wr, wi = cr * pr - ci * pi, cr * pi + ci * pr
        return wr.transpose(0, 1, 4, 2, 3), -wi.transpose(0, 1, 4, 2, 3)

    cf_r, cf_i = entry(0, tok + 1)
    cb_r, cb_i = entry(1, T - tok)
    cmat = jnp.concatenate([cf_r, cf_i, cb_r, cb_i], axis=2).reshape(depth, G, 4 * P, T * H)
    lr, li = pwr[..., T, :], pwi[..., T, :]
    lam_a = jnp.concatenate([lr[:, 0], lr[:, 0], lr[:, 1], lr[:, 1]], axis=-1)
    lam_b = jnp.concatenate([-li[:, 0], li[:, 0], -li[:, 1], li[:, 1]], axis=-1)
    lamc = jnp.stack([lam_a, lam_b], axis=2)
    return tmat, bmat, cmat, lamc


def _ssm_tile_matrices(tmat, bmat, cmat, lamc):
    depth = tmat.shape[0]
    T, H, GL, NQ, P2 = SSM_CHUNK, SSM_GROUP, SSM_GL, SSM_NQ, 2 * SSM_STATE
    tm = tmat.reshape(depth, N_SSM_GROUPS, T, H, T, H)
    lag_neg = tm[:, :, 1:, :, 0, :][:, :, ::-1]
    lag_pos = tm[:, :, 0].transpose(0, 1, 3, 2, 4)
    lags = jnp.concatenate([lag_neg, lag_pos], axis=2)
    lags = lags.reshape(depth, NQ, GL, 2 * T - 1, H, H).transpose(0, 1, 3, 2, 4, 5)
    eye = jnp.eye(GL, dtype=F32)[None, None, None, :, None, :, None]
    dmat = (lags[..., None, :] * eye).reshape(depth, NQ, 2 * T - 1, LANES, LANES)
    emat = bmat.reshape(depth, NQ, GL, T, H, 2, P2).transpose(0, 1, 5, 3, 2, 4, 6)
    emat = emat.reshape(depth, NQ, 2, SSM_FLAT, P2)
    wmat = cmat.reshape(depth, NQ, GL, 2, P2, T, H).transpose(0, 1, 3, 4, 5, 2, 6)
    wmat = wmat.reshape(depth, NQ, 2, P2, SSM_FLAT)
    lamq = lamc.reshape(depth, NQ, GL, 2, 2, P2).transpose(0, 1, 4, 3, 2, 5).reshape(depth, NQ, 2, 2, SSM_SW)
    return dmat, emat, wmat, lamq


def _gelu_tanh(x):
    return 0.5 * x * (1.0 + jnp.tanh(math.sqrt(2.0 / math.pi) * (x + 0.044715 * (x * x * x))))


def _ssm_kernel(u_ref, d_ref, e_ref, w_ref, lam_ref, h0_ref, y_ref, hfin_ref,
                xf_scr, hb_scr, st_scr, y_scr, tq_scr, bq_scr, cq_scr, *, n_chunks, batch, seq):
    d = pl.program_id(1)
    T, GL = SSM_CHUNK, SSM_GL

    @pl.when(d == 0)
    def _():
        for s in range(batch):
            for tau in range(T):
                xf_scr[tau, pl.ds(s, n_chunks, stride=batch), :] = u_ref[pl.ds(s * seq + tau, n_chunks, stride=T), :]
        for j in range(T):
            for i in range(T):
                tq_scr[j * LANES:(j + 1) * LANES, i * LANES:(i + 1) * LANES] = d_ref[0, 0, i - j + T - 1].astype(BF16)

    ch_bits = SSM_GROUP.bit_length() - 1
    row_group = (lax.broadcasted_iota(jnp.int32, (SSM_FLAT, LANES), 0) >> ch_bits) & (GL - 1)
    col_group = (lax.broadcasted_iota(jnp.int32, (LANES, SSM_FLAT), 1) >> ch_bits) & (GL - 1)
    emat, wmat = e_ref[0, 0, 0], w_ref[0, 0, 0]
    for g in range(GL):
        bq_scr[:, g * LANES:(g + 1) * LANES] = jnp.where(row_group == g, emat, 0.0).astype(BF16)
        cq_scr[g * LANES:(g + 1) * LANES, :] = jnp.where(col_group == g, wmat, 0.0).astype(BF16)

    xf = jnp.concatenate([xf_scr[tau] for tau in range(T)], axis=1).astype(BF16)
    hb_scr[...] = jnp.dot(xf, bq_scr[...], preferred_element_type=F32)
    lam_a, lam_b = lam_ref[0, 0, 0, 0:1, :], lam_ref[0, 0, 0, 1:2, :]

    def body(t, s):
        c = jnp.where(d == 0, t, n_chunks - 1 - t)
        rows = pl.ds(pl.multiple_of(c * batch, batch), batch)
        st_scr[rows, :] = s
        swapped = jnp.concatenate(
            [pltpu.roll(s[:, k * LANES:(k + 1) * LANES], SSM_STATE, 1) for k in range(SSM_GL)], axis=1)
        return lam_a * s + lam_b * swapped + hb_scr[rows, :]

    hfin_ref[0, 0] = lax.fori_loop(0, n_chunks, body, h0_ref[0, 0, 0])
    part = jnp.dot(st_scr[...].astype(BF16), cq_scr[...], preferred_element_type=F32)

    @pl.when(d == 0)
    def _():
        y_scr[...] = part + jnp.dot(xf, tq_scr[...], preferred_element_type=F32)

    @pl.when(d == 1)
    def _():
        y = _gelu_tanh(y_scr[...] + part)
        for tau in range(T):
            xf_scr[tau] = y[:, tau * LANES:(tau + 1) * LANES]
        for s in range(batch):
            for tau in range(T):
                y_ref[pl.ds(s * seq + tau, n_chunks, stride=T), :] = xf_scr[tau, pl.ds(s, n_chunks, stride=batch), :]


def _ssm_call(u, dmat, emat, wmat, lamq, h0, layer, state_layer, batch, seq):
    n = batch * seq
    n_chunks = seq // SSM_CHUNK
    m = batch * n_chunks
    n_lags = 2 * SSM_CHUNK - 1
    return pl.pallas_call(
        functools.partial(_ssm_kernel, n_chunks=n_chunks, batch=batch, seq=seq),
        grid=(SSM_NQ, 2),
        in_specs=[
            pl.BlockSpec((n, LANES), lambda q, d: (0, q), pipeline_mode=pl.Buffered(1)),
            pl.BlockSpec((1, 1, n_lags, LANES, LANES), lambda q, d: (layer, q, 0, 0, 0)),
            pl.BlockSpec((1, 1, 1, SSM_FLAT, LANES), lambda q, d: (layer, q, d, 0, 0)),
            pl.BlockSpec((1, 1, 1, LANES, SSM_FLAT), lambda q, d: (layer, q, d, 0, 0)),
            pl.BlockSpec((1, 1, 1, 2, SSM_SW), lambda q, d: (layer, q, d, 0, 0)),
            pl.BlockSpec((1, 1, 1, batch, SSM_SW), lambda q, d: (state_layer, q, d, 0, 0)),
        ],
        out_specs=[
            pl.BlockSpec((n, LANES), lambda q, d: (0, q)),
            pl.BlockSpec((1, 1, batch, SSM_SW), lambda q, d: (q, d, 0, 0)),
        ],
        out_shape=[
            jax.ShapeDtypeStruct((n, SSM_WIDTH), F32),
            jax.ShapeDtypeStruct((SSM_NQ, 2, batch, SSM_SW), F32),
        ],
        scratch_shapes=[pltpu.VMEM((SSM_CHUNK, m, LANES), F32), pltpu.VMEM((m, SSM_SW), F32),
                        pltpu.VMEM((m, SSM_SW), F32), pltpu.VMEM((m, SSM_FLAT), F32),
                        pltpu.VMEM((SSM_FLAT, SSM_FLAT), BF16), pltpu.VMEM((SSM_FLAT, SSM_SW), BF16),
                        pltpu.VMEM((SSM_SW, SSM_FLAT), BF16)],
        compiler_params=_cparams(("arbitrary", "arbitrary")),
        name="ssm",
    )(u, dmat, emat, wmat, lamq, h0)


def _outproj_kernel(attn_ref, ret_ref, zs_ref, gw_ref, gb_ref, w_ref, x_ref, g1_ref, o_ref):
    zs = zs_ref[...]
    gl = jnp.dot(zs.astype(BF16), gw_ref[0].astype(BF16), preferred_element_type=F32) + gb_ref[0]
    mix = jnp.concatenate([attn_ref[...], ret_ref[...], (zs * jax.nn.sigmoid(gl)).astype(BF16)], axis=1)
    tn = PROJ_COLS
    for jt in range(o_ref.shape[1] // tn):
        cs = slice(jt * tn, (jt + 1) * tn)
        acc = jnp.dot(mix, w_ref[0, :, cs], preferred_element_type=F32)
        o_ref[:, cs] = x_ref[:, cs] + g1_ref[0, :, cs] * acc


def _outproj_call(attn, ret, zs, glu_w, glu_b, w_out_bf16, x2d, gate, layer):
    n, d = x2d.shape
    tm = PROJ_ROWS
    rows_per_gate = n // gate.shape[0]
    return pl.pallas_call(
        _outproj_kernel,
        grid=(n // tm,),
        in_specs=[
            pl.BlockSpec((tm, ATTN_W), lambda i: (i, 0)),
            pl.BlockSpec((tm, RET_W), lambda i: (i, 0)),
            pl.BlockSpec((tm, SSM_WIDTH), lambda i: (i, 0)),
            pl.BlockSpec((1, SSM_WIDTH, SSM_WIDTH), lambda i: (layer, 0, 0)),
            pl.BlockSpec((1, 1, SSM_WIDTH), lambda i: (layer, 0, 0)),
            pl.BlockSpec((1, d, d), lambda i: (layer, 0, 0), pipeline_mode=pl.Buffered(1)),
            pl.BlockSpec((tm, d), lambda i: (i, 0)),
            pl.BlockSpec((1, 1, d), lambda i: (i * tm // rows_per_gate, 0, 0)),
        ],
        out_specs=pl.BlockSpec((tm, d), lambda i: (i, 0)),
        out_shape=jax.ShapeDtypeStruct((n, d), F32),
        compiler_params=_cparams(("parallel",)),
        name="outproj",
    )(attn, ret, zs, glu_w, glu_b, w_out_bf16, x2d, gate)


GATHER_ROWS = 512
TOKEN_SPLIT_BITS = 5
TOKEN_SPLIT = 1 << TOKEN_SPLIT_BITS


def _route_kernel(x_ref, g_ref, sh_ref, sc_ref, rw_ref, xg_ref, gs_ref, tok_ref, h_scr, *, cap):
    t = x_ref.shape[1]
    E = N_EXPERTS
    rc = 256
    logits = []
    for c0 in range(0, t, rc):
        x = x_ref[0, c0:c0 + rc, :]
        ms = jnp.mean(x * x, axis=-1, keepdims=True)
        h = x * lax.rsqrt(ms + EPS) * g_ref[...]
        h = h * (1.0 + sc_ref[0]) + sh_ref[0]
        h_scr[c0:c0 + rc, :] = h.astype(BF16)
        logits.append(lax.dot_general(rw_ref[0], h, (((1,), (1,)), ((), ())),
                                      preferred_element_type=F32, precision=HIGHEST))
    lg = jnp.concatenate(logits, axis=1)
    ex = jnp.exp(lg - jnp.max(lg, axis=0, keepdims=True))
    aff = ex / jnp.sum(ex, axis=0, keepdims=True)

    def count_ge(v):
        return jnp.sum(jnp.where(aff >= v, 1.0, 0.0), axis=1, keepdims=True)

    def bisect_bits(_, lohi):
        lo, hi = lohi
        mid = lo + ((hi - lo + 1) >> 1)
        ok = count_ge(lax.bitcast_convert_type(mid, F32)) >= float(cap)
        return jnp.where(ok, mid, lo), jnp.where(ok, hi, mid - 1)

    lo_b, _ = lax.fori_loop(0, 31, bisect_bits,
                            (jnp.zeros((E, 1), jnp.int32), jnp.full((E, 1), 0x7F800000, jnp.int32)))

    def bisect_val(_, lohi):
        lo, hi = lohi
        mid = lo + (hi - lo) * 0.5
        ok = count_ge(mid) >= float(cap)
        return jnp.where(ok, mid, lo), jnp.where(ok, hi, mid)

    _, hi_v = lax.fori_loop(0, 8, bisect_val,
                            (lax.bitcast_convert_type(lo_b, F32), lax.bitcast_convert_type(lo_b + 1, F32)))
    thr = jnp.max(jnp.where(aff < hi_v, aff, 0.0), axis=1, keepdims=True)
    gt = aff > thr
    eq = aff == thr
    need = float(cap) - jnp.sum(jnp.where(gt, 1.0, 0.0), axis=1, keepdims=True)
    tri = (lax.broadcasted_iota(jnp.int32, (t, t), 0) < lax.broadcasted_iota(jnp.int32, (t, t), 1)).astype(BF16)
    eq_rank = jnp.dot(jnp.where(eq, 1.0, 0.0).astype(BF16), tri, preferred_element_type=F32)
    sel = gt | (eq & (eq_rank < need))
    pos = jnp.dot(jnp.where(sel, 1.0, 0.0).astype(BF16), tri, preferred_element_type=F32)
    base = (lax.broadcasted_iota(jnp.int32, (E, t), 0) * cap).astype(F32)
    gpos = jnp.where(sel, pos + base, -1.0)
    tok_i = lax.broadcasted_iota(jnp.int32, (8, t), 1)
    row_i = lax.broadcasted_iota(jnp.int32, (8, t), 0)
    tok_parts = jnp.where(row_i == 0, tok_i >> TOKEN_SPLIT_BITS,
                          jnp.where(row_i == 1, tok_i & (TOKEN_SPLIT - 1), 0))
    tok_parts = tok_parts.astype(F32).astype(BF16)
    n_grp = max(1, GATHER_ROWS // cap)
    for e0 in range(0, E, n_grp):
        onehots = []
        for e in range(e0, e0 + n_grp):
            slot = (e * cap + lax.broadcasted_iota(jnp.int32, (cap, t), 0)).astype(F32)
            hit = gpos[e:e + 1, :] == slot
            gs_ref[e, 0] = jnp.sum(jnp.where(hit, aff[e:e + 1, :], 0.0), axis=1, keepdims=True)
            onehots.append(jnp.where(hit, 1.0, 0.0).astype(BF16))
        onehot = jnp.concatenate(onehots, axis=0)
        xg = jnp.dot(onehot, h_scr[...], preferred_element_type=F32)
        for k in range(n_grp):
            xg_ref[e0 + k, 0] = xg[k * cap:(k + 1) * cap].astype(BF16)
        tk = _nt_dot(tok_parts, onehot)
        tok_ref[0, :, e0 * cap:(e0 + n_grp) * cap] = TOKEN_SPLIT * tk[0:1, :] + tk[1:2, :]


def _route_call(x3d, g, shift, scale, router_wt, layer):
    batch, t, d = x3d.shape
    cap = EC_CAPACITY * t // N_EXPERTS
    nb = shift.shape[0]
    mod_idx = (lambda b: (b, 0, 0)) if nb > 1 else (lambda b: (0, 0, 0))
    return pl.pallas_call(
        functools.partial(_route_kernel, cap=cap),
        grid=(batch,),
        in_specs=[
            pl.BlockSpec((1, t, d), lambda b: (b, 0, 0)),
            pl.BlockSpec((1, d), lambda b: (0, 0)),
            pl.BlockSpec((1, 1, d), mod_idx),
            pl.BlockSpec((1, 1, d), mod_idx),
            pl.BlockSpec((1, N_EXPERTS, d), lambda b: (layer, 0, 0)),
        ],
        out_specs=[
            pl.BlockSpec((N_EXPERTS, 1, cap, d), lambda b: (0, b, 0, 0)),
            pl.BlockSpec((N_EXPERTS, 1, cap, 1), lambda b: (0, b, 0, 0)),
            pl.BlockSpec((1, 1, N_EXPERTS * cap), lambda b: (b, 0, 0)),
        ],
        out_shape=[
            jax.ShapeDtypeStruct((N_EXPERTS, batch, cap, d), BF16),
            jax.ShapeDtypeStruct((N_EXPERTS, batch, cap, 1), F32),
            jax.ShapeDtypeStruct((batch, 1, N_EXPERTS * cap), F32),
        ],
        scratch_shapes=[pltpu.VMEM((t, d), BF16)],
        compiler_params=_cparams(("parallel",)),
        name="route",
    )(x3d, g, shift, scale, router_wt)


EXPERT_UP_TILE = 256
EXPERT_DOWN_TILE = 512


def _expert_kernel(xa_ref, xb_ref, wg_ref, wu_ref, wd_ref, ga_ref, gb_ref, oa_ref, ob_ref, h_scr, *, n_up):
    s = pl.program_id(1)
    tf = EXPERT_UP_TILE
    for k in range(n_up):
        @pl.when(s == k)
        def _(k=k):
            wg, wu = wg_ref[0, 0].astype(BF16), wu_ref[0, 0].astype(BF16)
            for i, x_ref in enumerate((xa_ref, xb_ref)):
                x = x_ref[0]
                a = jnp.dot(x, wg, preferred_element_type=F32)
                b = jnp.dot(x, wu, preferred_element_type=F32)
                h_scr[i, :, k * tf:(k + 1) * tf] = (_silu(a) * b).astype(BF16)

    @pl.when(s >= n_up)
    def _():
        wd = wd_ref[0, 0].astype(BF16)
        for i, (g_ref, o_ref) in enumerate(((ga_ref, oa_ref), (gb_ref, ob_ref))):
            y = jnp.dot(h_scr[i], wd, preferred_element_type=F32)
            o_ref[0] = (y * g_ref[0]).astype(o_ref.dtype)


def _expert_call(xg_a, gate_a, xg_b, gate_b, w_gate, w_up, w_down, layer):
    E, m, d = xg_a.shape
    assert xg_b.shape == xg_a.shape
    ff = w_gate.shape[-1]
    tf, tn = EXPERT_UP_TILE, EXPERT_DOWN_TILE
    n_up, n_down = ff // tf, d // tn
    x_spec = pl.BlockSpec((1, m, d), lambda e, s: (e, 0, 0))
    g_spec = pl.BlockSpec((1, m, 1), lambda e, s: (e, 0, 0))
    o_spec = pl.BlockSpec((1, m, tn), lambda e, s: (e, 0, jnp.maximum(s - n_up, 0)))
    up_spec = pl.BlockSpec((1, 1, d, tf), lambda e, s: (layer, e, 0, jnp.minimum(s, n_up - 1)))
    return pl.pallas_call(
        functools.partial(_expert_kernel, n_up=n_up),
        grid=(E, n_up + n_down),
        in_specs=[
            x_spec, x_spec, up_spec, up_spec,
            pl.BlockSpec((1, 1, ff, tn), lambda e, s: (layer, e, 0, jnp.maximum(s - n_up, 0))),
            g_spec, g_spec,
        ],
        out_specs=[o_spec, o_spec],
        out_shape=[jax.ShapeDtypeStruct((E, m, d), BF16), jax.ShapeDtypeStruct((E, m, d), BF16)],
        scratch_shapes=[pltpu.VMEM((2, m, ff), BF16)],
        compiler_params=_cparams(("parallel", "arbitrary")),
        name="expert",
    )(xg_a, xg_b, w_gate, w_up, w_down, gate_a, gate_b)


def _combine_kernel(y_ref, tok_ref, x_ref, g2_ref, o_ref, oh_scr):
    j = pl.program_id(1)
    t = x_ref.shape[1]
    n_slots = tok_ref.shape[2]

    @pl.when(j == 0)
    def _():
        tok_i = lax.broadcasted_iota(jnp.int32, (t, n_slots), 0).astype(F32)
        oh_scr[...] = jnp.where(tok_i == tok_ref[0], 1.0, 0.0).astype(BF16)

    y = jnp.concatenate([y_ref[e, 0] for e in range(N_EXPERTS)], axis=0)
    moe = jnp.dot(oh_scr[...], y, preferred_element_type=F32)
    o_ref[0] = x_ref[0] + g2_ref[0] * moe


def _combine_call(y4d, tok, x3d, gate):
    batch, t, d = x3d.shape
    cap = y4d.shape[2]
    tn = 512
    nb = gate.shape[0]
    gate_idx = (lambda b, j: (b, 0, j)) if nb > 1 else (lambda b, j: (0, 0, j))
    return pl.pallas_call(
        _combine_kernel,
        grid=(batch, d // tn),
        in_specs=[
            pl.BlockSpec((N_EXPERTS, 1, cap, tn), lambda b, j: (0, b, 0, j)),
            pl.BlockSpec((1, 1, N_EXPERTS * cap), lambda b, j: (b, 0, 0)),
            pl.BlockSpec((1, t, tn), lambda b, j: (b, 0, j)),
            pl.BlockSpec((1, 1, tn), gate_idx),
        ],
        out_specs=pl.BlockSpec((1, t, tn), lambda b, j: (b, 0, j)),
        out_shape=jax.ShapeDtypeStruct((batch, t, d), F32),
        scratch_shapes=[pltpu.VMEM((t, N_EXPERTS * cap), BF16)],
        compiler_params=_cparams(("parallel", "arbitrary")),
        name="combine",
    )(y4d, tok, x3d, gate)


def _norm_kernel(x_ref, g_ref, o_ref):
    x = x_ref[...]
    o_ref[...] = x * lax.rsqrt(jnp.mean(x * x, axis=-1, keepdims=True) + EPS) * g_ref[...]


def _norm_call(x2d, g):
    n, d = x2d.shape
    tm = 512
    return pl.pallas_call(
        _norm_kernel,
        grid=(n // tm,),
        in_specs=[pl.BlockSpec((tm, d), lambda i: (i, 0)), pl.BlockSpec((1, d), lambda i: (0, 0))],
        out_specs=pl.BlockSpec((tm, d), lambda i: (i, 0)),
        out_shape=jax.ShapeDtypeStruct((n, d), F32),
        compiler_params=_cparams(("parallel",)),
        name="final_norm",
    )(x2d, g)


def _rope_tables(n_tokens):
    n_rows = n_tokens // GRID_W
    row = jnp.repeat(jnp.arange(n_rows), GRID_W).astype(F32)
    col = jnp.tile(jnp.arange(GRID_W), n_rows).astype(F32)
    axis_dim = HEAD_DIM // 2
    inv_freq = ROPE_BASE ** (-jnp.arange(0, axis_dim, 2, dtype=F32) / axis_dim)
    ang_r, ang_c = row[:, None] * inv_freq[None, :], col[:, None] * inv_freq[None, :]
    cos = jnp.concatenate([jnp.cos(ang_r), jnp.cos(ang_r), jnp.cos(ang_c), jnp.cos(ang_c)], axis=1)
    sin = jnp.concatenate([-jnp.sin(ang_r), jnp.sin(ang_r), -jnp.sin(ang_c), jnp.sin(ang_c)], axis=1)
    return cos, sin


def _mix_and_route(x3d, layer, mods, w, ssm_mats, rope, ctx):
    batch, seq, d = x3d.shape
    n = batch * seq
    sh1, sc1, g1, sh2, sc2, _ = mods
    x2d = x3d.reshape(n, d)
    latent = ctx is not None
    z, u = _inproj_call(x2d, w['norm1_g'][layer][None], sh1, sc1, w['w_in'], layer, rope[0], rope[1], latent, seq)
    if latent:
        attn = _attn_lat_call(z, w['attn_sink'][layer], ctx['k'], ctx['v'], layer, batch, seq)
        k_new = v_new = None
        r0, h0, state_layer = ctx['ret'], ctx['ssm'], layer
    else:
        attn, k_new, v_new = _attn_ctx_call(z, w['attn_sink'][layer], batch, seq)
        r0, h0, state_layer = None, jnp.zeros((1, SSM_NQ, 2, batch, SSM_SW), F32), 0
    ret, r_fin = _ret_call(z, w['ret_log_gamma'][layer], w['ret_gn_g'][layer][None], r0, state_layer, batch, seq)
    zs, h_fin = _ssm_call(u, *ssm_mats, h0, layer, state_layer, batch, seq)
    x1 = _outproj_call(attn, ret, zs, w['ssm_glu_w'], w['ssm_glu_b'], w['w_out'], x2d, g1, layer)
    x1 = x1.reshape(batch, seq, d)
    xg, gate_slot, tok = _route_call(x1, w['norm2_g'][layer][None], sh2, sc2, w['router_wt'], layer)
    return x1, (xg, gate_slot, tok), (k_new, v_new, r_fin, h_fin)


def kernel(x_prompt, x_sample, cache_attn_k, cache_attn_v, state_ret, state_ssm_re, state_ssm_im, c, c_ctx, mod_w, mod_b, norm1_g, norm2_g, w_in, w_out, attn_sink, ret_log_gamma, ret_gn_g, ssm_lam_re, ssm_lam_im, ssm_log_step, ssm_b_re, ssm_b_im, ssm_c_re, ssm_c_im, ssm_d, ssm_glu_w, ssm_glu_b, router_w, moe_w_gate, moe_w_up, moe_w_down, final_norm_g):
    depth = w_in.shape[0]
    batch, seq, d = x_prompt.shape
    dec_batch, dec_seq, _ = x_sample.shape
    past = cache_attn_k.shape[2]
    G, P = N_SSM_GROUPS, SSM_STATE
    assert dec_seq % PROJ_ROWS == 0 and (batch * seq) % PROJ_ROWS == 0

    cv = jnp.concatenate([c_ctx[None, :], c, jnp.zeros((MOD_ROWS - 1 - dec_batch, d), F32)], axis=0)
    mod = _mod_call(cv, mod_w, mod_b)
    ssm_mats = _ssm_tile_matrices(*_ssm_group_matrices(
        ssm_lam_re, ssm_lam_im, ssm_log_step, ssm_b_re, ssm_b_im, ssm_c_re, ssm_c_im, ssm_d))
    rope = _rope_tables(dec_seq)
    no_rope = (jnp.ones((PROJ_ROWS, LANES), F32), jnp.zeros((PROJ_ROWS, LANES), F32))
    w = {
        'norm1_g': norm1_g, 'norm2_g': norm2_g, 'w_in': w_in.astype(BF16), 'w_out': w_out.astype(BF16),
        'attn_sink': attn_sink,
        'ret_log_gamma': ret_log_gamma, 'ret_gn_g': ret_gn_g, 'ssm_glu_w': ssm_glu_w,
        'ssm_glu_b': ssm_glu_b.reshape(depth, 1, SSM_WIDTH), 'router_wt': router_w.transpose(0, 2, 1),
    }
    h0_lat = jnp.concatenate([state_ssm_re, state_ssm_im], axis=-1)
    h0_lat = h0_lat.reshape(dec_batch, depth, 2, SSM_NQ, SSM_SW).transpose(1, 3, 2, 0, 4)
    ctx = {
        'k': cache_attn_k.reshape(dec_batch, depth, past, KV_W),
        'v': cache_attn_v.reshape(dec_batch, depth, past, KV_W),
        'ret': state_ret, 'ssm': h0_lat,
    }
    xp, xs = x_prompt, x_sample
    ks, vs, rets, ssms = [], [], [], []
    for layer in range(depth):
        m = mod[layer]
        mods_p = [m[0:1, i * d:(i + 1) * d][:, None, :] for i in range(6)]
        mods_s = [m[1:1 + dec_batch, i * d:(i + 1) * d][:, None, :] for i in range(6)]
        xp1, (xg_p, gs_p, tok_p), (k_l, v_l, r_l, h_l) = _mix_and_route(xp, layer, mods_p, w, ssm_mats, no_rope, None)
        xs1, (xg_s, gs_s, tok_s), _ = _mix_and_route(xs, layer, mods_s, w, ssm_mats, rope, ctx)
        cap_p, cap_s = xg_p.shape[2], xg_s.shape[2]
        y_p, y_s = _expert_call(
            xg_p.reshape(N_EXPERTS, batch * cap_p, d), gs_p.reshape(N_EXPERTS, batch * cap_p, 1),
            xg_s.reshape(N_EXPERTS, dec_batch * cap_s, d), gs_s.reshape(N_EXPERTS, dec_batch * cap_s, 1),
            moe_w_gate, moe_w_up, moe_w_down, layer)
        xp = _combine_call(y_p.reshape(N_EXPERTS, batch, cap_p, d), tok_p, xp1, mods_p[5])
        xs = _combine_call(y_s.reshape(N_EXPERTS, dec_batch, cap_s, d), tok_s, xs1, mods_s[5])
        ks.append(k_l.reshape(batch, seq, N_KV_HEADS, HEAD_DIM))
        vs.append(v_l.reshape(batch, seq, N_KV_HEADS, HEAD_DIM))
        rets.append(r_l)
        ssms.append(h_l)
    y_prompt = _norm_call(xp.reshape(batch * seq, d), final_norm_g[None]).reshape(batch, seq, d)
    y_sample = _norm_call(xs.reshape(dec_batch * dec_seq, d), final_norm_g[None]).reshape(dec_batch, dec_seq, d)
    h_all = jnp.stack(ssms, axis=0).reshape(depth, SSM_NQ, 2, batch, SSM_GL, 2 * P)
    h_all = h_all.transpose(3, 0, 2, 1, 4, 5).reshape(batch, depth, 2, G, 2 * P)
    return (y_prompt, y_sample, jnp.stack(ks, axis=1), jnp.stack(vs, axis=1), jnp.stack(rets, axis=1),
            h_all[..., :P], h_all[..., P:])
```

```python
import functools
import math

import jax
import jax.numpy as jnp
from jax import lax
from jax.experimental import pallas as pl
from jax.experimental.pallas import tpu as pltpu

F32 = jnp.float32
BF16 = jnp.bfloat16
HIGHEST = lax.Precision.HIGHEST

D_MODEL = 2048
GRID_W = 64
EPS = 1e-6
HEAD_DIM = 128
ATTN_W = D_MODEL // 2
N_HEADS = ATTN_W // HEAD_DIM
N_KV_HEADS = N_HEADS // 4
Q_PER_KV = N_HEADS // N_KV_HEADS
KV_W = N_KV_HEADS * HEAD_DIM
ATTN_BLOCK = 128
RET_W = D_MODEL // 4
RET_DK = 128
N_RET_HEADS = RET_W // RET_DK
RET_CHUNK = 128
SSM_WIDTH = D_MODEL // 4
SSM_GROUP = 16
N_SSM_GROUPS = SSM_WIDTH // SSM_GROUP
SSM_STATE = 64
N_EXPERTS = 16
EXPERT_FF = D_MODEL // 2
EC_CAPACITY = 2
ROPE_BASE = 10000.0
IN_W = ATTN_W + 2 * KV_W + 4 * RET_W + SSM_WIDTH

LANES = 128
HEAD_Q0 = 0
HEAD_K0 = ATTN_W // LANES
HEAD_V0 = HEAD_K0 + KV_W // LANES
HEAD_RQ0 = HEAD_V0 + KV_W // LANES
HEAD_RK0 = HEAD_RQ0 + RET_W // LANES
HEAD_RV0 = HEAD_RK0 + RET_W // LANES
HEAD_RG0 = HEAD_RV0 + RET_W // LANES
HEAD_U0 = HEAD_RG0 + RET_W // LANES

SSM_CHUNK = 8
SSM_GL = LANES // SSM_GROUP
SSM_NQ = SSM_WIDTH // LANES
SSM_FLAT = SSM_CHUNK * LANES
SSM_SW = SSM_GL * 2 * SSM_STATE

MOD_ROWS = 16
PROJ_ROWS = 512
PROJ_COLS = 512
VMEM_LIMIT = 56 * 1024 * 1024


def _cparams(sem):
    return pltpu.CompilerParams(dimension_semantics=sem, vmem_limit_bytes=VMEM_LIMIT)


def _silu(x):
    return x * jax.nn.sigmoid(x)


def _nt_dot(a, b):
    return lax.dot_general(a, b, (((1,), (1,)), ((), ())), preferred_element_type=F32)


def _tn_dot(a, b):
    return lax.dot_general(a, b, (((0,), (0,)), ((), ())), preferred_element_type=F32)


def _mod_kernel(cv_ref, w_ref, b_ref, o_ref):
    a = _silu(cv_ref[...])
    o_ref[0] = jnp.dot(a, w_ref[0], preferred_element_type=F32, precision=HIGHEST) + b_ref[0]


def _mod_call(cv, mod_w, mod_b):
    depth, d, e = mod_w.shape
    tn = 1024
    return pl.pallas_call(
        _mod_kernel,
        grid=(depth, e // tn),
        in_specs=[
            pl.BlockSpec((MOD_ROWS, d), lambda l, j: (0, 0)),
            pl.BlockSpec((1, d, tn), lambda l, j: (l, 0, j)),
            pl.BlockSpec((1, 1, tn), lambda l, j: (l, 0, j)),
        ],
        out_specs=pl.BlockSpec((1, MOD_ROWS, tn), lambda l, j: (l, 0, j)),
        out_shape=jax.ShapeDtypeStruct((depth, MOD_ROWS, e), F32),
        compiler_params=_cparams(("parallel", "parallel")),
        name="mod",
    )(cv, mod_w, mod_b.reshape(depth, 1, e))


def _rope_head(z, cos, sin_signed):
    lane = lax.broadcasted_iota(jnp.int32, z.shape, 1)
    partner = jnp.where((lane % 64) < 32, pltpu.roll(z, LANES - 32, 1), pltpu.roll(z, 32, 1))
    return z * cos + partner * sin_signed


def _inproj_kernel(x_ref, g_ref, sh_ref, sc_ref, w_ref, cos_ref, sin_ref, z_ref, u_ref, *, use_rope):
    x = x_ref[...]
    ms = jnp.mean(x * x, axis=-1, keepdims=True)
    hn = x * lax.rsqrt(ms + EPS) * g_ref[...]
    h = (hn * (1.0 + sc_ref[0]) + sh_ref[0]).astype(BF16)
    tn = PROJ_COLS
    heads_per_tile = tn // LANES
    for jt in range(IN_W // tn):
        acc = jnp.dot(h, w_ref[0, :, jt * tn:(jt + 1) * tn], preferred_element_type=F32)
        head0 = jt * heads_per_tile
        if head0 >= HEAD_U0:
            u_ref[:, (head0 - HEAD_U0) * LANES:(head0 - HEAD_U0) * LANES + tn] = acc
            continue
        for k in range(heads_per_tile):
            head = head0 + k
            zk = acc[:, k * LANES:(k + 1) * LANES]
            if HEAD_RK0 <= head < HEAD_RV0:
                zk = zk * RET_DK ** -0.5
            if use_rope and (head < HEAD_V0 or HEAD_RQ0 <= head < HEAD_RV0):
                zk = _rope_head(zk, cos_ref[...], sin_ref[...])
            z_ref[:, head * LANES:(head + 1) * LANES] = zk.astype(z_ref.dtype)


def _inproj_call(x2d, g, shift, scale, w_in_bf16, layer, cos, sin_signed, use_rope, seq):
    n, d = x2d.shape
    tm = PROJ_ROWS
    z_w = IN_W - SSM_WIDTH
    nb = shift.shape[0]
    rows_per_mod = n // nb
    tiles_per_seq = seq // tm
    mod_idx = lambda i: (i * tm // rows_per_mod, 0, 0)
    pos_idx = (lambda i: (i % tiles_per_seq, 0)) if use_rope else (lambda i: (0, 0))
    return pl.pallas_call(
        functools.partial(_inproj_kernel, use_rope=use_rope),
        grid=(n // tm,),
        in_specs=[
            pl.BlockSpec((tm, d), lambda i: (i, 0)),
            pl.BlockSpec((1, d), lambda i: (0, 0)),
            pl.BlockSpec((1, 1, d), mod_idx),
            pl.BlockSpec((1, 1, d), mod_idx),
            pl.BlockSpec((1, d, IN_W), lambda i: (layer, 0, 0), pipeline_mode=pl.Buffered(1)),
            pl.BlockSpec((tm, LANES), pos_idx),
            pl.BlockSpec((tm, LANES), pos_idx),
        ],
        out_specs=[
            pl.BlockSpec((tm, z_w), lambda i: (i, 0)),
            pl.BlockSpec((tm, SSM_WIDTH), lambda i: (i, 0)),
        ],
        out_shape=[jax.ShapeDtypeStruct((n, z_w), BF16), jax.ShapeDtypeStruct((n, SSM_WIDTH), F32)],
        compiler_params=_cparams(("parallel",)),
        name="inproj",
    )(x2d, g, shift, scale, w_in_bf16, cos, sin_signed)


def _sink_softmax_parts(s, sink_col):
    m = jnp.maximum(jnp.max(s, axis=-1, keepdims=True), sink_col)
    e = jnp.exp(s - m)
    return e, 1.0 / (jnp.sum(e, axis=-1, keepdims=True) + jnp.exp(sink_col - m))


def _stack_q(q_ref, kh):
    return jnp.concatenate(
        [q_ref[:, (kh * Q_PER_KV + g) * HEAD_DIM:(kh * Q_PER_KV + g + 1) * HEAD_DIM] for g in range(Q_PER_KV)], axis=0)


def _sink_col(sink_ref, kh, rows):
    return jnp.concatenate(
        [jnp.full((rows, 1), sink_ref[kh * Q_PER_KV + g], F32) for g in range(Q_PER_KV)], axis=0)


def _attn_ctx_kernel(sink_ref, q_ref, k_ref, v_ref, o_ref, kn_ref, vn_ref):
    rows = q_ref.shape[0]
    scale = HEAD_DIM ** -0.5
    for kh in range(N_KV_HEADS):
        hs = slice(kh * HEAD_DIM, (kh + 1) * HEAD_DIM)
        s = _nt_dot(_stack_q(q_ref, kh), k_ref[:, hs]) * scale
        e, inv = _sink_softmax_parts(s, _sink_col(sink_ref, kh, rows))
        o = jnp.dot(e.astype(BF16), v_ref[:, hs], preferred_element_type=F32) * inv
        for g in range(Q_PER_KV):
            c0 = (kh * Q_PER_KV + g) * HEAD_DIM
            o_ref[:, c0:c0 + HEAD_DIM] = o[g * rows:(g + 1) * rows].astype(o_ref.dtype)
    kn_ref[0] = k_ref[...].astype(F32)
    vn_ref[0] = v_ref[...].astype(F32)


def _attn_ctx_call(z, sink, batch, seq):
    n = z.shape[0]
    kcol, vcol = HEAD_K0 * LANES // KV_W, HEAD_V0 * LANES // KV_W
    return pl.pallas_call(
        _attn_ctx_kernel,
        grid=(batch,),
        in_specs=[
            pl.BlockSpec(memory_space=pltpu.SMEM),
            pl.BlockSpec((seq, ATTN_W), lambda b: (b, 0)),
            pl.BlockSpec((seq, KV_W), lambda b: (b, kcol)),
            pl.BlockSpec((seq, KV_W), lambda b: (b, vcol)),
        ],
        out_specs=[
            pl.BlockSpec((seq, ATTN_W), lambda b: (b, 0)),
            pl.BlockSpec((1, seq, KV_W), lambda b: (b, 0, 0)),
            pl.BlockSpec((1, seq, KV_W), lambda b: (b, 0, 0)),
        ],
        out_shape=[
            jax.ShapeDtypeStruct((n, ATTN_W), BF16),
            jax.ShapeDtypeStruct((batch, seq, KV_W), F32),
            jax.ShapeDtypeStruct((batch, seq, KV_W), F32),
        ],
        compiler_params=_cparams(("parallel",)),
        name="attn_ctx",
    )(sink, z, z, z)


def _attn_lat_kernel(sink_ref, q_ref, kp_ref, kc_ref, kn_ref, vp_ref, vc_ref, vn_ref, kx_ref, vx_ref, o_ref, *, nblk):
    i = pl.program_id(1)
    rows = q_ref.shape[0]
    scale = HEAD_DIM ** -0.5
    r = lax.broadcasted_iota(jnp.int32, (rows, ATTN_BLOCK), 0)
    c = lax.broadcasted_iota(jnp.int32, (rows, ATTN_BLOCK), 1)
    neg = jnp.float32(-jnp.inf)
    bias_prev = jnp.where(c >= r, jnp.where(i > 0, 0.0, neg), neg)
    bias_next = jnp.where(c <= r, jnp.where(i < nblk - 1, 0.0, neg), neg)
    n_ctx = kx_ref.shape[2]
    bias = jnp.concatenate([bias_prev, jnp.zeros((rows, ATTN_BLOCK), F32), bias_next,
                            jnp.zeros((rows, n_ctx), F32)], axis=1)
    bias = jnp.concatenate([bias] * Q_PER_KV, axis=0)
    for kh in range(N_KV_HEADS):
        hs = slice(kh * HEAD_DIM, (kh + 1) * HEAD_DIM)
        kcat = jnp.concatenate([kp_ref[:, hs], kc_ref[:, hs], kn_ref[:, hs], kx_ref[0, 0, :, hs].astype(BF16)], axis=0)
        vcat = jnp.concatenate([vp_ref[:, hs], vc_ref[:, hs], vn_ref[:, hs], vx_ref[0, 0, :, hs].astype(BF16)], axis=0)
        s = _nt_dot(_stack_q(q_ref, kh), kcat) * scale + bias
        e, inv = _sink_softmax_parts(s, _sink_col(sink_ref, kh, rows))
        o = jnp.dot(e.astype(BF16), vcat, preferred_element_type=F32) * inv
        for g in range(Q_PER_KV):
            c0 = (kh * Q_PER_KV + g) * HEAD_DIM
            o_ref[:, c0:c0 + HEAD_DIM] = o[g * rows:(g + 1) * rows].astype(o_ref.dtype)


def _attn_lat_call(z, sink, cache_k, cache_v, layer, batch, seq):
    n = z.shape[0]
    nblk = seq // ATTN_BLOCK
    kcol, vcol = HEAD_K0 * LANES // KV_W, HEAD_V0 * LANES // KV_W
    past = cache_k.shape[2]

    def kv_spec(col, off):
        return pl.BlockSpec((ATTN_BLOCK, KV_W),
                            lambda b, i: (b * nblk + jnp.clip(i + off, 0, nblk - 1), col))

    cache_spec = pl.BlockSpec((1, 1, past, KV_W), lambda b, i: (b, layer, 0, 0))
    return pl.pallas_call(
        functools.partial(_attn_lat_kernel, nblk=nblk),
        grid=(batch, nblk),
        in_specs=[
            pl.BlockSpec(memory_space=pltpu.SMEM),
            pl.BlockSpec((ATTN_BLOCK, ATTN_W), lambda b, i: (b * nblk + i, 0)),
            kv_spec(kcol, -1), kv_spec(kcol, 0), kv_spec(kcol, 1),
            kv_spec(vcol, -1), kv_spec(vcol, 0), kv_spec(vcol, 1),
            cache_spec, cache_spec,
        ],
        out_specs=pl.BlockSpec((ATTN_BLOCK, ATTN_W), lambda b, i: (b * nblk + i, 0)),
        out_shape=jax.ShapeDtypeStruct((n, ATTN_W), BF16),
        compiler_params=_cparams(("parallel", "parallel")),
        name="attn_lat",
    )(sink, z, z, z, z, z, z, z, cache_k, cache_v)


def _ret_kernel(lg_ref, q_ref, k_ref, v_ref, rg_ref, gn_ref, *rest, n_chunks, has_r0):
    r0_ref = rest[0] if has_r0 else None
    o_ref, rfin_ref, of_scr, ob_scr, r_scr, dec_scr, qd_scr, kd_scr, cd_scr = rest[1:] if has_r0 else rest
    C, dk = RET_CHUNK, RET_DK
    ii = lax.broadcasted_iota(jnp.int32, (C, C), 0).astype(F32)
    jj = lax.broadcasted_iota(jnp.int32, (C, C), 1).astype(F32)
    pos = lax.broadcasted_iota(jnp.int32, (C, dk), 0).astype(F32)
    for h in range(N_RET_HEADS):
        for d in range(2):
            lg = lg_ref[d, h]
            diff = (ii - jj) if d == 0 else (jj - ii)
            dec_scr[d, h] = jnp.where(diff >= 0, jnp.exp(jnp.maximum(diff, 0.0) * lg), 0.0)
            if d == 0:
                qd_scr[d, h] = jnp.exp((pos + 1.0) * lg)
                kd_scr[d, h] = jnp.exp((C - 1.0 - pos) * lg)
            else:
                qd_scr[d, h] = jnp.exp((C - pos) * lg)
                kd_scr[d, h] = jnp.exp(pos * lg)
            cd_scr[d, h] = jnp.exp(jnp.full((8, dk), float(C), F32) * lg)
            r_scr[d, h] = r0_ref[0, 0, d, h] if has_r0 else jnp.zeros((dk, dk), F32)

    def body(t, carry):
        for h in range(N_RET_HEADS):
            hs = slice(h * dk, (h + 1) * dk)
            for d in range(2):
                n = t if d == 0 else n_chunks - 1 - t
                rows = pl.ds(pl.multiple_of(n * C, C), C)
                q, k, v = q_ref[rows, hs], k_ref[rows, hs], v_ref[rows, hs]
                r = r_scr[d, h]
                s = _nt_dot(q, k) * dec_scr[d, h]
                intra = jnp.dot(s.astype(BF16), v, preferred_element_type=F32)
                cross = jnp.dot((q.astype(F32) * qd_scr[d, h]).astype(BF16), r.astype(BF16),
                                preferred_element_type=F32)
                kv = _tn_dot((k.astype(F32) * kd_scr[d, h]).astype(BF16), v)
                (of_scr if d == 0 else ob_scr)[rows, hs] = intra + cross
                r_scr[d, h] = cd_scr[d, h, 0:1, :] * r + kv
        return carry

    lax.fori_loop(0, n_chunks, body, 0)
    rfin_ref[0] = r_scr[...]
    for h in range(N_RET_HEADS):
        hs = slice(h * dk, (h + 1) * dk)
        o = of_scr[:, hs] + ob_scr[:, hs]
        mu = jnp.mean(o, axis=-1, keepdims=True)
        var = jnp.mean(jnp.square(o - mu), axis=-1, keepdims=True)
        on = (o - mu) * lax.rsqrt(var + EPS) * gn_ref[:, hs]
        o_ref[:, hs] = (_silu(rg_ref[:, hs].astype(F32)) * on).astype(o_ref.dtype)


def _ret_call(z, log_gamma, gn_g, r0, state_layer, batch, seq):
    n = z.shape[0]
    dk, H = RET_DK, N_RET_HEADS

    def col(head0):
        return pl.BlockSpec((seq, RET_W), lambda b: (b, head0 * LANES // RET_W))

    has_r0 = r0 is not None
    state_specs = [pl.BlockSpec((1, 1, 2, H, dk, dk), lambda b: (b, state_layer, 0, 0, 0, 0))] if has_r0 else []
    state_args = (r0,) if has_r0 else ()
    return pl.pallas_call(
        functools.partial(_ret_kernel, n_chunks=seq // RET_CHUNK, has_r0=has_r0),
        grid=(batch,),
        in_specs=[
            pl.BlockSpec(memory_space=pltpu.SMEM),
            col(HEAD_RQ0), col(HEAD_RK0), col(HEAD_RV0), col(HEAD_RG0),
            pl.BlockSpec((1, RET_W), lambda b: (0, 0)),
        ] + state_specs,
        out_specs=[
            pl.BlockSpec((seq, RET_W), lambda b: (b, 0)),
            pl.BlockSpec((1, 2, H, dk, dk), lambda b: (b, 0, 0, 0, 0)),
        ],
        out_shape=[
            jax.ShapeDtypeStruct((n, RET_W), BF16),
            jax.ShapeDtypeStruct((batch, 2, H, dk, dk), F32),
        ],
        scratch_shapes=[
            pltpu.VMEM((seq, RET_W), F32), pltpu.VMEM((seq, RET_W), F32),
            pltpu.VMEM((2, H, dk, dk), F32), pltpu.VMEM((2, H, RET_CHUNK, RET_CHUNK), F32),
            pltpu.VMEM((2, H, RET_CHUNK, dk), F32), pltpu.VMEM((2, H, RET_CHUNK, dk), F32),
            pltpu.VMEM((2, H, 8, dk), F32),
        ],
        compiler_params=_cparams(("parallel",)),
        name="ret",
    )(log_gamma, z, z, z, z, gn_g, *state_args)


def _ssm_group_matrices(lam_re, lam_im, log_step, b_re, b_im, c_re, c_im, d_skip):
    depth, _, G, P = lam_re.shape
    H, T = SSM_GROUP, SSM_CHUNK
    dt = jnp.exp(log_step)[..., None]
    ar, ai = lam_re * dt, lam_im * dt
    mag = jnp.exp(ar)
    lbr, lbi = mag * jnp.cos(ai), mag * jnp.sin(ai)
    den = lam_re * lam_re + lam_im * lam_im
    fr = ((lbr - 1.0) * lam_re + lbi * lam_im) / den
    fi = (lbi * lam_re - (lbr - 1.0) * lam_im) / den
    bbr = fr[..., None] * b_re - fi[..., None] * b_im
    bbi = fr[..., None] * b_im + fi[..., None] * b_re
    m = jnp.arange(T + 1, dtype=F32)[:, None]
    pmag = jnp.exp(ar[..., None, :] * m)
    pwr, pwi = pmag * jnp.cos(ai[..., None, :] * m), pmag * jnp.sin(ai[..., None, :] * m)
    xr = pwr[..., None] * bbr[..., None, :, :] - pwi[..., None] * bbi[..., None, :, :]
    xi = pwr[..., None] * bbi[..., None, :, :] + pwi[..., None] * bbr[..., None, :, :]
    kern = (jnp.einsum('ldghp,ldgmpi->ldgmhi', c_re, xr[..., :T, :, :], precision=HIGHEST)
            - jnp.einsum('ldghp,ldgmpi->ldgmhi', c_im, xi[..., :T, :, :], precision=HIGHEST))
    tok = jnp.arange(T)
    lag = tok[None, :] - tok[:, None]
    kf = kern[:, 0][:, :, jnp.clip(lag, 0, T - 1)] * (lag >= 0)[None, None, :, :, None, None].astype(F32)
    kb = kern[:, 1][:, :, jnp.clip(-lag, 0, T - 1)] * (lag <= 0)[None, None, :, :, None, None].astype(F32)
    tm = (kf + kb).transpose(0, 1, 2, 5, 3, 4)
    skip = (jnp.eye(T, dtype=F32)[None, None, :, None, :, None]
            * jnp.eye(H, dtype=F32)[None, None, None, :, None, :]
            * d_skip.reshape(depth, G, 1, H, 1, 1))
    tmat = (tm + skip).reshape(depth, G, T * H, T * H)
    rev = T - 1 - tok
    ef_r, ef_i = xr[:, 0][:, :, rev], xi[:, 0][:, :, rev]
    eb_r, eb_i = xr[:, 1][:, :, tok], xi[:, 1][:, :, tok]
    bmat = jnp.concatenate([e.transpose(0, 1, 2, 4, 3) for e in (ef_r, ef_i, eb_r, eb_i)], axis=-1)
    bmat = bmat.reshape(depth, G, T * H, 4 * P)

    def entry(d, powers):
        pr = pwr[:, d][:, :, powers][:, :, :, None, :]
        pi = pwi[:, d][:, :, powers][:, :, :, None, :]
        cr, ci = c_re[:, d][:, :, None], c_im[:, d][:, :, None]
        wr, wi = cr * pr - ci * pi, cr * pi + ci * pr
        return wr.transpose(0, 1, 4, 2, 3), -wi.transpose(0, 1, 4, 2, 3)

    cf_r, cf_i = entry(0, tok + 1)
    cb_r, cb_i = entry(1, T - tok)
    cmat = jnp.concatenate([cf_r, cf_i, cb_r, cb_i], axis=2).reshape(depth, G, 4 * P, T * H)
    lam_t = jnp.stack([pwr[..., T, :], pwi[..., T, :]], axis=2)
    return tmat, bmat, cmat, lam_t


def _ssm_tile_matrices(tmat, bmat, cmat, lam_t):
    depth = tmat.shape[0]
    T, H, GL, NQ, P = SSM_CHUNK, SSM_GROUP, SSM_GL, SSM_NQ, SSM_STATE
    tm = tmat.reshape(depth, N_SSM_GROUPS, T, H, T, H)
    lag_neg = tm[:, :, 1:, :, 0, :][:, :, ::-1]
    lag_pos = tm[:, :, 0].transpose(0, 1, 3, 2, 4)
    lags = jnp.concatenate([lag_neg, lag_pos], axis=2)
    lags = lags.reshape(depth, NQ, GL, 2 * T - 1, H, H).transpose(0, 1, 3, 2, 4, 5)
    eye = jnp.eye(GL, dtype=F32)[None, None, None, :, None, :, None]
    dmat = (lags[..., None, :] * eye).reshape(depth, NQ, 2 * T - 1, LANES, LANES)
    emat = bmat.reshape(depth, NQ, GL, T, H, 2, 2, P).transpose(0, 1, 5, 6, 3, 2, 4, 7)
    emat = emat.reshape(depth, NQ, 2, 2, SSM_FLAT, P)
    emat = jnp.concatenate([emat, emat], axis=-1)
    wmat = cmat.reshape(depth, NQ, GL, 2, 2, P, T, H).transpose(0, 1, 3, 4, 5, 6, 2, 7)
    wmat = wmat.reshape(depth, NQ, 2, 2, P, SSM_FLAT)
    wmat = jnp.concatenate([wmat, wmat], axis=-2)
    lamq = lam_t.reshape(depth, 2, 2, NQ, GL * P).transpose(0, 3, 1, 2, 4)
    return dmat, emat, wmat, lamq


def _gelu_tanh(x):
    return 0.5 * x * (1.0 + jnp.tanh(math.sqrt(2.0 / math.pi) * (x + 0.044715 * (x * x * x))))


def _ssm_kernel(u_ref, d_ref, e_ref, w_ref, lam_ref, h0_ref, y_ref, hfin_ref,
                xf_scr, hb_scr, st_scr, y_scr, tq_scr, bq_scr, cq_scr, *, n_chunks, batch, seq):
    d = pl.program_id(1)
    T, GL = SSM_CHUNK, SSM_GL

    @pl.when(d == 0)
    def _():
        for s in range(batch):
            for tau in range(T):
                xf_scr[tau, pl.ds(s, n_chunks, stride=batch), :] = u_ref[pl.ds(s * seq + tau, n_chunks, stride=T), :]
        for j in range(T):
            for i in range(T):
                tq_scr[j * LANES:(j + 1) * LANES, i * LANES:(i + 1) * LANES] = d_ref[0, 0, i - j + T - 1].astype(BF16)

    ch_bits = SSM_GROUP.bit_length() - 1
    row_group = (lax.broadcasted_iota(jnp.int32, (SSM_FLAT, LANES), 0) >> ch_bits) & (GL - 1)
    col_group = (lax.broadcasted_iota(jnp.int32, (LANES, SSM_FLAT), 1) >> ch_bits) & (GL - 1)
    st_bits = SSM_STATE.bit_length() - 1
    lane_half = lax.broadcasted_iota(jnp.int32, (SSM_FLAT, LANES), 1) >> st_bits
    row_half = lax.broadcasted_iota(jnp.int32, (LANES, SSM_FLAT), 0) >> st_bits
    tiles_per_part = GL * SSM_STATE // LANES
    for part in range(2):
        emat, wmat = e_ref[0, 0, 0, part], w_ref[0, 0, 0, part]
        for m in range(tiles_per_part):
            ts = slice((part * tiles_per_part + m) * LANES, (part * tiles_per_part + m + 1) * LANES)
            bq_scr[:, ts] = jnp.where(row_group == 2 * m + lane_half, emat, 0.0).astype(BF16)
            cq_scr[ts, :] = jnp.where(col_group == 2 * m + row_half, wmat, 0.0).astype(BF16)

    xf = jnp.concatenate([xf_scr[tau] for tau in range(T)], axis=1).astype(BF16)
    hb_scr[...] = jnp.dot(xf, bq_scr[...], preferred_element_type=F32)
    HW = SSM_SW // 2
    lam_re, lam_im = lam_ref[0, 0, 0, 0:1, :], lam_ref[0, 0, 0, 1:2, :]

    def body(t, s):
        s_re, s_im = s
        c = jnp.where(d == 0, t, n_chunks - 1 - t)
        rows = pl.ds(pl.multiple_of(c * batch, batch), batch)
        st_scr[rows, 0:HW] = s_re
        st_scr[rows, HW:SSM_SW] = s_im
        return (lam_re * s_re - lam_im * s_im + hb_scr[rows, 0:HW],
                lam_re * s_im + lam_im * s_re + hb_scr[rows, HW:SSM_SW])

    h0 = h0_ref[0, 0, 0]
    f_re, f_im = lax.fori_loop(0, n_chunks, body, (h0[:, 0:HW], h0[:, HW:SSM_SW]))
    hfin_ref[0, 0] = jnp.concatenate([f_re, f_im], axis=1)
    part = jnp.dot(st_scr[...].astype(BF16), cq_scr[...], preferred_element_type=F32)

    @pl.when(d == 0)
    def _():
        y_scr[...] = part + jnp.dot(xf, tq_scr[...], preferred_element_type=F32)

    @pl.when(d == 1)
    def _():
        y = _gelu_tanh(y_scr[...] + part)
        for tau in range(T):
            xf_scr[tau] = y[:, tau * LANES:(tau + 1) * LANES]
        for s in range(batch):
            for tau in range(T):
                y_ref[pl.ds(s * seq + tau, n_chunks, stride=T), :] = xf_scr[tau, pl.ds(s, n_chunks, stride=batch), :]


def _ssm_call(u, dmat, emat, wmat, lamq, h0, layer, state_layer, batch, seq):
    n = batch * seq
    n_chunks = seq // SSM_CHUNK
    m = batch * n_chunks
    n_lags = 2 * SSM_CHUNK - 1
    return pl.pallas_call(
        functools.partial(_ssm_kernel, n_chunks=n_chunks, batch=batch, seq=seq),
        grid=(SSM_NQ, 2),
        in_specs=[
            pl.BlockSpec((n, LANES), lambda q, d: (0, q), pipeline_mode=pl.Buffered(1)),
            pl.BlockSpec((1, 1, n_lags, LANES, LANES), lambda q, d: (layer, q, 0, 0, 0)),
            pl.BlockSpec((1, 1, 1, 2, SSM_FLAT, LANES), lambda q, d: (layer, q, d, 0, 0, 0)),
            pl.BlockSpec((1, 1, 1, 2, LANES, SSM_FLAT), lambda q, d: (layer, q, d, 0, 0, 0)),
            pl.BlockSpec((1, 1, 1, 2, SSM_SW // 2), lambda q, d: (layer, q, d, 0, 0)),
            pl.BlockSpec((1, 1, 1, batch, SSM_SW), lambda q, d: (state_layer, q, d, 0, 0)),
        ],
        out_specs=[
            pl.BlockSpec((n, LANES), lambda q, d: (0, q)),
            pl.BlockSpec((1, 1, batch, SSM_SW), lambda q, d: (q, d, 0, 0)),
        ],
        out_shape=[
            jax.ShapeDtypeStruct((n, SSM_WIDTH), F32),
            jax.ShapeDtypeStruct((SSM_NQ, 2, batch, SSM_SW), F32),
        ],
        scratch_shapes=[pltpu.VMEM((SSM_CHUNK, m, LANES), F32), pltpu.VMEM((m, SSM_SW), F32),
                        pltpu.VMEM((m, SSM_SW), F32), pltpu.VMEM((m, SSM_FLAT), F32),
                        pltpu.VMEM((SSM_FLAT, SSM_FLAT), BF16), pltpu.VMEM((SSM_FLAT, SSM_SW), BF16),
                        pltpu.VMEM((SSM_SW, SSM_FLAT), BF16)],
        compiler_params=_cparams(("arbitrary", "arbitrary")),
        name="ssm",
    )(u, dmat, emat, wmat, lamq, h0)


def _outproj_kernel(attn_ref, ret_ref, zs_ref, gw_ref, gb_ref, w_ref, x_ref, g1_ref, o_ref):
    zs = zs_ref[...]
    gl = jnp.dot(zs.astype(BF16), gw_ref[0].astype(BF16), preferred_element_type=F32) + gb_ref[0]
    mix = jnp.concatenate([attn_ref[...], ret_ref[...], (zs * jax.nn.sigmoid(gl)).astype(BF16)], axis=1)
    tn = PROJ_COLS
    for jt in range(o_ref.shape[1] // tn):
        cs = slice(jt * tn, (jt + 1) * tn)
        acc = jnp.dot(mix, w_ref[0, :, cs], preferred_element_type=F32)
        o_ref[:, cs] = x_ref[:, cs] + g1_ref[0, :, cs] * acc


def _outproj_call(attn, ret, zs, glu_w, glu_b, w_out_bf16, x2d, gate, layer):
    n, d = x2d.shape
    tm = PROJ_ROWS
    rows_per_gate = n // gate.shape[0]
    return pl.pallas_call(
        _outproj_kernel,
        grid=(n // tm,),
        in_specs=[
            pl.BlockSpec((tm, ATTN_W), lambda i: (i, 0)),
            pl.BlockSpec((tm, RET_W), lambda i: (i, 0)),
            pl.BlockSpec((tm, SSM_WIDTH), lambda i: (i, 0)),
            pl.BlockSpec((1, SSM_WIDTH, SSM_WIDTH), lambda i: (layer, 0, 0)),
            pl.BlockSpec((1, 1, SSM_WIDTH), lambda i: (layer, 0, 0)),
            pl.BlockSpec((1, d, d), lambda i: (layer, 0, 0), pipeline_mode=pl.Buffered(1)),
            pl.BlockSpec((tm, d), lambda i: (i, 0)),
            pl.BlockSpec((1, 1, d), lambda i: (i * tm // rows_per_gate, 0, 0)),
        ],
        out_specs=pl.BlockSpec((tm, d), lambda i: (i, 0)),
        out_shape=jax.ShapeDtypeStruct((n, d), F32),
        compiler_params=_cparams(("parallel",)),
        name="outproj",
    )(attn, ret, zs, glu_w, glu_b, w_out_bf16, x2d, gate)


GATHER_ROWS = 512
TOKEN_SPLIT_BITS = 5
TOKEN_SPLIT = 1 << TOKEN_SPLIT_BITS


def _route_kernel(x_ref, g_ref, sh_ref, sc_ref, rw_ref, xg_ref, gs_ref, tok_ref, h_scr, *, cap):
    t = x_ref.shape[1]
    E = N_EXPERTS
    rc = 256
    logits = []
    for c0 in range(0, t, rc):
        x = x_ref[0, c0:c0 + rc, :]
        ms = jnp.mean(x * x, axis=-1, keepdims=True)
        h = x * lax.rsqrt(ms + EPS) * g_ref[...]
        h = h * (1.0 + sc_ref[0]) + sh_ref[0]
        h_scr[c0:c0 + rc, :] = h.astype(BF16)
        logits.append(lax.dot_general(rw_ref[0], h, (((1,), (1,)), ((), ())),
                                      preferred_element_type=F32, precision=HIGHEST))
    lg = jnp.concatenate(logits, axis=1)
    ex = jnp.exp(lg - jnp.max(lg, axis=0, keepdims=True))
    aff = ex / jnp.sum(ex, axis=0, keepdims=True)

    def count_ge(v):
        return jnp.sum(jnp.where(aff >= v, 1.0, 0.0), axis=1, keepdims=True)

    def bisect_bits(_, lohi):
        lo, hi = lohi
        mid = lo + ((hi - lo + 1) >> 1)
        ok = count_ge(lax.bitcast_convert_type(mid, F32)) >= float(cap)
        return jnp.where(ok, mid, lo), jnp.where(ok, hi, mid - 1)

    lo_b, _ = lax.fori_loop(0, 31, bisect_bits,
                            (jnp.zeros((E, 1), jnp.int32), jnp.full((E, 1), 0x7F800000, jnp.int32)))

    def bisect_val(_, lohi):
        lo, hi = lohi
        mid = lo + (hi - lo) * 0.5
        ok = count_ge(mid) >= float(cap)
        return jnp.where(ok, mid, lo), jnp.where(ok, hi, mid)

    _, hi_v = lax.fori_loop(0, 8, bisect_val,
                            (lax.bitcast_convert_type(lo_b, F32), lax.bitcast_convert_type(lo_b + 1, F32)))
    thr = jnp.max(jnp.where(aff < hi_v, aff, 0.0), axis=1, keepdims=True)
    gt = aff > thr
    eq = aff == thr
    need = float(cap) - jnp.sum(jnp.where(gt, 1.0, 0.0), axis=1, keepdims=True)
    tri = (lax.broadcasted_iota(jnp.int32, (t, t), 0) < lax.broadcasted_iota(jnp.int32, (t, t), 1)).astype(BF16)
    eq_rank = jnp.dot(jnp.where(eq, 1.0, 0.0).astype(BF16), tri, preferred_element_type=F32)
    sel = gt | (eq & (eq_rank < need))
    pos = jnp.dot(jnp.where(sel, 1.0, 0.0).astype(BF16), tri, preferred_element_type=F32)
    base = (lax.broadcasted_iota(jnp.int32, (E, t), 0) * cap).astype(F32)
    gpos = jnp.where(sel, pos + base, -1.0)
    tok_i = lax.broadcasted_iota(jnp.int32, (8, t), 1)
    row_i = lax.broadcasted_iota(jnp.int32, (8, t), 0)
    tok_parts = jnp.where(row_i == 0, tok_i >> TOKEN_SPLIT_BITS,
                          jnp.where(row_i == 1, tok_i & (TOKEN_SPLIT - 1), 0))
    tok_parts = tok_parts.astype(F32).astype(BF16)
    n_grp = max(1, GATHER_ROWS // cap)
    for e0 in range(0, E, n_grp):
        onehots = []
        for e in range(e0, e0 + n_grp):
            slot = (e * cap + lax.broadcasted_iota(jnp.int32, (cap, t), 0)).astype(F32)
            hit = gpos[e:e + 1, :] == slot
            gs_ref[e, 0] = jnp.sum(jnp.where(hit, aff[e:e + 1, :], 0.0), axis=1, keepdims=True)
            onehots.append(jnp.where(hit, 1.0, 0.0).astype(BF16))
        onehot = jnp.concatenate(onehots, axis=0)
        xg = jnp.dot(onehot, h_scr[...], preferred_element_type=F32)
        for k in range(n_grp):
            xg_ref[e0 + k, 0] = xg[k * cap:(k + 1) * cap].astype(BF16)
        tk = _nt_dot(tok_parts, onehot)
        tok_ref[0, :, e0 * cap:(e0 + n_grp) * cap] = TOKEN_SPLIT * tk[0:1, :] + tk[1:2, :]


def _route_call(x3d, g, shift, scale, router_wt, layer):
    batch, t, d = x3d.shape
    cap = EC_CAPACITY * t // N_EXPERTS
    nb = shift.shape[0]
    mod_idx = (lambda b: (b, 0, 0)) if nb > 1 else (lambda b: (0, 0, 0))
    return pl.pallas_call(
        functools.partial(_route_kernel, cap=cap),
        grid=(batch,),
        in_specs=[
            pl.BlockSpec((1, t, d), lambda b: (b, 0, 0)),
            pl.BlockSpec((1, d), lambda b: (0, 0)),
            pl.BlockSpec((1, 1, d), mod_idx),
            pl.BlockSpec((1, 1, d), mod_idx),
            pl.BlockSpec((1, N_EXPERTS, d), lambda b: (layer, 0, 0)),
        ],
        out_specs=[
            pl.BlockSpec((N_EXPERTS, 1, cap, d), lambda b: (0, b, 0, 0)),
            pl.BlockSpec((N_EXPERTS, 1, cap, 1), lambda b: (0, b, 0, 0)),
            pl.BlockSpec((1, 1, N_EXPERTS * cap), lambda b: (b, 0, 0)),
        ],
        out_shape=[
            jax.ShapeDtypeStruct((N_EXPERTS, batch, cap, d), BF16),
            jax.ShapeDtypeStruct((N_EXPERTS, batch, cap, 1), F32),
            jax.ShapeDtypeStruct((batch, 1, N_EXPERTS * cap), F32),
        ],
        scratch_shapes=[pltpu.VMEM((t, d), BF16)],
        compiler_params=_cparams(("parallel",)),
        name="route",
    )(x3d, g, shift, scale, router_wt)


EXPERT_UP_TILE = 512
EXPERT_DOWN_TILE = 512


def _expert_kernel(xa_ref, xb_ref, wg_ref, wu_ref, wd_ref, ga_ref, gb_ref, oa_ref, ob_ref, h_scr, *, n_up):
    s = pl.program_id(1)
    tf = EXPERT_UP_TILE
    for k in range(n_up):
        @pl.when(s == k)
        def _(k=k):
            wg, wu = wg_ref[0, 0].astype(BF16), wu_ref[0, 0].astype(BF16)
            for i, x_ref in enumerate((xa_ref, xb_ref)):
                x = x_ref[0]
                a = jnp.dot(x, wg, preferred_element_type=F32)
                b = jnp.dot(x, wu, preferred_element_type=F32)
                h_scr[i, :, k * tf:(k + 1) * tf] = (_silu(a) * b).astype(BF16)

    @pl.when(s >= n_up)
    def _():
        wd = wd_ref[0, 0].astype(BF16)
        for i, (g_ref, o_ref) in enumerate(((ga_ref, oa_ref), (gb_ref, ob_ref))):
            y = jnp.dot(h_scr[i], wd, preferred_element_type=F32)
            o_ref[0] = (y * g_ref[0]).astype(o_ref.dtype)


def _expert_call(xg_a, gate_a, xg_b, gate_b, w_gate, w_up, w_down, layer):
    E, m, d = xg_a.shape
    assert xg_b.shape == xg_a.shape
    ff = w_gate.shape[-1]
    tf, tn = EXPERT_UP_TILE, EXPERT_DOWN_TILE
    n_up, n_down = ff // tf, d // tn
    x_spec = pl.BlockSpec((1, m, d), lambda e, s: (e, 0, 0))
    g_spec = pl.BlockSpec((1, m, 1), lambda e, s: (e, 0, 0))
    o_spec = pl.BlockSpec((1, m, tn), lambda e, s: (e, 0, jnp.maximum(s - n_up, 0)))
    up_spec = pl.BlockSpec((1, 1, d, tf), lambda e, s: (layer, e, 0, jnp.minimum(s, n_up - 1)))
    return pl.pallas_call(
        functools.partial(_expert_kernel, n_up=n_up),
        grid=(E, n_up + n_down),
        in_specs=[
            x_spec, x_spec, up_spec, up_spec,
            pl.BlockSpec((1, 1, ff, tn), lambda e, s: (layer, e, 0, jnp.maximum(s - n_up, 0))),
            g_spec, g_spec,
        ],
        out_specs=[o_spec, o_spec],
        out_shape=[jax.ShapeDtypeStruct((E, m, d), BF16), jax.ShapeDtypeStruct((E, m, d), BF16)],
        scratch_shapes=[pltpu.VMEM((2, m, ff), BF16)],
        compiler_params=_cparams(("parallel", "arbitrary")),
        name="expert",
    )(xg_a, xg_b, w_gate, w_up, w_down, gate_a, gate_b)


COMBINE_TILE_ELEMS = 1024 * 1024


def _combine_kernel(y_ref, tok_ref, x_ref, g2_ref, o_ref, oh_scr):
    j = pl.program_id(1)
    t = x_ref.shape[1]
    n_slots = tok_ref.shape[2]

    @pl.when(j == 0)
    def _():
        tok_i = lax.broadcasted_iota(jnp.int32, (t, n_slots), 0).astype(F32)
        oh_scr[...] = jnp.where(tok_i == tok_ref[0], 1.0, 0.0).astype(BF16)

    y = jnp.concatenate([y_ref[e, 0] for e in range(N_EXPERTS)], axis=0)
    moe = jnp.dot(oh_scr[...], y, preferred_element_type=F32)
    o_ref[0] = x_ref[0] + g2_ref[0] * moe


def _combine_call(y4d, tok, x3d, gate):
    batch, t, d = x3d.shape
    cap = y4d.shape[2]
    tn = min(d, COMBINE_TILE_ELEMS // t)
    nb = gate.shape[0]
    gate_idx = (lambda b, j: (b, 0, j)) if nb > 1 else (lambda b, j: (0, 0, j))
    return pl.pallas_call(
        _combine_kernel,
        grid=(batch, d // tn),
        in_specs=[
            pl.BlockSpec((N_EXPERTS, 1, cap, tn), lambda b, j: (0, b, 0, j)),
            pl.BlockSpec((1, 1, N_EXPERTS * cap), lambda b, j: (b, 0, 0)),
            pl.BlockSpec((1, t, tn), lambda b, j: (b, 0, j)),
            pl.BlockSpec((1, 1, tn), gate_idx),
        ],
        out_specs=pl.BlockSpec((1, t, tn), lambda b, j: (b, 0, j)),
        out_shape=jax.ShapeDtypeStruct((batch, t, d), F32),
        scratch_shapes=[pltpu.VMEM((t, N_EXPERTS * cap), BF16)],
        compiler_params=_cparams(("parallel", "arbitrary")),
        name="combine",
    )(y4d, tok, x3d, gate)


def _norm_kernel(x_ref, g_ref, o_ref):
    x = x_ref[...]
    o_ref[...] = x * lax.rsqrt(jnp.mean(x * x, axis=-1, keepdims=True) + EPS) * g_ref[...]


def _norm_call(x2d, g):
    n, d = x2d.shape
    tm = 512
    return pl.pallas_call(
        _norm_kernel,
        grid=(n // tm,),
        in_specs=[pl.BlockSpec((tm, d), lambda i: (i, 0)), pl.BlockSpec((1, d), lambda i: (0, 0))],
        out_specs=pl.BlockSpec((tm, d), lambda i: (i, 0)),
        out_shape=jax.ShapeDtypeStruct((n, d), F32),
        compiler_params=_cparams(("parallel",)),
        name="final_norm",
    )(x2d, g)


def _rope_tables(n_tokens):
    n_rows = n_tokens // GRID_W
    row = jnp.repeat(jnp.arange(n_rows), GRID_W).astype(F32)
    col = jnp.tile(jnp.arange(GRID_W), n_rows).astype(F32)
    axis_dim = HEAD_DIM // 2
    inv_freq = ROPE_BASE ** (-jnp.arange(0, axis_dim, 2, dtype=F32) / axis_dim)
    ang_r, ang_c = row[:, None] * inv_freq[None, :], col[:, None] * inv_freq[None, :]
    cos = jnp.concatenate([jnp.cos(ang_r), jnp.cos(ang_r), jnp.cos(ang_c), jnp.cos(ang_c)], axis=1)
    sin = jnp.concatenate([-jnp.sin(ang_r), jnp.sin(ang_r), -jnp.sin(ang_c), jnp.sin(ang_c)], axis=1)
    return cos, sin


def _mix_and_route(x3d, layer, mods, w, ssm_mats, rope, ctx):
    batch, seq, d = x3d.shape
    n = batch * seq
    sh1, sc1, g1, sh2, sc2, _ = mods
    x2d = x3d.reshape(n, d)
    latent = ctx is not None
    z, u = _inproj_call(x2d, w['norm1_g'][layer][None], sh1, sc1, w['w_in'], layer, rope[0], rope[1], latent, seq)
    if latent:
        attn = _attn_lat_call(z, w['attn_sink'][layer], ctx['k'], ctx['v'], layer, batch, seq)
        k_new = v_new = None
        r0, h0, state_layer = ctx['ret'], ctx['ssm'], layer
    else:
        attn, k_new, v_new = _attn_ctx_call(z, w['attn_sink'][layer], batch, seq)
        r0, h0, state_layer = None, jnp.zeros((1, SSM_NQ, 2, batch, SSM_SW), F32), 0
    ret, r_fin = _ret_call(z, w['ret_log_gamma'][layer], w['ret_gn_g'][layer][None], r0, state_layer, batch, seq)
    zs, h_fin = _ssm_call(u, *ssm_mats, h0, layer, state_layer, batch, seq)
    x1 = _outproj_call(attn, ret, zs, w['ssm_glu_w'], w['ssm_glu_b'], w['w_out'], x2d, g1, layer)
    x1 = x1.reshape(batch, seq, d)
    xg, gate_slot, tok = _route_call(x1, w['norm2_g'][layer][None], sh2, sc2, w['router_wt'], layer)
    return x1, (xg, gate_slot, tok), (k_new, v_new, r_fin, h_fin)


def kernel(x_prompt, x_sample, cache_attn_k, cache_attn_v, state_ret, state_ssm_re, state_ssm_im, c, c_ctx, mod_w, mod_b, norm1_g, norm2_g, w_in, w_out, attn_sink, ret_log_gamma, ret_gn_g, ssm_lam_re, ssm_lam_im, ssm_log_step, ssm_b_re, ssm_b_im, ssm_c_re, ssm_c_im, ssm_d, ssm_glu_w, ssm_glu_b, router_w, moe_w_gate, moe_w_up, moe_w_down, final_norm_g):
    depth = w_in.shape[0]
    batch, seq, d = x_prompt.shape
    dec_batch, dec_seq, _ = x_sample.shape
    past = cache_attn_k.shape[2]
    G, P = N_SSM_GROUPS, SSM_STATE
    assert dec_seq % PROJ_ROWS == 0 and (batch * seq) % PROJ_ROWS == 0

    cv = jnp.concatenate([c_ctx[None, :], c, jnp.zeros((MOD_ROWS - 1 - dec_batch, d), F32)], axis=0)
    mod = _mod_call(cv, mod_w, mod_b)
    ssm_mats = _ssm_tile_matrices(*_ssm_group_matrices(
        ssm_lam_re, ssm_lam_im, ssm_log_step, ssm_b_re, ssm_b_im, ssm_c_re, ssm_c_im, ssm_d))
    rope = _rope_tables(dec_seq)
    no_rope = (jnp.ones((PROJ_ROWS, LANES), F32), jnp.zeros((PROJ_ROWS, LANES), F32))
    w = {
        'norm1_g': norm1_g, 'norm2_g': norm2_g, 'w_in': w_in.astype(BF16), 'w_out': w_out.astype(BF16),
        'attn_sink': attn_sink,
        'ret_log_gamma': ret_log_gamma, 'ret_gn_g': ret_gn_g, 'ssm_glu_w': ssm_glu_w,
        'ssm_glu_b': ssm_glu_b.reshape(depth, 1, SSM_WIDTH), 'router_wt': router_w.transpose(0, 2, 1),
    }
    h0_lat = jnp.concatenate([state_ssm_re.reshape(dec_batch, depth, 2, SSM_NQ, SSM_SW // 2),
                              state_ssm_im.reshape(dec_batch, depth, 2, SSM_NQ, SSM_SW // 2)], axis=-1)
    h0_lat = h0_lat.transpose(1, 3, 2, 0, 4)
    ctx = {
        'k': cache_attn_k.reshape(dec_batch, depth, past, KV_W),
        'v': cache_attn_v.reshape(dec_batch, depth, past, KV_W),
        'ret': state_ret, 'ssm': h0_lat,
    }
    xp, xs = x_prompt, x_sample
    ks, vs, rets, ssms = [], [], [], []
    for layer in range(depth):
        m = mod[layer]
        mods_p = [m[0:1, i * d:(i + 1) * d][:, None, :] for i in range(6)]
        mods_s = [m[1:1 + dec_batch, i * d:(i + 1) * d][:, None, :] for i in range(6)]
        xp1, (xg_p, gs_p, tok_p), (k_l, v_l, r_l, h_l) = _mix_and_route(xp, layer, mods_p, w, ssm_mats, no_rope, None)
        xs1, (xg_s, gs_s, tok_s), _ = _mix_and_route(xs, layer, mods_s, w, ssm_mats, rope, ctx)
        cap_p, cap_s = xg_p.shape[2], xg_s.shape[2]
        y_p, y_s = _expert_call(
            xg_p.reshape(N_EXPERTS, batch * cap_p, d), gs_p.reshape(N_EXPERTS, batch * cap_p, 1),
            xg_s.reshape(N_EXPERTS, dec_batch * cap_s, d), gs_s.reshape(N_EXPERTS, dec_batch * cap_s, 1),
            moe_w_gate, moe_w_up, moe_w_down, layer)
        xp = _combine_call(y_p.reshape(N_EXPERTS, batch, cap_p, d), tok_p, xp1, mods_p[5])
        xs = _combine_call(y_s.reshape(N_EXPERTS, dec_batch, cap_s, d), tok_s, xs1, mods_s[5])
        ks.append(k_l.reshape(batch, seq, N_KV_HEADS, HEAD_DIM))
        vs.append(v_l.reshape(batch, seq, N_KV_HEADS, HEAD_DIM))
        rets.append(r_l)
        ssms.append(h_l)
    y_prompt = _norm_call(xp.reshape(batch * seq, d), final_norm_g[None]).reshape(batch, seq, d)
    y_sample = _norm_call(xs.reshape(dec_batch * dec_seq, d), final_norm_g[None]).reshape(dec_batch, dec_seq, d)
    h_all = jnp.stack(ssms, axis=0).reshape(depth, SSM_NQ, 2, batch, 2, SSM_GL, P)
    h_all = h_all.transpose(4, 3, 0, 2, 1, 5, 6).reshape(2, batch, depth, 2, G, P)
    return (y_prompt, y_sample, jnp.stack(ks, axis=1), jnp.stack(vs, axis=1), jnp.stack(rets, axis=1),
            h_all[0], h_all[1])
```

```python
import functools
import math

import jax
import jax.numpy as jnp
from jax import lax
from jax.experimental import pallas as pl
from jax.experimental.pallas import tpu as pltpu

F32 = jnp.float32
BF16 = jnp.bfloat16
HIGHEST = lax.Precision.HIGHEST

D_MODEL = 2048
GRID_W = 64
EPS = 1e-6
HEAD_DIM = 128
ATTN_W = D_MODEL // 2
N_HEADS = ATTN_W // HEAD_DIM
N_KV_HEADS = N_HEADS // 4
Q_PER_KV = N_HEADS // N_KV_HEADS
KV_W = N_KV_HEADS * HEAD_DIM
ATTN_BLOCK = 128
RET_W = D_MODEL // 4
RET_DK = 128
N_RET_HEADS = RET_W // RET_DK
RET_CHUNK = 128
SSM_WIDTH = D_MODEL // 4
SSM_GROUP = 16
N_SSM_GROUPS = SSM_WIDTH // SSM_GROUP
SSM_STATE = 64
N_EXPERTS = 16
EXPERT_FF = D_MODEL // 2
EC_CAPACITY = 2
ROPE_BASE = 10000.0
IN_W = ATTN_W + 2 * KV_W + 4 * RET_W + SSM_WIDTH

LANES = 128
HEAD_Q0 = 0
HEAD_K0 = ATTN_W // LANES
HEAD_V0 = HEAD_K0 + KV_W // LANES
HEAD_RQ0 = HEAD_V0 + KV_W // LANES
HEAD_RK0 = HEAD_RQ0 + RET_W // LANES
HEAD_RV0 = HEAD_RK0 + RET_W // LANES
HEAD_RG0 = HEAD_RV0 + RET_W // LANES
HEAD_U0 = HEAD_RG0 + RET_W // LANES

SSM_CHUNK = 8
SSM_GL = LANES // SSM_GROUP
SSM_NQ = SSM_WIDTH // LANES
SSM_FLAT = SSM_CHUNK * LANES
SSM_SW = SSM_GL * 2 * SSM_STATE

MOD_ROWS = 16
PROJ_ROWS = 512
PROJ_COLS = 512
VMEM_LIMIT = 56 * 1024 * 1024


def _cparams(sem):
    return pltpu.CompilerParams(dimension_semantics=sem, vmem_limit_bytes=VMEM_LIMIT)


def _silu(x):
    return x * jax.nn.sigmoid(x)


def _nt_dot(a, b):
    return lax.dot_general(a, b, (((1,), (1,)), ((), ())), preferred_element_type=F32)


def _tn_dot(a, b):
    return lax.dot_general(a, b, (((0,), (0,)), ((), ())), preferred_element_type=F32)


def _mod_kernel(cv_ref, w_ref, b_ref, o_ref):
    a = _silu(cv_ref[...])
    o_ref[0] = jnp.dot(a, w_ref[0], preferred_element_type=F32, precision=HIGHEST) + b_ref[0]


def _mod_call(cv, mod_w, mod_b):
    depth, d, e = mod_w.shape
    tn = 1024
    return pl.pallas_call(
        _mod_kernel,
        grid=(depth, e // tn),
        in_specs=[
            pl.BlockSpec((MOD_ROWS, d), lambda l, j: (0, 0)),
            pl.BlockSpec((1, d, tn), lambda l, j: (l, 0, j)),
            pl.BlockSpec((1, 1, tn), lambda l, j: (l, 0, j)),
        ],
        out_specs=pl.BlockSpec((1, MOD_ROWS, tn), lambda l, j: (l, 0, j)),
        out_shape=jax.ShapeDtypeStruct((depth, MOD_ROWS, e), F32),
        compiler_params=_cparams(("parallel", "parallel")),
        name="mod",
    )(cv, mod_w, mod_b.reshape(depth, 1, e))


def _rope_head(z, cos, sin_signed):
    lane = lax.broadcasted_iota(jnp.int32, z.shape, 1)
    partner = jnp.where((lane % 64) < 32, pltpu.roll(z, LANES - 32, 1), pltpu.roll(z, 32, 1))
    return z * cos + partner * sin_signed


def _inproj_kernel(x_ref, g_ref, sh_ref, sc_ref, w_ref, cos_ref, sin_ref, z_ref, u_ref, *, use_rope):
    x = x_ref[...]
    ms = jnp.mean(x * x, axis=-1, keepdims=True)
    hn = x * lax.rsqrt(ms + EPS) * g_ref[...]
    h = (hn * (1.0 + sc_ref[0]) + sh_ref[0]).astype(BF16)
    tn = PROJ_COLS
    heads_per_tile = tn // LANES
    for jt in range(IN_W // tn):
        acc = jnp.dot(h, w_ref[0, :, jt * tn:(jt + 1) * tn], preferred_element_type=F32)
        head0 = jt * heads_per_tile
        if head0 >= HEAD_U0:
            u_ref[:, (head0 - HEAD_U0) * LANES:(head0 - HEAD_U0) * LANES + tn] = acc
            continue
        for k in range(heads_per_tile):
            head = head0 + k
            zk = acc[:, k * LANES:(k + 1) * LANES]
            if HEAD_RK0 <= head < HEAD_RV0:
                zk = zk * RET_DK ** -0.5
            if use_rope and (head < HEAD_V0 or HEAD_RQ0 <= head < HEAD_RV0):
                zk = _rope_head(zk, cos_ref[...], sin_ref[...])
            z_ref[:, head * LANES:(head + 1) * LANES] = zk.astype(z_ref.dtype)


def _inproj_call(x2d, g, shift, scale, w_in_bf16, layer, cos, sin_signed, use_rope, seq):
    n, d = x2d.shape
    tm = PROJ_ROWS
    z_w = IN_W - SSM_WIDTH
    nb = shift.shape[0]
    rows_per_mod = n // nb
    tiles_per_seq = seq // tm
    mod_idx = lambda i: (i * tm // rows_per_mod, 0, 0)
    pos_idx = (lambda i: (i % tiles_per_seq, 0)) if use_rope else (lambda i: (0, 0))
    return pl.pallas_call(
        functools.partial(_inproj_kernel, use_rope=use_rope),
        grid=(n // tm,),
        in_specs=[
            pl.BlockSpec((tm, d), lambda i: (i, 0)),
            pl.BlockSpec((1, d), lambda i: (0, 0)),
            pl.BlockSpec((1, 1, d), mod_idx),
            pl.BlockSpec((1, 1, d), mod_idx),
            pl.BlockSpec((1, d, IN_W), lambda i: (layer, 0, 0), pipeline_mode=pl.Buffered(1)),
            pl.BlockSpec((tm, LANES), pos_idx),
            pl.BlockSpec((tm, LANES), pos_idx),
        ],
        out_specs=[
            pl.BlockSpec((tm, z_w), lambda i: (i, 0)),
            pl.BlockSpec((tm, SSM_WIDTH), lambda i: (i, 0)),
        ],
        out_shape=[jax.ShapeDtypeStruct((n, z_w), BF16), jax.ShapeDtypeStruct((n, SSM_WIDTH), F32)],
        compiler_params=_cparams(("parallel",)),
        name="inproj",
    )(x2d, g, shift, scale, w_in_bf16, cos, sin_signed)


def _sink_softmax_parts(s, sink_col):
    m = jnp.maximum(jnp.max(s, axis=-1, keepdims=True), sink_col)
    e = jnp.exp(s - m)
    return e, 1.0 / (jnp.sum(e, axis=-1, keepdims=True) + jnp.exp(sink_col - m))


def _stack_q(q_ref, kh):
    return jnp.concatenate(
        [q_ref[:, (kh * Q_PER_KV + g) * HEAD_DIM:(kh * Q_PER_KV + g + 1) * HEAD_DIM] for g in range(Q_PER_KV)], axis=0)


def _sink_col(sink_ref, kh, rows):
    return jnp.concatenate(
        [jnp.full((rows, 1), sink_ref[kh * Q_PER_KV + g], F32) for g in range(Q_PER_KV)], axis=0)


def _attn_ctx_kernel(sink_ref, q_ref, k_ref, v_ref, o_ref, kn_ref, vn_ref):
    rows = q_ref.shape[0]
    scale = HEAD_DIM ** -0.5
    for kh in range(N_KV_HEADS):
        hs = slice(kh * HEAD_DIM, (kh + 1) * HEAD_DIM)
        s = _nt_dot(_stack_q(q_ref, kh), k_ref[:, hs]) * scale
        e, inv = _sink_softmax_parts(s, _sink_col(sink_ref, kh, rows))
        o = jnp.dot(e.astype(BF16), v_ref[:, hs], preferred_element_type=F32) * inv
        for g in range(Q_PER_KV):
            c0 = (kh * Q_PER_KV + g) * HEAD_DIM
            o_ref[:, c0:c0 + HEAD_DIM] = o[g * rows:(g + 1) * rows].astype(o_ref.dtype)
    kn_ref[0] = k_ref[...].astype(F32)
    vn_ref[0] = v_ref[...].astype(F32)


def _attn_ctx_call(z, sink, batch, seq):
    n = z.shape[0]
    kcol, vcol = HEAD_K0 * LANES // KV_W, HEAD_V0 * LANES // KV_W
    return pl.pallas_call(
        _attn_ctx_kernel,
        grid=(batch,),
        in_specs=[
            pl.BlockSpec(memory_space=pltpu.SMEM),
            pl.BlockSpec((seq, ATTN_W), lambda b: (b, 0)),
            pl.BlockSpec((seq, KV_W), lambda b: (b, kcol)),
            pl.BlockSpec((seq, KV_W), lambda b: (b, vcol)),
        ],
        out_specs=[
            pl.BlockSpec((seq, ATTN_W), lambda b: (b, 0)),
            pl.BlockSpec((1, seq, KV_W), lambda b: (b, 0, 0)),
            pl.BlockSpec((1, seq, KV_W), lambda b: (b, 0, 0)),
        ],
        out_shape=[
            jax.ShapeDtypeStruct((n, ATTN_W), BF16),
            jax.ShapeDtypeStruct((batch, seq, KV_W), F32),
            jax.ShapeDtypeStruct((batch, seq, KV_W), F32),
        ],
        compiler_params=_cparams(("parallel",)),
        name="attn_ctx",
    )(sink, z, z, z)


def _attn_lat_kernel(sink_ref, q_ref, kp_ref, kc_ref, kn_ref, vp_ref, vc_ref, vn_ref, kx_ref, vx_ref, o_ref, *, nblk):
    i = pl.program_id(1)
    rows = q_ref.shape[0]
    scale = HEAD_DIM ** -0.5
    r = lax.broadcasted_iota(jnp.int32, (rows, ATTN_BLOCK), 0)
    c = lax.broadcasted_iota(jnp.int32, (rows, ATTN_BLOCK), 1)
    neg = jnp.float32(-jnp.inf)
    bias_prev = jnp.where(c >= r, jnp.where(i > 0, 0.0, neg), neg)
    bias_next = jnp.where(c <= r, jnp.where(i < nblk - 1, 0.0, neg), neg)
    n_ctx = kx_ref.shape[2]
    bias = jnp.concatenate([bias_prev, jnp.zeros((rows, ATTN_BLOCK), F32), bias_next,
                            jnp.zeros((rows, n_ctx), F32)], axis=1)
    bias = jnp.concatenate([bias] * Q_PER_KV, axis=0)
    for kh in range(N_KV_HEADS):
        hs = slice(kh * HEAD_DIM, (kh + 1) * HEAD_DIM)
        kcat = jnp.concatenate([kp_ref[:, hs], kc_ref[:, hs], kn_ref[:, hs], kx_ref[0, 0, :, hs].astype(BF16)], axis=0)
        vcat = jnp.concatenate([vp_ref[:, hs], vc_ref[:, hs], vn_ref[:, hs], vx_ref[0, 0, :, hs].astype(BF16)], axis=0)
        s = _nt_dot(_stack_q(q_ref, kh), kcat) * scale + bias
        e, inv = _sink_softmax_parts(s, _sink_col(sink_ref, kh, rows))
        o = jnp.dot(e.astype(BF16), vcat, preferred_element_type=F32) * inv
        for g in range(Q_PER_KV):
            c0 = (kh * Q_PER_KV + g) * HEAD_DIM
            o_ref[:, c0:c0 + HEAD_DIM] = o[g * rows:(g + 1) * rows].astype(o_ref.dtype)


def _attn_lat_call(z, sink, cache_k, cache_v, layer, batch, seq):
    n = z.shape[0]
    nblk = seq // ATTN_BLOCK
    kcol, vcol = HEAD_K0 * LANES // KV_W, HEAD_V0 * LANES // KV_W
    past = cache_k.shape[2]

    def kv_spec(col, off):
        return pl.BlockSpec((ATTN_BLOCK, KV_W),
                            lambda b, i: (b * nblk + jnp.clip(i + off, 0, nblk - 1), col))

    cache_spec = pl.BlockSpec((1, 1, past, KV_W), lambda b, i: (b, layer, 0, 0))
    return pl.pallas_call(
        functools.partial(_attn_lat_kernel, nblk=nblk),
        grid=(batch, nblk),
        in_specs=[
            pl.BlockSpec(memory_space=pltpu.SMEM),
            pl.BlockSpec((ATTN_BLOCK, ATTN_W), lambda b, i: (b * nblk + i, 0)),
            kv_spec(kcol, -1), kv_spec(kcol, 0), kv_spec(kcol, 1),
            kv_spec(vcol, -1), kv_spec(vcol, 0), kv_spec(vcol, 1),
            cache_spec, cache_spec,
        ],
        out_specs=pl.BlockSpec((ATTN_BLOCK, ATTN_W), lambda b, i: (b * nblk + i, 0)),
        out_shape=jax.ShapeDtypeStruct((n, ATTN_W), BF16),
        compiler_params=_cparams(("parallel", "parallel")),
        name="attn_lat",
    )(sink, z, z, z, z, z, z, z, cache_k, cache_v)


def _ret_kernel(lg_ref, q_ref, k_ref, v_ref, rg_ref, gn_ref, *rest, n_chunks, has_r0):
    r0_ref = rest[0] if has_r0 else None
    o_ref, rfin_ref, of_scr, ob_scr, r_scr, dec_scr, qd_scr, kd_scr, cd_scr = rest[1:] if has_r0 else rest
    C, dk = RET_CHUNK, RET_DK
    ii = lax.broadcasted_iota(jnp.int32, (C, C), 0).astype(F32)
    jj = lax.broadcasted_iota(jnp.int32, (C, C), 1).astype(F32)
    pos = lax.broadcasted_iota(jnp.int32, (C, dk), 0).astype(F32)
    for h in range(N_RET_HEADS):
        for d in range(2):
            lg = lg_ref[d, h]
            diff = (ii - jj) if d == 0 else (jj - ii)
            dec_scr[d, h] = jnp.where(diff >= 0, jnp.exp(jnp.maximum(diff, 0.0) * lg), 0.0)
            if d == 0:
                qd_scr[d, h] = jnp.exp((pos + 1.0) * lg)
                kd_scr[d, h] = jnp.exp((C - 1.0 - pos) * lg)
            else:
                qd_scr[d, h] = jnp.exp((C - pos) * lg)
                kd_scr[d, h] = jnp.exp(pos * lg)
            cd_scr[d, h] = jnp.exp(jnp.full((8, dk), float(C), F32) * lg)
            r_scr[d, h] = r0_ref[0, 0, d, h] if has_r0 else jnp.zeros((dk, dk), F32)

    def body(t, carry):
        for h in range(N_RET_HEADS):
            hs = slice(h * dk, (h + 1) * dk)
            for d in range(2):
                n = t if d == 0 else n_chunks - 1 - t
                rows = pl.ds(pl.multiple_of(n * C, C), C)
                q, k, v = q_ref[rows, hs], k_ref[rows, hs], v_ref[rows, hs]
                r = r_scr[d, h]
                s = _nt_dot(q, k) * dec_scr[d, h]
                intra = jnp.dot(s.astype(BF16), v, preferred_element_type=F32)
                cross = jnp.dot((q.astype(F32) * qd_scr[d, h]).astype(BF16), r.astype(BF16),
                                preferred_element_type=F32)
                kv = _tn_dot((k.astype(F32) * kd_scr[d, h]).astype(BF16), v)
                (of_scr if d == 0 else ob_scr)[rows, hs] = intra + cross
                r_scr[d, h] = cd_scr[d, h, 0:1, :] * r + kv
        return carry

    lax.fori_loop(0, n_chunks, body, 0)
    rfin_ref[0] = r_scr[...]
    for h in range(N_RET_HEADS):
        hs = slice(h * dk, (h + 1) * dk)
        o = of_scr[:, hs] + ob_scr[:, hs]
        mu = jnp.mean(o, axis=-1, keepdims=True)
        var = jnp.mean(jnp.square(o - mu), axis=-1, keepdims=True)
        on = (o - mu) * lax.rsqrt(var + EPS) * gn_ref[:, hs]
        o_ref[:, hs] = (_silu(rg_ref[:, hs].astype(F32)) * on).astype(o_ref.dtype)


def _ret_call(z, log_gamma, gn_g, r0, state_layer, batch, seq):
    n = z.shape[0]
    dk, H = RET_DK, N_RET_HEADS

    def col(head0):
        return pl.BlockSpec((seq, RET_W), lambda b: (b, head0 * LANES // RET_W))

    has_r0 = r0 is not None
    state_specs = [pl.BlockSpec((1, 1, 2, H, dk, dk), lambda b: (b, state_layer, 0, 0, 0, 0))] if has_r0 else []
    state_args = (r0,) if has_r0 else ()
    return pl.pallas_call(
        functools.partial(_ret_kernel, n_chunks=seq // RET_CHUNK, has_r0=has_r0),
        grid=(batch,),
        in_specs=[
            pl.BlockSpec(memory_space=pltpu.SMEM),
            col(HEAD_RQ0), col(HEAD_RK0), col(HEAD_RV0), col(HEAD_RG0),
            pl.BlockSpec((1, RET_W), lambda b: (0, 0)),
        ] + state_specs,
        out_specs=[
            pl.BlockSpec((seq, RET_W), lambda b: (b, 0)),
            pl.BlockSpec((1, 2, H, dk, dk), lambda b: (b, 0, 0, 0, 0)),
        ],
        out_shape=[
            jax.ShapeDtypeStruct((n, RET_W), BF16),
            jax.ShapeDtypeStruct((batch, 2, H, dk, dk), F32),
        ],
        scratch_shapes=[
            pltpu.VMEM((seq, RET_W), F32), pltpu.VMEM((seq, RET_W), F32),
            pltpu.VMEM((2, H, dk, dk), F32), pltpu.VMEM((2, H, RET_CHUNK, RET_CHUNK), F32),
            pltpu.VMEM((2, H, RET_CHUNK, dk), F32), pltpu.VMEM((2, H, RET_CHUNK, dk), F32),
            pltpu.VMEM((2, H, 8, dk), F32),
        ],
        compiler_params=_cparams(("parallel",)),
        name="ret",
    )(log_gamma, z, z, z, z, gn_g, *state_args)


def _ssm_group_matrices(lam_re, lam_im, log_step, b_re, b_im, c_re, c_im, d_skip):
    depth, _, G, P = lam_re.shape
    H, T = SSM_GROUP, SSM_CHUNK
    dt = jnp.exp(log_step)[..., None]
    ar, ai = lam_re * dt, lam_im * dt
    mag = jnp.exp(ar)
    lbr, lbi = mag * jnp.cos(ai), mag * jnp.sin(ai)
    den = lam_re * lam_re + lam_im * lam_im
    fr = ((lbr - 1.0) * lam_re + lbi * lam_im) / den
    fi = (lbi * lam_re - (lbr - 1.0) * lam_im) / den
    bbr = fr[..., None] * b_re - fi[..., None] * b_im
    bbi = fr[..., None] * b_im + fi[..., None] * b_re
    m = jnp.arange(T + 1, dtype=F32)[:, None]
    pmag = jnp.exp(ar[..., None, :] * m)
    pwr, pwi = pmag * jnp.cos(ai[..., None, :] * m), pmag * jnp.sin(ai[..., None, :] * m)
    xr = pwr[..., None] * bbr[..., None, :, :] - pwi[..., None] * bbi[..., None, :, :]
    xi = pwr[..., None] * bbi[..., None, :, :] + pwi[..., None] * bbr[..., None, :, :]
    kern = (jnp.einsum('ldghp,ldgmpi->ldgmhi', c_re, xr[..., :T, :, :], precision=HIGHEST)
            - jnp.einsum('ldghp,ldgmpi->ldgmhi', c_im, xi[..., :T, :, :], precision=HIGHEST))
    tok = jnp.arange(T)
    lag = tok[None, :] - tok[:, None]
    kf = kern[:, 0][:, :, jnp.clip(lag, 0, T - 1)] * (lag >= 0)[None, None, :, :, None, None].astype(F32)
    kb = kern[:, 1][:, :, jnp.clip(-lag, 0, T - 1)] * (lag <= 0)[None, None, :, :, None, None].astype(F32)
    tm = (kf + kb).transpose(0, 1, 2, 5, 3, 4)
    skip = (jnp.eye(T, dtype=F32)[None, None, :, None, :, None]
            * jnp.eye(H, dtype=F32)[None, None, None, :, None, :]
            * d_skip.reshape(depth, G, 1, H, 1, 1))
    tmat = (tm + skip).reshape(depth, G, T * H, T * H)
    rev = T - 1 - tok
    ef_r, ef_i = xr[:, 0][:, :, rev], xi[:, 0][:, :, rev]
    eb_r, eb_i = xr[:, 1][:, :, tok], xi[:, 1][:, :, tok]
    bmat = jnp.concatenate([e.transpose(0, 1, 2, 4, 3) for e in (ef_r, ef_i, eb_r, eb_i)], axis=-1)
    bmat = bmat.reshape(depth, G, T * H, 4 * P)

    def entry(d, powers):
        pr = pwr[:, d][:, :, powers][:, :, :, None, :]
        pi = pwi[:, d][:, :, powers][:, :, :, None, :]
        cr, ci = c_re[:, d][:, :, None], c_im[:, d][:, :, None]
        wr, wi = cr * pr - ci * pi, cr * pi + ci * pr
        return wr.transpose(0, 1, 4, 2, 3), -wi.transpose(0, 1, 4, 2, 3)

    cf_r, cf_i = entry(0, tok + 1)
    cb_r, cb_i = entry(1, T - tok)
    cmat = jnp.concatenate([cf_r, cf_i, cb_r, cb_i], axis=2).reshape(depth, G, 4 * P, T * H)
    lam_t = jnp.stack([pwr[..., T, :], pwi[..., T, :]], axis=2)
    return tmat, bmat, cmat, lam_t


def _ssm_tile_matrices(tmat, bmat, cmat, lam_t):
    depth = tmat.shape[0]
    T, H, GL, NQ, P = SSM_CHUNK, SSM_GROUP, SSM_GL, SSM_NQ, SSM_STATE
    tm = tmat.reshape(depth, N_SSM_GROUPS, T, H, T, H)
    lag_neg = tm[:, :, 1:, :, 0, :][:, :, ::-1]
    lag_pos = tm[:, :, 0].transpose(0, 1, 3, 2, 4)
    lags = jnp.concatenate([lag_neg, lag_pos], axis=2)
    lags = lags.reshape(depth, NQ, GL, 2 * T - 1, H, H).transpose(0, 1, 3, 2, 4, 5)
    eye = jnp.eye(GL, dtype=F32)[None, None, None, :, None, :, None]
    dmat = (lags[..., None, :] * eye).reshape(depth, NQ, 2 * T - 1, LANES, LANES)
    emat = bmat.astype(BF16).reshape(depth, NQ, GL, T, H, 2, 2, P).transpose(0, 1, 5, 6, 3, 2, 4, 7)
    emat = emat.reshape(depth, NQ, 2, 2, SSM_FLAT, 1, P)
    emat = jnp.broadcast_to(emat, (depth, NQ, 2, 2, SSM_FLAT, 2, P)).reshape(depth, NQ, 2, 2, SSM_FLAT, 2 * P)
    wmat = cmat.astype(BF16).reshape(depth, NQ, GL, 2, 2, P, T, H).transpose(0, 1, 3, 4, 5, 6, 2, 7)
    wmat = wmat.reshape(depth, NQ, 2, 2, 1, P, SSM_FLAT)
    wmat = jnp.broadcast_to(wmat, (depth, NQ, 2, 2, 2, P, SSM_FLAT)).reshape(depth, NQ, 2, 2, 2 * P, SSM_FLAT)
    lamq = lam_t.reshape(depth, 2, 2, NQ, GL * P).transpose(0, 3, 1, 2, 4)
    return dmat.astype(BF16), emat, wmat, lamq


def _gelu_tanh(x):
    return 0.5 * x * (1.0 + jnp.tanh(math.sqrt(2.0 / math.pi) * (x + 0.044715 * (x * x * x))))


def _ssm_kernel(u_ref, d_ref, e_ref, w_ref, lam_ref, h0_ref, y_ref, hfin_ref,
                xf_scr, hb_scr, st_scr, y_scr, tq_scr, bq_scr, cq_scr, *, n_chunks, batch, seq):
    d = pl.program_id(1)
    T, GL = SSM_CHUNK, SSM_GL

    @pl.when(d == 0)
    def _():
        for s in range(batch):
            for tau in range(T):
                xf_scr[tau, pl.ds(s, n_chunks, stride=batch), :] = u_ref[pl.ds(s * seq + tau, n_chunks, stride=T), :]
        for j in range(T):
            for i in range(T):
                tq_scr[j * LANES:(j + 1) * LANES, i * LANES:(i + 1) * LANES] = d_ref[0, 0, i - j + T - 1].astype(BF16)

    ch_bits = SSM_GROUP.bit_length() - 1
    row_group = (lax.broadcasted_iota(jnp.int32, (SSM_FLAT, LANES), 0) >> ch_bits) & (GL - 1)
    col_group = (lax.broadcasted_iota(jnp.int32, (LANES, SSM_FLAT), 1) >> ch_bits) & (GL - 1)
    st_bits = SSM_STATE.bit_length() - 1
    lane_half = lax.broadcasted_iota(jnp.int32, (SSM_FLAT, LANES), 1) >> st_bits
    row_half = lax.broadcasted_iota(jnp.int32, (LANES, SSM_FLAT), 0) >> st_bits
    tiles_per_part = GL * SSM_STATE // LANES
    for part in range(2):
        emat, wmat = e_ref[0, 0, 0, part], w_ref[0, 0, 0, part]
        for m in range(tiles_per_part):
            ts = slice((part * tiles_per_part + m) * LANES, (part * tiles_per_part + m + 1) * LANES)
            bq_scr[:, ts] = jnp.where(row_group == 2 * m + lane_half, emat, 0.0).astype(BF16)
            cq_scr[ts, :] = jnp.where(col_group == 2 * m + row_half, wmat, 0.0).astype(BF16)

    xf = jnp.concatenate([xf_scr[tau] for tau in range(T)], axis=1).astype(BF16)
    hb_scr[...] = jnp.dot(xf, bq_scr[...], preferred_element_type=F32)
    HW = SSM_SW // 2
    lam_re, lam_im = lam_ref[0, 0, 0, 0:1, :], lam_ref[0, 0, 0, 1:2, :]

    def body(t, s):
        s_re, s_im = s
        c = jnp.where(d == 0, t, n_chunks - 1 - t)
        rows = pl.ds(pl.multiple_of(c * batch, batch), batch)
        st_scr[rows, 0:HW] = s_re
        st_scr[rows, HW:SSM_SW] = s_im
        return (lam_re * s_re - lam_im * s_im + hb_scr[rows, 0:HW],
                lam_re * s_im + lam_im * s_re + hb_scr[rows, HW:SSM_SW])

    h0 = h0_ref[0, 0, 0]
    f_re, f_im = lax.fori_loop(0, n_chunks, body, (h0[:, 0:HW], h0[:, HW:SSM_SW]))
    hfin_ref[0, 0] = jnp.concatenate([f_re, f_im], axis=1)
    part = jnp.dot(st_scr[...].astype(BF16), cq_scr[...], preferred_element_type=F32)

    @pl.when(d == 0)
    def _():
        y_scr[...] = part + jnp.dot(xf, tq_scr[...], preferred_element_type=F32)

    @pl.when(d == 1)
    def _():
        y = _gelu_tanh(y_scr[...] + part)
        for tau in range(T):
            xf_scr[tau] = y[:, tau * LANES:(tau + 1) * LANES]
        for s in range(batch):
            for tau in range(T):
                y_ref[pl.ds(s * seq + tau, n_chunks, stride=T), :] = xf_scr[tau, pl.ds(s, n_chunks, stride=batch), :]


def _ssm_call(u, dmat, emat, wmat, lamq, h0, layer, state_layer, batch, seq):
    n = batch * seq
    n_chunks = seq // SSM_CHUNK
    m = batch * n_chunks
    n_lags = 2 * SSM_CHUNK - 1
    return pl.pallas_call(
        functools.partial(_ssm_kernel, n_chunks=n_chunks, batch=batch, seq=seq),
        grid=(SSM_NQ, 2),
        in_specs=[
            pl.BlockSpec((n, LANES), lambda q, d: (0, q), pipeline_mode=pl.Buffered(1)),
            pl.BlockSpec((1, 1, n_lags, LANES, LANES), lambda q, d: (layer, q, 0, 0, 0)),
            pl.BlockSpec((1, 1, 1, 2, SSM_FLAT, LANES), lambda q, d: (layer, q, d, 0, 0, 0)),
            pl.BlockSpec((1, 1, 1, 2, LANES, SSM_FLAT), lambda q, d: (layer, q, d, 0, 0, 0)),
            pl.BlockSpec((1, 1, 1, 2, SSM_SW // 2), lambda q, d: (layer, q, d, 0, 0)),
            pl.BlockSpec((1, 1, 1, batch, SSM_SW), lambda q, d: (state_layer, q, d, 0, 0)),
        ],
        out_specs=[
            pl.BlockSpec((n, LANES), lambda q, d: (0, q)),
            pl.BlockSpec((1, 1, batch, SSM_SW), lambda q, d: (q, d, 0, 0)),
        ],
        out_shape=[
            jax.ShapeDtypeStruct((n, SSM_WIDTH), F32),
            jax.ShapeDtypeStruct((SSM_NQ, 2, batch, SSM_SW), F32),
        ],
        scratch_shapes=[pltpu.VMEM((SSM_CHUNK, m, LANES), F32), pltpu.VMEM((m, SSM_SW), F32),
                        pltpu.VMEM((m, SSM_SW), F32), pltpu.VMEM((m, SSM_FLAT), F32),
                        pltpu.VMEM((SSM_FLAT, SSM_FLAT), BF16), pltpu.VMEM((SSM_FLAT, SSM_SW), BF16),
                        pltpu.VMEM((SSM_SW, SSM_FLAT), BF16)],
        compiler_params=_cparams(("arbitrary", "arbitrary")),
        name="ssm",
    )(u, dmat, emat, wmat, lamq, h0)


def _outproj_kernel(attn_ref, ret_ref, zs_ref, gw_ref, gb_ref, w_ref, x_ref, g1_ref, o_ref):
    zs = zs_ref[...]
    gl = jnp.dot(zs.astype(BF16), gw_ref[0].astype(BF16), preferred_element_type=F32) + gb_ref[0]
    mix = jnp.concatenate([attn_ref[...], ret_ref[...], (zs * jax.nn.sigmoid(gl)).astype(BF16)], axis=1)
    tn = PROJ_COLS
    for jt in range(o_ref.shape[1] // tn):
        cs = slice(jt * tn, (jt + 1) * tn)
        acc = jnp.dot(mix, w_ref[0, :, cs], preferred_element_type=F32)
        o_ref[:, cs] = x_ref[:, cs] + g1_ref[0, :, cs] * acc


def _outproj_call(attn, ret, zs, glu_w, glu_b, w_out_bf16, x2d, gate, layer):
    n, d = x2d.shape
    tm = PROJ_ROWS
    rows_per_gate = n // gate.shape[0]
    return pl.pallas_call(
        _outproj_kernel,
        grid=(n // tm,),
        in_specs=[
            pl.BlockSpec((tm, ATTN_W), lambda i: (i, 0)),
            pl.BlockSpec((tm, RET_W), lambda i: (i, 0)),
            pl.BlockSpec((tm, SSM_WIDTH), lambda i: (i, 0)),
            pl.BlockSpec((1, SSM_WIDTH, SSM_WIDTH), lambda i: (layer, 0, 0)),
            pl.BlockSpec((1, 1, SSM_WIDTH), lambda i: (layer, 0, 0)),
            pl.BlockSpec((1, d, d), lambda i: (layer, 0, 0), pipeline_mode=pl.Buffered(1)),
            pl.BlockSpec((tm, d), lambda i: (i, 0)),
            pl.BlockSpec((1, 1, d), lambda i: (i * tm // rows_per_gate, 0, 0)),
        ],
        out_specs=pl.BlockSpec((tm, d), lambda i: (i, 0)),
        out_shape=jax.ShapeDtypeStruct((n, d), F32),
        compiler_params=_cparams(("parallel",)),
        name="outproj",
    )(attn, ret, zs, glu_w, glu_b, w_out_bf16, x2d, gate)


GATHER_ROWS = 512
TOKEN_SPLIT_BITS = 5
TOKEN_SPLIT = 1 << TOKEN_SPLIT_BITS


def _route_kernel(x_ref, g_ref, sh_ref, sc_ref, rw_ref, xg_ref, gs_ref, tok_ref, h_scr, *, cap):
    t = x_ref.shape[1]
    E = N_EXPERTS
    rc = 256
    logits = []
    for c0 in range(0, t, rc):
        x = x_ref[0, c0:c0 + rc, :]
        ms = jnp.mean(x * x, axis=-1, keepdims=True)
        h = x * lax.rsqrt(ms + EPS) * g_ref[...]
        h = h * (1.0 + sc_ref[0]) + sh_ref[0]
        h_scr[c0:c0 + rc, :] = h.astype(BF16)
        logits.append(lax.dot_general(rw_ref[0], h, (((1,), (1,)), ((), ())),
                                      preferred_element_type=F32, precision=HIGHEST))
    lg = jnp.concatenate(logits, axis=1)
    ex = jnp.exp(lg - jnp.max(lg, axis=0, keepdims=True))
    aff = ex / jnp.sum(ex, axis=0, keepdims=True)

    def count_ge(v):
        return jnp.sum(jnp.where(aff >= v, 1.0, 0.0), axis=1, keepdims=True)

    def bisect_bits(_, lohi):
        lo, hi = lohi
        mid = lo + ((hi - lo + 1) >> 1)
        ok = count_ge(lax.bitcast_convert_type(mid, F32)) >= float(cap)
        return jnp.where(ok, mid, lo), jnp.where(ok, hi, mid - 1)

    lo_b, _ = lax.fori_loop(0, 31, bisect_bits,
                            (jnp.zeros((E, 1), jnp.int32), jnp.full((E, 1), 0x7F800000, jnp.int32)))

    def bisect_val(_, lohi):
        lo, hi = lohi
        mid = lo + (hi - lo) * 0.5
        ok = count_ge(mid) >= float(cap)
        return jnp.where(ok, mid, lo), jnp.where(ok, hi, mid)

    _, hi_v = lax.fori_loop(0, 8, bisect_val,
                            (lax.bitcast_convert_type(lo_b, F32), lax.bitcast_convert_type(lo_b + 1, F32)))
    thr = jnp.max(jnp.where(aff < hi_v, aff, 0.0), axis=1, keepdims=True)
    gt = aff > thr
    eq = aff == thr
    need = float(cap) - jnp.sum(jnp.where(gt, 1.0, 0.0), axis=1, keepdims=True)
    tri = (lax.broadcasted_iota(jnp.int32, (LANES, LANES), 0)
           < lax.broadcasted_iota(jnp.int32, (LANES, LANES), 1)).astype(BF16)

    def prefix_count(flags):
        out, before = [], jnp.zeros((E, 1), F32)
        for c0 in range(0, t, LANES):
            blk = flags[:, c0:c0 + LANES]
            out.append(jnp.dot(blk.astype(BF16), tri, preferred_element_type=F32) + before)
            before = before + jnp.sum(blk, axis=1, keepdims=True)
        return jnp.concatenate(out, axis=1)

    sel = gt | (eq & (prefix_count(jnp.where(eq, 1.0, 0.0)) < need))
    pos = prefix_count(jnp.where(sel, 1.0, 0.0))
    base = (lax.broadcasted_iota(jnp.int32, (E, t), 0) * cap).astype(F32)
    gpos = jnp.where(sel, pos + base, -1.0)
    a_hi = aff.astype(BF16)
    a_mid = (aff - a_hi.astype(F32)).astype(BF16)
    a_lo = (aff - a_hi.astype(F32) - a_mid.astype(F32)).astype(BF16)
    aff_parts = jnp.concatenate([a_hi, a_mid, a_lo, jnp.zeros((LANES - 3 * E, t), BF16)], axis=0)
    tok_i = lax.broadcasted_iota(jnp.int32, (8, t), 1)
    row_i = lax.broadcasted_iota(jnp.int32, (8, t), 0)
    tok_parts = jnp.where(row_i == 0, tok_i >> TOKEN_SPLIT_BITS,
                          jnp.where(row_i == 1, tok_i & (TOKEN_SPLIT - 1), 0))
    tok_parts = tok_parts.astype(F32).astype(BF16)
    n_grp = max(1, GATHER_ROWS // cap)
    for e0 in range(0, E, n_grp):
        onehots = []
        for e in range(e0, e0 + n_grp):
            slot = (e * cap + lax.broadcasted_iota(jnp.int32, (cap, t), 0)).astype(F32)
            onehots.append(jnp.where(gpos[e:e + 1, :] == slot, 1.0, 0.0).astype(BF16))
        onehot = jnp.concatenate(onehots, axis=0)
        xg = jnp.dot(onehot, h_scr[...], preferred_element_type=F32)
        gate = _nt_dot(onehot, aff_parts)
        for k in range(n_grp):
            e, rs = e0 + k, slice(k * cap, (k + 1) * cap)
            xg_ref[e, 0] = xg[rs].astype(BF16)
            gs_ref[e, 0] = gate[rs, e:e + 1] + gate[rs, E + e:E + e + 1] + gate[rs, 2 * E + e:2 * E + e + 1]
        tk = _nt_dot(tok_parts, onehot)
        tok_ref[0, :, e0 * cap:(e0 + n_grp) * cap] = TOKEN_SPLIT * tk[0:1, :] + tk[1:2, :]


def _route_call(x3d, g, shift, scale, router_wt, layer):
    batch, t, d = x3d.shape
    cap = EC_CAPACITY * t // N_EXPERTS
    nb = shift.shape[0]
    mod_idx = (lambda b: (b, 0, 0)) if nb > 1 else (lambda b: (0, 0, 0))
    return pl.pallas_call(
        functools.partial(_route_kernel, cap=cap),
        grid=(batch,),
        in_specs=[
            pl.BlockSpec((1, t, d), lambda b: (b, 0, 0)),
            pl.BlockSpec((1, d), lambda b: (0, 0)),
            pl.BlockSpec((1, 1, d), mod_idx),
            pl.BlockSpec((1, 1, d), mod_idx),
            pl.BlockSpec((1, N_EXPERTS, d), lambda b: (layer, 0, 0)),
        ],
        out_specs=[
            pl.BlockSpec((N_EXPERTS, 1, cap, d), lambda b: (0, b, 0, 0)),
            pl.BlockSpec((N_EXPERTS, 1, cap, 1), lambda b: (0, b, 0, 0)),
            pl.BlockSpec((1, 1, N_EXPERTS * cap), lambda b: (b, 0, 0)),
        ],
        out_shape=[
            jax.ShapeDtypeStruct((N_EXPERTS, batch, cap, d), BF16),
            jax.ShapeDtypeStruct((N_EXPERTS, batch, cap, 1), F32),
            jax.ShapeDtypeStruct((batch, 1, N_EXPERTS * cap), F32),
        ],
        scratch_shapes=[pltpu.VMEM((t, d), BF16)],
        compiler_params=_cparams(("parallel",)),
        name="route",
    )(x3d, g, shift, scale, router_wt)


EXPERT_UP_TILE = 512
EXPERT_DOWN_TILE = 512


def _expert_kernel(xa_ref, xb_ref, wg_ref, wu_ref, wd_ref, ga_ref, gb_ref, oa_ref, ob_ref, h_scr, *, n_up):
    s = pl.program_id(1)
    tf = EXPERT_UP_TILE
    for k in range(n_up):
        @pl.when(s == k)
        def _(k=k):
            wg, wu = wg_ref[0, 0].astype(BF16), wu_ref[0, 0].astype(BF16)
            for i, x_ref in enumerate((xa_ref, xb_ref)):
                x = x_ref[0]
                a = jnp.dot(x, wg, preferred_element_type=F32)
                b = jnp.dot(x, wu, preferred_element_type=F32)
                h_scr[i, :, k * tf:(k + 1) * tf] = (_silu(a) * b).astype(BF16)

    @pl.when(s >= n_up)
    def _():
        wd = wd_ref[0, 0].astype(BF16)
        for i, (g_ref, o_ref) in enumerate(((ga_ref, oa_ref), (gb_ref, ob_ref))):
            y = jnp.dot(h_scr[i], wd, preferred_element_type=F32)
            o_ref[0] = (y * g_ref[0]).astype(o_ref.dtype)


def _expert_call(xg_a, gate_a, xg_b, gate_b, w_gate, w_up, w_down, layer):
    E, m, d = xg_a.shape
    assert xg_b.shape == xg_a.shape
    ff = w_gate.shape[-1]
    tf, tn = EXPERT_UP_TILE, EXPERT_DOWN_TILE
    n_up, n_down = ff // tf, d // tn
    def up_expert(e, s):
        return jnp.where(s >= n_up, jnp.minimum(e + 1, E - 1), e)

    x_spec = pl.BlockSpec((1, m, d), lambda e, s: (up_expert(e, s), 0, 0))
    g_spec = pl.BlockSpec((1, m, 1), lambda e, s: (e, 0, 0))
    o_spec = pl.BlockSpec((1, m, tn), lambda e, s: (e, 0, jnp.maximum(s - n_up, 0)))
    up_spec = pl.BlockSpec((1, 1, d, tf), lambda e, s: (layer, up_expert(e, s), 0, jnp.where(s >= n_up, 0, s)))
    return pl.pallas_call(
        functools.partial(_expert_kernel, n_up=n_up),
        grid=(E, n_up + n_down),
        in_specs=[
            x_spec, x_spec, up_spec, up_spec,
            pl.BlockSpec((1, 1, ff, tn), lambda e, s: (layer, e, 0, jnp.maximum(s - n_up, 0))),
            g_spec, g_spec,
        ],
        out_specs=[o_spec, o_spec],
        out_shape=[jax.ShapeDtypeStruct((E, m, d), BF16), jax.ShapeDtypeStruct((E, m, d), BF16)],
        scratch_shapes=[pltpu.VMEM((2, m, ff), BF16)],
        compiler_params=_cparams(("parallel", "arbitrary")),
        name="expert",
    )(xg_a, xg_b, w_gate, w_up, w_down, gate_a, gate_b)


COMBINE_TILE_ELEMS = 1024 * 1024


def _combine_kernel(y_ref, tok_ref, x_ref, g2_ref, o_ref, oh_scr):
    j = pl.program_id(1)
    t = x_ref.shape[1]
    n_slots = tok_ref.shape[2]

    @pl.when(j == 0)
    def _():
        tok_i = lax.broadcasted_iota(jnp.int32, (t, n_slots), 0).astype(F32)
        oh_scr[...] = jnp.where(tok_i == tok_ref[0], 1.0, 0.0).astype(BF16)

    y = jnp.concatenate([y_ref[e, 0] for e in range(N_EXPERTS)], axis=0)
    moe = jnp.dot(oh_scr[...], y, preferred_element_type=F32)
    o_ref[0] = x_ref[0] + g2_ref[0] * moe


def _combine_call(y4d, tok, x3d, gate):
    batch, t, d = x3d.shape
    cap = y4d.shape[2]
    tn = min(d, COMBINE_TILE_ELEMS // t)
    nb = gate.shape[0]
    gate_idx = (lambda b, j: (b, 0, j)) if nb > 1 else (lambda b, j: (0, 0, j))
    return pl.pallas_call(
        _combine_kernel,
        grid=(batch, d // tn),
        in_specs=[
            pl.BlockSpec((N_EXPERTS, 1, cap, tn), lambda b, j: (0, b, 0, j)),
            pl.BlockSpec((1, 1, N_EXPERTS * cap), lambda b, j: (b, 0, 0)),
            pl.BlockSpec((1, t, tn), lambda b, j: (b, 0, j)),
            pl.BlockSpec((1, 1, tn), gate_idx),
        ],
        out_specs=pl.BlockSpec((1, t, tn), lambda b, j: (b, 0, j)),
        out_shape=jax.ShapeDtypeStruct((batch, t, d), F32),
        scratch_shapes=[pltpu.VMEM((t, N_EXPERTS * cap), BF16)],
        compiler_params=_cparams(("parallel", "arbitrary")),
        name="combine",
    )(y4d, tok, x3d, gate)


def _norm_kernel(x_ref, g_ref, o_ref):
    x = x_ref[...]
    o_ref[...] = x * lax.rsqrt(jnp.mean(x * x, axis=-1, keepdims=True) + EPS) * g_ref[...]


def _norm_call(x2d, g):
    n, d = x2d.shape
    tm = 512
    return pl.pallas_call(
        _norm_kernel,
        grid=(n // tm,),
        in_specs=[pl.BlockSpec((tm, d), lambda i: (i, 0)), pl.BlockSpec((1, d), lambda i: (0, 0))],
        out_specs=pl.BlockSpec((tm, d), lambda i: (i, 0)),
        out_shape=jax.ShapeDtypeStruct((n, d), F32),
        compiler_params=_cparams(("parallel",)),
        name="final_norm",
    )(x2d, g)


def _rope_tables(n_tokens):
    n_rows = n_tokens // GRID_W
    row = jnp.repeat(jnp.arange(n_rows), GRID_W).astype(F32)
    col = jnp.tile(jnp.arange(GRID_W), n_rows).astype(F32)
    axis_dim = HEAD_DIM // 2
    inv_freq = ROPE_BASE ** (-jnp.arange(0, axis_dim, 2, dtype=F32) / axis_dim)
    ang_r, ang_c = row[:, None] * inv_freq[None, :], col[:, None] * inv_freq[None, :]
    cos = jnp.concatenate([jnp.cos(ang_r), jnp.cos(ang_r), jnp.cos(ang_c), jnp.cos(ang_c)], axis=1)
    sin = jnp.concatenate([-jnp.sin(ang_r), jnp.sin(ang_r), -jnp.sin(ang_c), jnp.sin(ang_c)], axis=1)
    return cos, sin


def _mix_and_route(x3d, layer, mods, w, ssm_mats, rope, ctx):
    batch, seq, d = x3d.shape
    n = batch * seq
    sh1, sc1, g1, sh2, sc2, _ = mods
    x2d = x3d.reshape(n, d)
    latent = ctx is not None
    z, u = _inproj_call(x2d, w['norm1_g'][layer][None], sh1, sc1, w['w_in'], layer, rope[0], rope[1], latent, seq)
    if latent:
        attn = _attn_lat_call(z, w['attn_sink'][layer], ctx['k'], ctx['v'], layer, batch, seq)
        k_new = v_new = None
        r0, h0, state_layer = ctx['ret'], ctx['ssm'], layer
    else:
        attn, k_new, v_new = _attn_ctx_call(z, w['attn_sink'][layer], batch, seq)
        r0, h0, state_layer = None, jnp.zeros((1, SSM_NQ, 2, batch, SSM_SW), F32), 0
    ret, r_fin = _ret_call(z, w['ret_log_gamma'][layer], w['ret_gn_g'][layer][None], r0, state_layer, batch, seq)
    zs, h_fin = _ssm_call(u, *ssm_mats, h0, layer, state_layer, batch, seq)
    x1 = _outproj_call(attn, ret, zs, w['ssm_glu_w'], w['ssm_glu_b'], w['w_out'], x2d, g1, layer)
    x1 = x1.reshape(batch, seq, d)
    xg, gate_slot, tok = _route_call(x1, w['norm2_g'][layer][None], sh2, sc2, w['router_wt'], layer)
    return x1, (xg, gate_slot, tok), (k_new, v_new, r_fin, h_fin)


def kernel(x_prompt, x_sample, cache_attn_k, cache_attn_v, state_ret, state_ssm_re, state_ssm_im, c, c_ctx, mod_w, mod_b, norm1_g, norm2_g, w_in, w_out, attn_sink, ret_log_gamma, ret_gn_g, ssm_lam_re, ssm_lam_im, ssm_log_step, ssm_b_re, ssm_b_im, ssm_c_re, ssm_c_im, ssm_d, ssm_glu_w, ssm_glu_b, router_w, moe_w_gate, moe_w_up, moe_w_down, final_norm_g):
    depth = w_in.shape[0]
    batch, seq, d = x_prompt.shape
    dec_batch, dec_seq, _ = x_sample.shape
    past = cache_attn_k.shape[2]
    G, P = N_SSM_GROUPS, SSM_STATE
    assert dec_seq % PROJ_ROWS == 0 and (batch * seq) % PROJ_ROWS == 0

    cv = jnp.concatenate([c_ctx[None, :], c, jnp.zeros((MOD_ROWS - 1 - dec_batch, d), F32)], axis=0)
    mod = _mod_call(cv, mod_w, mod_b)
    ssm_mats = _ssm_tile_matrices(*_ssm_group_matrices(
        ssm_lam_re, ssm_lam_im, ssm_log_step, ssm_b_re, ssm_b_im, ssm_c_re, ssm_c_im, ssm_d))
    rope = _rope_tables(dec_seq)
    no_rope = (jnp.ones((PROJ_ROWS, LANES), F32), jnp.zeros((PROJ_ROWS, LANES), F32))
    w = {
        'norm1_g': norm1_g, 'norm2_g': norm2_g, 'w_in': w_in.astype(BF16), 'w_out': w_out.astype(BF16),
        'attn_sink': attn_sink,
        'ret_log_gamma': ret_log_gamma, 'ret_gn_g': ret_gn_g, 'ssm_glu_w': ssm_glu_w,
        'ssm_glu_b': ssm_glu_b.reshape(depth, 1, SSM_WIDTH), 'router_wt': router_w.transpose(0, 2, 1),
    }
    h0_lat = jnp.concatenate([state_ssm_re.reshape(dec_batch, depth, 2, SSM_NQ, SSM_SW // 2),
                              state_ssm_im.reshape(dec_batch, depth, 2, SSM_NQ, SSM_SW // 2)], axis=-1)
    h0_lat = h0_lat.transpose(1, 3, 2, 0, 4)
    ctx = {
        'k': cache_attn_k.reshape(dec_batch, depth, past, KV_W),
        'v': cache_attn_v.reshape(dec_batch, depth, past, KV_W),
        'ret': state_ret, 'ssm': h0_lat,
    }
    xp, xs = x_prompt, x_sample
    ks, vs, rets, ssms = [], [], [], []
    for layer in range(depth):
        m = mod[layer]
        mods_p = [m[0:1, i * d:(i + 1) * d][:, None, :] for i in range(6)]
        mods_s = [m[1:1 + dec_batch, i * d:(i + 1) * d][:, None, :] for i in range(6)]
        xp1, (xg_p, gs_p, tok_p), (k_l, v_l, r_l, h_l) = _mix_and_route(xp, layer, mods_p, w, ssm_mats, no_rope, None)
        xs1, (xg_s, gs_s, tok_s), _ = _mix_and_route(xs, layer, mods_s, w, ssm_mats, rope, ctx)
        cap_p, cap_s = xg_p.shape[2], xg_s.shape[2]
        y_p, y_s = _expert_call(
            xg_p.reshape(N_EXPERTS, batch * cap_p, d), gs_p.reshape(N_EXPERTS, batch * cap_p, 1),
            xg_s.reshape(N_EXPERTS, dec_batch * cap_s, d), gs_s.reshape(N_EXPERTS, dec_batch * cap_s, 1),
            moe_w_gate, moe_w_up, moe_w_down, layer)
        xp = _combine_call(y_p.reshape(N_EXPERTS, batch, cap_p, d), tok_p, xp1, mods_p[5])
        xs = _combine_call(y_s.reshape(N_EXPERTS, dec_batch, cap_s, d), tok_s, xs1, mods_s[5])
        ks.append(k_l.reshape(batch, seq, N_KV_HEADS, HEAD_DIM))
        vs.append(v_l.reshape(batch, seq, N_KV_HEADS, HEAD_DIM))
        rets.append(r_l)
        ssms.append(h_l)
    y_prompt = _norm_call(xp.reshape(batch * seq, d), final_norm_g[None]).reshape(batch, seq, d)
    y_sample = _norm_call(xs.reshape(dec_batch * dec_seq, d), final_norm_g[None]).reshape(dec_batch, dec_seq, d)
    h_all = jnp.stack(ssms, axis=0).reshape(depth, SSM_NQ, 2, batch, 2, SSM_GL, P)
    h_all = h_all.transpose(4, 3, 0, 2, 1, 5, 6).reshape(2, batch, depth, 2, G, P)
    return (y_prompt, y_sample, jnp.stack(ks, axis=1), jnp.stack(vs, axis=1), jnp.stack(rets, axis=1),
            h_all[0], h_all[1])
```

```python
import functools
import math

import jax
import jax.numpy as jnp
from jax import lax
from jax.experimental import pallas as pl
from jax.experimental.pallas import tpu as pltpu

F32 = jnp.float32
BF16 = jnp.bfloat16
HIGHEST = lax.Precision.HIGHEST

D_MODEL = 2048
GRID_W = 64
EPS = 1e-6
HEAD_DIM = 128
ATTN_W = D_MODEL // 2
N_HEADS = ATTN_W // HEAD_DIM
N_KV_HEADS = N_HEADS // 4
Q_PER_KV = N_HEADS // N_KV_HEADS
KV_W = N_KV_HEADS * HEAD_DIM
ATTN_BLOCK = 128
RET_W = D_MODEL // 4
RET_DK = 128
N_RET_HEADS = RET_W // RET_DK
RET_CHUNK = 128
SSM_WIDTH = D_MODEL // 4
SSM_GROUP = 16
N_SSM_GROUPS = SSM_WIDTH // SSM_GROUP
SSM_STATE = 64
N_EXPERTS = 16
EXPERT_FF = D_MODEL // 2
EC_CAPACITY = 2
ROPE_BASE = 10000.0
IN_W = ATTN_W + 2 * KV_W + 4 * RET_W + SSM_WIDTH

LANES = 128
HEAD_Q0 = 0
HEAD_K0 = ATTN_W // LANES
HEAD_V0 = HEAD_K0 + KV_W // LANES
HEAD_RQ0 = HEAD_V0 + KV_W // LANES
HEAD_RK0 = HEAD_RQ0 + RET_W // LANES
HEAD_RV0 = HEAD_RK0 + RET_W // LANES
HEAD_RG0 = HEAD_RV0 + RET_W // LANES
HEAD_U0 = HEAD_RG0 + RET_W // LANES

SSM_CHUNK = 8
SSM_GL = LANES // SSM_GROUP
SSM_NQ = SSM_WIDTH // LANES
SSM_FLAT = SSM_CHUNK * LANES
SSM_SW = SSM_GL * 2 * SSM_STATE

MOD_ROWS = 16
PROJ_ROWS = 512
PROJ_COLS = 512
VMEM_LIMIT = 56 * 1024 * 1024


def _cparams(sem):
    return pltpu.CompilerParams(dimension_semantics=sem, vmem_limit_bytes=VMEM_LIMIT)


def _silu(x):
    return x * jax.nn.sigmoid(x)


def _nt_dot(a, b):
    return lax.dot_general(a, b, (((1,), (1,)), ((), ())), preferred_element_type=F32)


def _tn_dot(a, b):
    return lax.dot_general(a, b, (((0,), (0,)), ((), ())), preferred_element_type=F32)


def _mod_kernel(cv_ref, w_ref, b_ref, o_ref):
    a = _silu(cv_ref[...])
    o_ref[0] = jnp.dot(a, w_ref[0], preferred_element_type=F32, precision=HIGHEST) + b_ref[0]


def _mod_call(cv, mod_w, mod_b):
    depth, d, e = mod_w.shape
    tn = 1024
    return pl.pallas_call(
        _mod_kernel,
        grid=(depth, e // tn),
        in_specs=[
            pl.BlockSpec((MOD_ROWS, d), lambda l, j: (0, 0)),
            pl.BlockSpec((1, d, tn), lambda l, j: (l, 0, j)),
            pl.BlockSpec((1, 1, tn), lambda l, j: (l, 0, j)),
        ],
        out_specs=pl.BlockSpec((1, MOD_ROWS, tn), lambda l, j: (l, 0, j)),
        out_shape=jax.ShapeDtypeStruct((depth, MOD_ROWS, e), F32),
        compiler_params=_cparams(("parallel", "parallel")),
        name="mod",
    )(cv, mod_w, mod_b.reshape(depth, 1, e))


def _rope_head(z, cos, sin_signed):
    lane = lax.broadcasted_iota(jnp.int32, z.shape, 1)
    partner = jnp.where((lane % 64) < 32, pltpu.roll(z, LANES - 32, 1), pltpu.roll(z, 32, 1))
    return z * cos + partner * sin_signed


def _inproj_kernel(x_ref, g_ref, sh_ref, sc_ref, w_ref, cos_ref, sin_ref, z_ref, u_ref, *, use_rope):
    x = x_ref[...]
    ms = jnp.mean(x * x, axis=-1, keepdims=True)
    hn = x * lax.rsqrt(ms + EPS) * g_ref[...]
    h = (hn * (1.0 + sc_ref[0]) + sh_ref[0]).astype(BF16)
    tn = PROJ_COLS
    heads_per_tile = tn // LANES
    for jt in range(IN_W // tn):
        acc = jnp.dot(h, w_ref[0, :, jt * tn:(jt + 1) * tn], preferred_element_type=F32)
        head0 = jt * heads_per_tile
        if head0 >= HEAD_U0:
            u_ref[:, (head0 - HEAD_U0) * LANES:(head0 - HEAD_U0) * LANES + tn] = acc
            continue
        for k in range(heads_per_tile):
            head = head0 + k
            zk = acc[:, k * LANES:(k + 1) * LANES]
            if HEAD_RK0 <= head < HEAD_RV0:
                zk = zk * RET_DK ** -0.5
            if use_rope and (head < HEAD_V0 or HEAD_RQ0 <= head < HEAD_RV0):
                zk = _rope_head(zk, cos_ref[...], sin_ref[...])
            z_ref[:, head * LANES:(head + 1) * LANES] = zk.astype(z_ref.dtype)


def _inproj_call(x2d, g, shift, scale, w_in_bf16, layer, cos, sin_signed, use_rope, seq):
    n, d = x2d.shape
    tm = PROJ_ROWS
    z_w = IN_W - SSM_WIDTH
    nb = shift.shape[0]
    rows_per_mod = n // nb
    tiles_per_seq = seq // tm
    mod_idx = lambda i: (i * tm // rows_per_mod, 0, 0)
    pos_idx = (lambda i: (i % tiles_per_seq, 0)) if use_rope else (lambda i: (0, 0))
    return pl.pallas_call(
        functools.partial(_inproj_kernel, use_rope=use_rope),
        grid=(n // tm,),
        in_specs=[
            pl.BlockSpec((tm, d), lambda i: (i, 0)),
            pl.BlockSpec((1, d), lambda i: (0, 0)),
            pl.BlockSpec((1, 1, d), mod_idx),
            pl.BlockSpec((1, 1, d), mod_idx),
            pl.BlockSpec((1, d, IN_W), lambda i: (layer, 0, 0), pipeline_mode=pl.Buffered(1)),
            pl.BlockSpec((tm, LANES), pos_idx),
            pl.BlockSpec((tm, LANES), pos_idx),
        ],
        out_specs=[
            pl.BlockSpec((tm, z_w), lambda i: (i, 0)),
            pl.BlockSpec((tm, SSM_WIDTH), lambda i: (i, 0)),
        ],
        out_shape=[jax.ShapeDtypeStruct((n, z_w), BF16), jax.ShapeDtypeStruct((n, SSM_WIDTH), F32)],
        compiler_params=_cparams(("parallel",)),
        name="inproj",
    )(x2d, g, shift, scale, w_in_bf16, cos, sin_signed)


def _sink_softmax_parts(s, sink_col):
    m = jnp.maximum(jnp.max(s, axis=-1, keepdims=True), sink_col)
    e = jnp.exp(s - m)
    return e, 1.0 / (jnp.sum(e, axis=-1, keepdims=True) + jnp.exp(sink_col - m))


def _stack_q(q_ref, kh):
    return jnp.concatenate(
        [q_ref[:, (kh * Q_PER_KV + g) * HEAD_DIM:(kh * Q_PER_KV + g + 1) * HEAD_DIM] for g in range(Q_PER_KV)], axis=0)


def _sink_col(sink_ref, kh, rows):
    return jnp.concatenate(
        [jnp.full((rows, 1), sink_ref[kh * Q_PER_KV + g], F32) for g in range(Q_PER_KV)], axis=0)


def _attn_ctx_kernel(sink_ref, q_ref, k_ref, v_ref, o_ref, kn_ref, vn_ref):
    rows = q_ref.shape[0]
    scale = HEAD_DIM ** -0.5
    for kh in range(N_KV_HEADS):
        hs = slice(kh * HEAD_DIM, (kh + 1) * HEAD_DIM)
        s = _nt_dot(_stack_q(q_ref, kh), k_ref[:, hs]) * scale
        e, inv = _sink_softmax_parts(s, _sink_col(sink_ref, kh, rows))
        o = jnp.dot(e.astype(BF16), v_ref[:, hs], preferred_element_type=F32) * inv
        for g in range(Q_PER_KV):
            c0 = (kh * Q_PER_KV + g) * HEAD_DIM
            o_ref[:, c0:c0 + HEAD_DIM] = o[g * rows:(g + 1) * rows].astype(o_ref.dtype)
    kn_ref[0] = k_ref[...].astype(F32)
    vn_ref[0] = v_ref[...].astype(F32)


def _attn_ctx_call(z, sink, batch, seq):
    n = z.shape[0]
    kcol, vcol = HEAD_K0 * LANES // KV_W, HEAD_V0 * LANES // KV_W
    return pl.pallas_call(
        _attn_ctx_kernel,
        grid=(batch,),
        in_specs=[
            pl.BlockSpec(memory_space=pltpu.SMEM),
            pl.BlockSpec((seq, ATTN_W), lambda b: (b, 0)),
            pl.BlockSpec((seq, KV_W), lambda b: (b, kcol)),
            pl.BlockSpec((seq, KV_W), lambda b: (b, vcol)),
        ],
        out_specs=[
            pl.BlockSpec((seq, ATTN_W), lambda b: (b, 0)),
            pl.BlockSpec((1, seq, KV_W), lambda b: (b, 0, 0)),
            pl.BlockSpec((1, seq, KV_W), lambda b: (b, 0, 0)),
        ],
        out_shape=[
            jax.ShapeDtypeStruct((n, ATTN_W), BF16),
            jax.ShapeDtypeStruct((batch, seq, KV_W), F32),
            jax.ShapeDtypeStruct((batch, seq, KV_W), F32),
        ],
        compiler_params=_cparams(("parallel",)),
        name="attn_ctx",
    )(sink, z, z, z)


def _attn_lat_kernel(sink_ref, q_ref, kp_ref, kc_ref, kn_ref, vp_ref, vc_ref, vn_ref, kx_ref, vx_ref, o_ref, *, nblk):
    i = pl.program_id(1)
    rows = q_ref.shape[0]
    scale = HEAD_DIM ** -0.5
    r = lax.broadcasted_iota(jnp.int32, (rows, ATTN_BLOCK), 0)
    c = lax.broadcasted_iota(jnp.int32, (rows, ATTN_BLOCK), 1)
    neg = jnp.float32(-jnp.inf)
    bias_prev = jnp.where(c >= r, jnp.where(i > 0, 0.0, neg), neg)
    bias_next = jnp.where(c <= r, jnp.where(i < nblk - 1, 0.0, neg), neg)
    n_ctx = kx_ref.shape[2]
    bias = jnp.concatenate([bias_prev, jnp.zeros((rows, ATTN_BLOCK), F32), bias_next,
                            jnp.zeros((rows, n_ctx), F32)], axis=1)
    bias = jnp.concatenate([bias] * Q_PER_KV, axis=0)
    for kh in range(N_KV_HEADS):
        hs = slice(kh * HEAD_DIM, (kh + 1) * HEAD_DIM)
        kcat = jnp.concatenate([kp_ref[:, hs], kc_ref[:, hs], kn_ref[:, hs], kx_ref[0, 0, :, hs].astype(BF16)], axis=0)
        vcat = jnp.concatenate([vp_ref[:, hs], vc_ref[:, hs], vn_ref[:, hs], vx_ref[0, 0, :, hs].astype(BF16)], axis=0)
        s = _nt_dot(_stack_q(q_ref, kh), kcat) * scale + bias
        e, inv = _sink_softmax_parts(s, _sink_col(sink_ref, kh, rows))
        o = jnp.dot(e.astype(BF16), vcat, preferred_element_type=F32) * inv
        for g in range(Q_PER_KV):
            c0 = (kh * Q_PER_KV + g) * HEAD_DIM
            o_ref[:, c0:c0 + HEAD_DIM] = o[g * rows:(g + 1) * rows].astype(o_ref.dtype)


def _attn_lat_call(z, sink, cache_k, cache_v, layer, batch, seq):
    n = z.shape[0]
    nblk = seq // ATTN_BLOCK
    kcol, vcol = HEAD_K0 * LANES // KV_W, HEAD_V0 * LANES // KV_W
    past = cache_k.shape[2]

    def kv_spec(col, off):
        return pl.BlockSpec((ATTN_BLOCK, KV_W),
                            lambda b, i: (b * nblk + jnp.clip(i + off, 0, nblk - 1), col))

    cache_spec = pl.BlockSpec((1, 1, past, KV_W), lambda b, i: (b, layer, 0, 0))
    return pl.pallas_call(
        functools.partial(_attn_lat_kernel, nblk=nblk),
        grid=(batch, nblk),
        in_specs=[
            pl.BlockSpec(memory_space=pltpu.SMEM),
            pl.BlockSpec((ATTN_BLOCK, ATTN_W), lambda b, i: (b * nblk + i, 0)),
            kv_spec(kcol, -1), kv_spec(kcol, 0), kv_spec(kcol, 1),
            kv_spec(vcol, -1), kv_spec(vcol, 0), kv_spec(vcol, 1),
            cache_spec, cache_spec,
        ],
        out_specs=pl.BlockSpec((ATTN_BLOCK, ATTN_W), lambda b, i: (b * nblk + i, 0)),
        out_shape=jax.ShapeDtypeStruct((n, ATTN_W), BF16),
        compiler_params=_cparams(("parallel", "parallel")),
        name="attn_lat",
    )(sink, z, z, z, z, z, z, z, cache_k, cache_v)


def _ret_kernel(lg_ref, q_ref, k_ref, v_ref, rg_ref, gn_ref, *rest, n_chunks, has_r0):
    r0_ref = rest[0] if has_r0 else None
    o_ref, rfin_ref, of_scr, ob_scr, r_scr, dec_scr, qd_scr, kd_scr, cd_scr = rest[1:] if has_r0 else rest
    C, dk = RET_CHUNK, RET_DK
    ii = lax.broadcasted_iota(jnp.int32, (C, C), 0).astype(F32)
    jj = lax.broadcasted_iota(jnp.int32, (C, C), 1).astype(F32)
    pos = lax.broadcasted_iota(jnp.int32, (C, dk), 0).astype(F32)
    for h in range(N_RET_HEADS):
        for d in range(2):
            lg = lg_ref[d, h]
            diff = (ii - jj) if d == 0 else (jj - ii)
            dec_scr[d, h] = jnp.where(diff >= 0, jnp.exp(jnp.maximum(diff, 0.0) * lg), 0.0)
            if d == 0:
                qd_scr[d, h] = jnp.exp((pos + 1.0) * lg)
                kd_scr[d, h] = jnp.exp((C - 1.0 - pos) * lg)
            else:
                qd_scr[d, h] = jnp.exp((C - pos) * lg)
                kd_scr[d, h] = jnp.exp(pos * lg)
            cd_scr[d, h] = jnp.exp(jnp.full((8, dk), float(C), F32) * lg)
            r_scr[d, h] = r0_ref[0, 0, d, h] if has_r0 else jnp.zeros((dk, dk), F32)

    def body(t, carry):
        for h in range(N_RET_HEADS):
            hs = slice(h * dk, (h + 1) * dk)
            for d in range(2):
                n = t if d == 0 else n_chunks - 1 - t
                rows = pl.ds(pl.multiple_of(n * C, C), C)
                q, k, v = q_ref[rows, hs], k_ref[rows, hs], v_ref[rows, hs]
                r = r_scr[d, h]
                s = _nt_dot(q, k) * dec_scr[d, h]
                intra = jnp.dot(s.astype(BF16), v, preferred_element_type=F32)
                cross = jnp.dot((q.astype(F32) * qd_scr[d, h]).astype(BF16), r.astype(BF16),
                                preferred_element_type=F32)
                kv = _tn_dot((k.astype(F32) * kd_scr[d, h]).astype(BF16), v)
                (of_scr if d == 0 else ob_scr)[rows, hs] = intra + cross
                r_scr[d, h] = cd_scr[d, h, 0:1, :] * r + kv
        return carry

    lax.fori_loop(0, n_chunks, body, 0)
    rfin_ref[0] = r_scr[...]
    for h in range(N_RET_HEADS):
        hs = slice(h * dk, (h + 1) * dk)
        o = of_scr[:, hs] + ob_scr[:, hs]
        mu = jnp.mean(o, axis=-1, keepdims=True)
        var = jnp.mean(jnp.square(o - mu), axis=-1, keepdims=True)
        on = (o - mu) * lax.rsqrt(var + EPS) * gn_ref[:, hs]
        o_ref[:, hs] = (_silu(rg_ref[:, hs].astype(F32)) * on).astype(o_ref.dtype)


def _ret_call(z, log_gamma, gn_g, r0, state_layer, batch, seq):
    n = z.shape[0]
    dk, H = RET_DK, N_RET_HEADS

    def col(head0):
        return pl.BlockSpec((seq, RET_W), lambda b: (b, head0 * LANES // RET_W))

    has_r0 = r0 is not None
    state_specs = [pl.BlockSpec((1, 1, 2, H, dk, dk), lambda b: (b, state_layer, 0, 0, 0, 0))] if has_r0 else []
    state_args = (r0,) if has_r0 else ()
    return pl.pallas_call(
        functools.partial(_ret_kernel, n_chunks=seq // RET_CHUNK, has_r0=has_r0),
        grid=(batch,),
        in_specs=[
            pl.BlockSpec(memory_space=pltpu.SMEM),
            col(HEAD_RQ0), col(HEAD_RK0), col(HEAD_RV0), col(HEAD_RG0),
            pl.BlockSpec((1, RET_W), lambda b: (0, 0)),
        ] + state_specs,
        out_specs=[
            pl.BlockSpec((seq, RET_W), lambda b: (b, 0)),
            pl.BlockSpec((1, 2, H, dk, dk), lambda b: (b, 0, 0, 0, 0)),
        ],
        out_shape=[
            jax.ShapeDtypeStruct((n, RET_W), BF16),
            jax.ShapeDtypeStruct((batch, 2, H, dk, dk), F32),
        ],
        scratch_shapes=[
            pltpu.VMEM((seq, RET_W), F32), pltpu.VMEM((seq, RET_W), F32),
            pltpu.VMEM((2, H, dk, dk), F32), pltpu.VMEM((2, H, RET_CHUNK, RET_CHUNK), F32),
            pltpu.VMEM((2, H, RET_CHUNK, dk), F32), pltpu.VMEM((2, H, RET_CHUNK, dk), F32),
            pltpu.VMEM((2, H, 8, dk), F32),
        ],
        compiler_params=_cparams(("parallel",)),
        name="ret",
    )(log_gamma, z, z, z, z, gn_g, *state_args)


def _ssm_group_matrices(lam_re, lam_im, log_step, b_re, b_im, c_re, c_im, d_skip):
    depth, _, G, P = lam_re.shape
    H, T = SSM_GROUP, SSM_CHUNK
    dt = jnp.exp(log_step)[..., None]
    ar, ai = lam_re * dt, lam_im * dt
    mag = jnp.exp(ar)
    lbr, lbi = mag * jnp.cos(ai), mag * jnp.sin(ai)
    den = lam_re * lam_re + lam_im * lam_im
    fr = ((lbr - 1.0) * lam_re + lbi * lam_im) / den
    fi = (lbi * lam_re - (lbr - 1.0) * lam_im) / den
    bbr = fr[..., None] * b_re - fi[..., None] * b_im
    bbi = fr[..., None] * b_im + fi[..., None] * b_re
    m = jnp.arange(T + 1, dtype=F32)[:, None]
    pmag = jnp.exp(ar[..., None, :] * m)
    pwr, pwi = pmag * jnp.cos(ai[..., None, :] * m), pmag * jnp.sin(ai[..., None, :] * m)
    xr = pwr[..., None] * bbr[..., None, :, :] - pwi[..., None] * bbi[..., None, :, :]
    xi = pwr[..., None] * bbi[..., None, :, :] + pwi[..., None] * bbr[..., None, :, :]
    kern = (jnp.einsum('ldghp,ldgmpi->ldgmhi', c_re, xr[..., :T, :, :], precision=HIGHEST)
            - jnp.einsum('ldghp,ldgmpi->ldgmhi', c_im, xi[..., :T, :, :], precision=HIGHEST))
    tok = jnp.arange(T)
    lag = tok[None, :] - tok[:, None]
    kf = kern[:, 0][:, :, jnp.clip(lag, 0, T - 1)] * (lag >= 0)[None, None, :, :, None, None].astype(F32)
    kb = kern[:, 1][:, :, jnp.clip(-lag, 0, T - 1)] * (lag <= 0)[None, None, :, :, None, None].astype(F32)
    tm = (kf + kb).transpose(0, 1, 2, 5, 3, 4)
    skip = (jnp.eye(T, dtype=F32)[None, None, :, None, :, None]
            * jnp.eye(H, dtype=F32)[None, None, None, :, None, :]
            * d_skip.reshape(depth, G, 1, H, 1, 1))
    tmat = (tm + skip).reshape(depth, G, T * H, T * H)
    rev = T - 1 - tok
    ef_r, ef_i = xr[:, 0][:, :, rev], xi[:, 0][:, :, rev]
    eb_r, eb_i = xr[:, 1][:, :, tok], xi[:, 1][:, :, tok]
    bmat = jnp.concatenate([e.transpose(0, 1, 2, 4, 3) for e in (ef_r, ef_i, eb_r, eb_i)], axis=-1)
    bmat = bmat.reshape(depth, G, T * H, 4 * P)

    def entry(d, powers):
        pr = pwr[:, d][:, :, powers][:, :, :, None, :]
        pi = pwi[:, d][:, :, powers][:, :, :, None, :]
        cr, ci = c_re[:, d][:, :, None], c_im[:, d][:, :, None]
        wr, wi = cr * pr - ci * pi, cr * pi + ci * pr
        return wr.transpose(0, 1, 4, 2, 3), -wi.transpose(0, 1, 4, 2, 3)

    cf_r, cf_i = entry(0, tok + 1)
    cb_r, cb_i = entry(1, T - tok)
    cmat = jnp.concatenate([cf_r, cf_i, cb_r, cb_i], axis=2).reshape(depth, G, 4 * P, T * H)
    lam_t = jnp.stack([pwr[..., T, :], pwi[..., T, :]], axis=2)
    return tmat, bmat, cmat, lam_t


def _ssm_tile_matrices(tmat, bmat, cmat, lam_t):
    depth = tmat.shape[0]
    T, H, GL, NQ, P = SSM_CHUNK, SSM_GROUP, SSM_GL, SSM_NQ, SSM_STATE
    tm = tmat.reshape(depth, N_SSM_GROUPS, T, H, T, H)
    lag_neg = tm[:, :, 1:, :, 0, :][:, :, ::-1]
    lag_pos = tm[:, :, 0].transpose(0, 1, 3, 2, 4)
    lags = jnp.concatenate([lag_neg, lag_pos], axis=2)
    lags = lags.reshape(depth, NQ, GL, 2 * T - 1, H, H).transpose(0, 1, 3, 2, 4, 5)
    eye = jnp.eye(GL, dtype=F32)[None, None, None, :, None, :, None]
    dmat = (lags[..., None, :] * eye).reshape(depth, NQ, 2 * T - 1, LANES, LANES)
    emat = bmat.astype(BF16).reshape(depth, NQ, GL, T, H, 2, 2, P).transpose(0, 1, 5, 6, 3, 2, 4, 7)
    emat = emat.reshape(depth, NQ, 2, 2, SSM_FLAT, 1, P)
    emat = jnp.broadcast_to(emat, (depth, NQ, 2, 2, SSM_FLAT, 2, P)).reshape(depth, NQ, 2, 2, SSM_FLAT, 2 * P)
    wmat = cmat.astype(BF16).reshape(depth, NQ, GL, 2, 2, P, T, H).transpose(0, 1, 3, 4, 5, 6, 2, 7)
    wmat = wmat.reshape(depth, NQ, 2, 2, 1, P, SSM_FLAT)
    wmat = jnp.broadcast_to(wmat, (depth, NQ, 2, 2, 2, P, SSM_FLAT)).reshape(depth, NQ, 2, 2, 2 * P, SSM_FLAT)
    lamq = lam_t.reshape(depth, 2, 2, NQ, GL * P).transpose(0, 3, 1, 2, 4)
    return dmat.astype(BF16), emat, wmat, lamq


def _gelu_tanh(x):
    return 0.5 * x * (1.0 + jnp.tanh(math.sqrt(2.0 / math.pi) * (x + 0.044715 * (x * x * x))))


def _ssm_kernel(u_ref, d_ref, e_ref, w_ref, lam_ref, h0_ref, y_ref, hfin_ref,
                xf_scr, hb_scr, st_scr, y_scr, tq_scr, bq_scr, cq_scr, *, n_chunks, batch, seq):
    d = pl.program_id(1)
    T, GL = SSM_CHUNK, SSM_GL

    @pl.when(d == 0)
    def _():
        for s in range(batch):
            for tau in range(T):
                xf_scr[tau, pl.ds(s, n_chunks, stride=batch), :] = u_ref[pl.ds(s * seq + tau, n_chunks, stride=T), :]
        for j in range(T):
            for i in range(T):
                tq_scr[j * LANES:(j + 1) * LANES, i * LANES:(i + 1) * LANES] = d_ref[0, 0, i - j + T - 1].astype(BF16)

    ch_bits = SSM_GROUP.bit_length() - 1
    row_group = (lax.broadcasted_iota(jnp.int32, (SSM_FLAT, LANES), 0) >> ch_bits) & (GL - 1)
    col_group = (lax.broadcasted_iota(jnp.int32, (LANES, SSM_FLAT), 1) >> ch_bits) & (GL - 1)
    st_bits = SSM_STATE.bit_length() - 1
    lane_half = lax.broadcasted_iota(jnp.int32, (SSM_FLAT, LANES), 1) >> st_bits
    row_half = lax.broadcasted_iota(jnp.int32, (LANES, SSM_FLAT), 0) >> st_bits
    tiles_per_part = GL * SSM_STATE // LANES
    for part in range(2):
        emat, wmat = e_ref[0, 0, 0, part], w_ref[0, 0, 0, part]
        for m in range(tiles_per_part):
            ts = slice((part * tiles_per_part + m) * LANES, (part * tiles_per_part + m + 1) * LANES)
            bq_scr[:, ts] = jnp.where(row_group == 2 * m + lane_half, emat, 0.0).astype(BF16)
            cq_scr[ts, :] = jnp.where(col_group == 2 * m + row_half, wmat, 0.0).astype(BF16)

    xf = jnp.concatenate([xf_scr[tau] for tau in range(T)], axis=1).astype(BF16)
    hb_scr[...] = jnp.dot(xf, bq_scr[...], preferred_element_type=F32)
    HW = SSM_SW // 2
    lam_re, lam_im = lam_ref[0, 0, 0, 0:1, :], lam_ref[0, 0, 0, 1:2, :]

    def body(t, s):
        s_re, s_im = s
        c = jnp.where(d == 0, t, n_chunks - 1 - t)
        rows = pl.ds(pl.multiple_of(c * batch, batch), batch)
        st_scr[rows, 0:HW] = s_re
        st_scr[rows, HW:SSM_SW] = s_im
        return (lam_re * s_re - lam_im * s_im + hb_scr[rows, 0:HW],
                lam_re * s_im + lam_im * s_re + hb_scr[rows, HW:SSM_SW])

    h0 = h0_ref[0, 0, 0]
    f_re, f_im = lax.fori_loop(0, n_chunks, body, (h0[:, 0:HW], h0[:, HW:SSM_SW]))
    hfin_ref[0, 0] = jnp.concatenate([f_re, f_im], axis=1)
    part = jnp.dot(st_scr[...].astype(BF16), cq_scr[...], preferred_element_type=F32)

    @pl.when(d == 0)
    def _():
        y_scr[...] = part + jnp.dot(xf, tq_scr[...], preferred_element_type=F32)

    @pl.when(d == 1)
    def _():
        y = _gelu_tanh(y_scr[...] + part)
        for tau in range(T):
            xf_scr[tau] = y[:, tau * LANES:(tau + 1) * LANES]
        for s in range(batch):
            for tau in range(T):
                y_ref[pl.ds(s * seq + tau, n_chunks, stride=T), :] = xf_scr[tau, pl.ds(s, n_chunks, stride=batch), :]


def _ssm_call(u, dmat, emat, wmat, lamq, h0, layer, state_layer, batch, seq):
    n = batch * seq
    n_chunks = seq // SSM_CHUNK
    m = batch * n_chunks
    n_lags = 2 * SSM_CHUNK - 1
    return pl.pallas_call(
        functools.partial(_ssm_kernel, n_chunks=n_chunks, batch=batch, seq=seq),
        grid=(SSM_NQ, 2),
        in_specs=[
            pl.BlockSpec((n, LANES), lambda q, d: (0, q), pipeline_mode=pl.Buffered(1)),
            pl.BlockSpec((1, 1, n_lags, LANES, LANES), lambda q, d: (layer, q, 0, 0, 0)),
            pl.BlockSpec((1, 1, 1, 2, SSM_FLAT, LANES), lambda q, d: (layer, q, d, 0, 0, 0)),
            pl.BlockSpec((1, 1, 1, 2, LANES, SSM_FLAT), lambda q, d: (layer, q, d, 0, 0, 0)),
            pl.BlockSpec((1, 1, 1, 2, SSM_SW // 2), lambda q, d: (layer, q, d, 0, 0)),
            pl.BlockSpec((1, 1, 1, batch, SSM_SW), lambda q, d: (state_layer, q, d, 0, 0)),
        ],
        out_specs=[
            pl.BlockSpec((n, LANES), lambda q, d: (0, q)),
            pl.BlockSpec((1, 1, batch, SSM_SW), lambda q, d: (q, d, 0, 0)),
        ],
        out_shape=[
            jax.ShapeDtypeStruct((n, SSM_WIDTH), F32),
            jax.ShapeDtypeStruct((SSM_NQ, 2, batch, SSM_SW), F32),
        ],
        scratch_shapes=[pltpu.VMEM((SSM_CHUNK, m, LANES), F32), pltpu.VMEM((m, SSM_SW), F32),
                        pltpu.VMEM((m, SSM_SW), F32), pltpu.VMEM((m, SSM_FLAT), F32),
                        pltpu.VMEM((SSM_FLAT, SSM_FLAT), BF16), pltpu.VMEM((SSM_FLAT, SSM_SW), BF16),
                        pltpu.VMEM((SSM_SW, SSM_FLAT), BF16)],
        compiler_params=_cparams(("arbitrary", "arbitrary")),
        name="ssm",
    )(u, dmat, emat, wmat, lamq, h0)


def _outproj_kernel(attn_ref, ret_ref, zs_ref, gw_ref, gb_ref, w_ref, x_ref, g1_ref, o_ref):
    zs = zs_ref[...]
    gl = jnp.dot(zs.astype(BF16), gw_ref[0].astype(BF16), preferred_element_type=F32) + gb_ref[0]
    mix = jnp.concatenate([attn_ref[...], ret_ref[...], (zs * jax.nn.sigmoid(gl)).astype(BF16)], axis=1)
    tn = PROJ_COLS
    for jt in range(o_ref.shape[1] // tn):
        cs = slice(jt * tn, (jt + 1) * tn)
        acc = jnp.dot(mix, w_ref[0, :, cs], preferred_element_type=F32)
        o_ref[:, cs] = x_ref[:, cs] + g1_ref[0, :, cs] * acc


def _outproj_call(attn, ret, zs, glu_w, glu_b, w_out_bf16, x2d, gate, layer):
    n, d = x2d.shape
    tm = PROJ_ROWS
    rows_per_gate = n // gate.shape[0]
    return pl.pallas_call(
        _outproj_kernel,
        grid=(n // tm,),
        in_specs=[
            pl.BlockSpec((tm, ATTN_W), lambda i: (i, 0)),
            pl.BlockSpec((tm, RET_W), lambda i: (i, 0)),
            pl.BlockSpec((tm, SSM_WIDTH), lambda i: (i, 0)),
            pl.BlockSpec((1, SSM_WIDTH, SSM_WIDTH), lambda i: (layer, 0, 0)),
            pl.BlockSpec((1, 1, SSM_WIDTH), lambda i: (layer, 0, 0)),
            pl.BlockSpec((1, d, d), lambda i: (layer, 0, 0), pipeline_mode=pl.Buffered(1)),
            pl.BlockSpec((tm, d), lambda i: (i, 0)),
            pl.BlockSpec((1, 1, d), lambda i: (i * tm // rows_per_gate, 0, 0)),
        ],
        out_specs=pl.BlockSpec((tm, d), lambda i: (i, 0)),
        out_shape=jax.ShapeDtypeStruct((n, d), F32),
        compiler_params=_cparams(("parallel",)),
        name="outproj",
    )(attn, ret, zs, glu_w, glu_b, w_out_bf16, x2d, gate)


GATHER_ROWS = 512
TOKEN_SPLIT_BITS = 5
TOKEN_SPLIT = 1 << TOKEN_SPLIT_BITS


def _route_kernel(x_ref, g_ref, sh_ref, sc_ref, rw_ref, xg_ref, gs_ref, tok_ref, h_scr, *, cap, per_batch_mod):
    bb, t = x_ref.shape[0], x_ref.shape[1]
    E = N_EXPERTS
    R = bb * E
    rc = 256
    affs = []
    for bi in range(bb):
        mi = bi if per_batch_mod else 0
        logits = []
        for c0 in range(0, t, rc):
            x = x_ref[bi, c0:c0 + rc, :]
            ms = jnp.mean(x * x, axis=-1, keepdims=True)
            h = x * lax.rsqrt(ms + EPS) * g_ref[...]
            h = h * (1.0 + sc_ref[mi]) + sh_ref[mi]
            h_scr[bi, c0:c0 + rc, :] = h.astype(BF16)
            logits.append(lax.dot_general(rw_ref[0], h, (((1,), (1,)), ((), ())),
                                          preferred_element_type=F32, precision=HIGHEST))
        lg = jnp.concatenate(logits, axis=1)
        ex = jnp.exp(lg - jnp.max(lg, axis=0, keepdims=True))
        affs.append(ex / jnp.sum(ex, axis=0, keepdims=True))
    aff = jnp.concatenate(affs, axis=0)

    def count_ge(v):
        return jnp.sum(jnp.where(aff >= v, 1.0, 0.0), axis=1, keepdims=True)

    def bisect_bits(_, lohi):
        lo, hi = lohi
        mid = lo + ((hi - lo + 1) >> 1)
        ok = count_ge(lax.bitcast_convert_type(mid, F32)) >= float(cap)
        return jnp.where(ok, mid, lo), jnp.where(ok, hi, mid - 1)

    lo_b, _ = lax.fori_loop(0, 31, bisect_bits,
                            (jnp.zeros((R, 1), jnp.int32), jnp.full((R, 1), 0x7F800000, jnp.int32)))

    def bisect_val(_, lohi):
        lo, hi = lohi
        mid = lo + (hi - lo) * 0.5
        ok = count_ge(mid) >= float(cap)
        return jnp.where(ok, mid, lo), jnp.where(ok, hi, mid)

    _, hi_v = lax.fori_loop(0, 8, bisect_val,
                            (lax.bitcast_convert_type(lo_b, F32), lax.bitcast_convert_type(lo_b + 1, F32)))
    thr = jnp.max(jnp.where(aff < hi_v, aff, 0.0), axis=1, keepdims=True)
    gt = aff > thr
    eq = aff == thr
    need = float(cap) - jnp.sum(jnp.where(gt, 1.0, 0.0), axis=1, keepdims=True)
    tri = (lax.broadcasted_iota(jnp.int32, (LANES, LANES), 0)
           < lax.broadcasted_iota(jnp.int32, (LANES, LANES), 1)).astype(BF16)

    def prefix_count(flags):
        out, before = [], jnp.zeros((R, 1), F32)
        for c0 in range(0, t, LANES):
            blk = flags[:, c0:c0 + LANES]
            out.append(jnp.dot(blk.astype(BF16), tri, preferred_element_type=F32) + before)
            before = before + jnp.sum(blk, axis=1, keepdims=True)
        return jnp.concatenate(out, axis=1)

    sel = gt | (eq & (prefix_count(jnp.where(eq, 1.0, 0.0)) < need))
    pos = prefix_count(jnp.where(sel, 1.0, 0.0))
    base = ((lax.broadcasted_iota(jnp.int32, (R, t), 0) & (E - 1)) * cap).astype(F32)
    gpos = jnp.where(sel, pos + base, -1.0)
    a_hi = aff.astype(BF16)
    a_mid = (aff - a_hi.astype(F32)).astype(BF16)
    a_lo = (aff - a_hi.astype(F32) - a_mid.astype(F32)).astype(BF16)
    tok_i = lax.broadcasted_iota(jnp.int32, (8, t), 1)
    row_i = lax.broadcasted_iota(jnp.int32, (8, t), 0)
    tok_parts = jnp.where(row_i == 0, tok_i >> TOKEN_SPLIT_BITS,
                          jnp.where(row_i == 1, tok_i & (TOKEN_SPLIT - 1), 0))
    tok_parts = tok_parts.astype(F32).astype(BF16)
    n_grp = max(1, GATHER_ROWS // cap)
    for bi in range(bb):
        r0 = bi * E
        aff_parts = jnp.concatenate([a_hi[r0:r0 + E], a_mid[r0:r0 + E], a_lo[r0:r0 + E],
                                     jnp.zeros((LANES - 3 * E, t), BF16)], axis=0)
        for e0 in range(0, E, n_grp):
            onehots = []
            for e in range(e0, e0 + n_grp):
                slot = (e * cap + lax.broadcasted_iota(jnp.int32, (cap, t), 0)).astype(F32)
                onehots.append(jnp.where(gpos[r0 + e:r0 + e + 1, :] == slot, 1.0, 0.0).astype(BF16))
            onehot = jnp.concatenate(onehots, axis=0)
            xg = jnp.dot(onehot, h_scr[bi], preferred_element_type=F32)
            gate = _nt_dot(onehot, aff_parts)
            for k in range(n_grp):
                e, rs = e0 + k, slice(k * cap, (k + 1) * cap)
                xg_ref[e, bi] = xg[rs].astype(BF16)
                gs_ref[e, bi * cap:(bi + 1) * cap, :] = (
                    gate[rs, e:e + 1] + gate[rs, E + e:E + e + 1] + gate[rs, 2 * E + e:2 * E + e + 1])
            tk = _nt_dot(tok_parts, onehot)
            tok_ref[bi, :, e0 * cap:(e0 + n_grp) * cap] = TOKEN_SPLIT * tk[0:1, :] + tk[1:2, :]


ROUTE_TOKENS = 1024


def _route_call(x3d, g, shift, scale, router_wt, layer):
    batch, t, d = x3d.shape
    cap = EC_CAPACITY * t // N_EXPERTS
    bb = max(1, ROUTE_TOKENS // t)
    per_batch_mod = shift.shape[0] > 1
    mod_spec = pl.BlockSpec((bb if per_batch_mod else 1, 1, d), (lambda b: (b, 0, 0)) if per_batch_mod
                            else (lambda b: (0, 0, 0)))
    return pl.pallas_call(
        functools.partial(_route_kernel, cap=cap, per_batch_mod=per_batch_mod),
        grid=(batch // bb,),
        in_specs=[
            pl.BlockSpec((bb, t, d), lambda b: (b, 0, 0)),
            pl.BlockSpec((1, d), lambda b: (0, 0)),
            mod_spec, mod_spec,
            pl.BlockSpec((1, N_EXPERTS, d), lambda b: (layer, 0, 0)),
        ],
        out_specs=[
            pl.BlockSpec((N_EXPERTS, bb, cap, d), lambda b: (0, b, 0, 0)),
            pl.BlockSpec((N_EXPERTS, bb * cap, 1), lambda b: (0, b, 0)),
            pl.BlockSpec((bb, 1, N_EXPERTS * cap), lambda b: (b, 0, 0)),
        ],
        out_shape=[
            jax.ShapeDtypeStruct((N_EXPERTS, batch, cap, d), BF16),
            jax.ShapeDtypeStruct((N_EXPERTS, batch * cap, 1), F32),
            jax.ShapeDtypeStruct((batch, 1, N_EXPERTS * cap), F32),
        ],
        scratch_shapes=[pltpu.VMEM((bb, t, d), BF16)],
        compiler_params=_cparams(("parallel",)),
        name="route",
    )(x3d, g, shift, scale, router_wt)


EXPERT_UP_TILE = 512
EXPERT_DOWN_TILE = 512


def _expert_kernel(xa_ref, xb_ref, wg_ref, wu_ref, wd_ref, ga_ref, gb_ref, oa_ref, ob_ref, h_scr, *, n_up):
    s = pl.program_id(1)
    tf = EXPERT_UP_TILE
    for k in range(n_up):
        @pl.when(s == k)
        def _(k=k):
            wg, wu = wg_ref[0, 0].astype(BF16), wu_ref[0, 0].astype(BF16)
            for i, x_ref in enumerate((xa_ref, xb_ref)):
                x = x_ref[0]
                a = jnp.dot(x, wg, preferred_element_type=F32)
                b = jnp.dot(x, wu, preferred_element_type=F32)
                h_scr[i, :, k * tf:(k + 1) * tf] = (_silu(a) * b).astype(BF16)

    @pl.when(s >= n_up)
    def _():
        wd = wd_ref[0, 0].astype(BF16)
        for i, (g_ref, o_ref) in enumerate(((ga_ref, oa_ref), (gb_ref, ob_ref))):
            y = jnp.dot(h_scr[i], wd, preferred_element_type=F32)
            o_ref[0] = (y * g_ref[0]).astype(o_ref.dtype)


def _expert_call(xg_a, gate_a, xg_b, gate_b, w_gate, w_up, w_down, layer):
    E, m, d = xg_a.shape
    assert xg_b.shape == xg_a.shape
    ff = w_gate.shape[-1]
    tf, tn = EXPERT_UP_TILE, EXPERT_DOWN_TILE
    n_up, n_down = ff // tf, d // tn
    def up_expert(e, s):
        return jnp.where(s >= n_up, jnp.minimum(e + 1, E - 1), e)

    x_spec = pl.BlockSpec((1, m, d), lambda e, s: (up_expert(e, s), 0, 0))
    g_spec = pl.BlockSpec((1, m, 1), lambda e, s: (e, 0, 0))
    o_spec = pl.BlockSpec((1, m, tn), lambda e, s: (e, 0, jnp.maximum(s - n_up, 0)))
    up_spec = pl.BlockSpec((1, 1, d, tf), lambda e, s: (layer, up_expert(e, s), 0, jnp.where(s >= n_up, 0, s)))
    return pl.pallas_call(
        functools.partial(_expert_kernel, n_up=n_up),
        grid=(E, n_up + n_down),
        in_specs=[
            x_spec, x_spec, up_spec, up_spec,
            pl.BlockSpec((1, 1, ff, tn), lambda e, s: (layer, e, 0, jnp.maximum(s - n_up, 0))),
            g_spec, g_spec,
        ],
        out_specs=[o_spec, o_spec],
        out_shape=[jax.ShapeDtypeStruct((E, m, d), BF16), jax.ShapeDtypeStruct((E, m, d), BF16)],
        scratch_shapes=[pltpu.VMEM((2, m, ff), BF16)],
        compiler_params=_cparams(("parallel", "arbitrary")),
        name="expert",
    )(xg_a, xg_b, w_gate, w_up, w_down, gate_a, gate_b)


COMBINE_TILE_ELEMS = 1024 * 1024


def _combine_kernel(y_ref, tok_ref, x_ref, g2_ref, o_ref, oh_scr):
    j = pl.program_id(1)
    t = x_ref.shape[1]
    n_slots = tok_ref.shape[2]

    @pl.when(j == 0)
    def _():
        tok_i = lax.broadcasted_iota(jnp.int32, (t, n_slots), 0).astype(F32)
        oh_scr[...] = jnp.where(tok_i == tok_ref[0], 1.0, 0.0).astype(BF16)

    y = jnp.concatenate([y_ref[e, 0] for e in range(N_EXPERTS)], axis=0)
    moe = jnp.dot(oh_scr[...], y, preferred_element_type=F32)
    o_ref[0] = x_ref[0] + g2_ref[0] * moe


def _combine_call(y4d, tok, x3d, gate):
    batch, t, d = x3d.shape
    cap = y4d.shape[2]
    tn = min(d, COMBINE_TILE_ELEMS // t)
    nb = gate.shape[0]
    gate_idx = (lambda b, j: (b, 0, j)) if nb > 1 else (lambda b, j: (0, 0, j))
    return pl.pallas_call(
        _combine_kernel,
        grid=(batch, d // tn),
        in_specs=[
            pl.BlockSpec((N_EXPERTS, 1, cap, tn), lambda b, j: (0, b, 0, j)),
            pl.BlockSpec((1, 1, N_EXPERTS * cap), lambda b, j: (b, 0, 0)),
            pl.BlockSpec((1, t, tn), lambda b, j: (b, 0, j)),
            pl.BlockSpec((1, 1, tn), gate_idx),
        ],
        out_specs=pl.BlockSpec((1, t, tn), lambda b, j: (b, 0, j)),
        out_shape=jax.ShapeDtypeStruct((batch, t, d), F32),
        scratch_shapes=[pltpu.VMEM((t, N_EXPERTS * cap), BF16)],
        compiler_params=_cparams(("parallel", "arbitrary")),
        name="combine",
    )(y4d, tok, x3d, gate)


def _norm_kernel(x_ref, g_ref, o_ref):
    x = x_ref[...]
    o_ref[...] = x * lax.rsqrt(jnp.mean(x * x, axis=-1, keepdims=True) + EPS) * g_ref[...]


def _norm_call(x2d, g):
    n, d = x2d.shape
    tm = 512
    return pl.pallas_call(
        _norm_kernel,
        grid=(n // tm,),
        in_specs=[pl.BlockSpec((tm, d), lambda i: (i, 0)), pl.BlockSpec((1, d), lambda i: (0, 0))],
        out_specs=pl.BlockSpec((tm, d), lambda i: (i, 0)),
        out_shape=jax.ShapeDtypeStruct((n, d), F32),
        compiler_params=_cparams(("parallel",)),
        name="final_norm",
    )(x2d, g)


def _rope_tables(n_tokens):
    n_rows = n_tokens // GRID_W
    row = jnp.repeat(jnp.arange(n_rows), GRID_W).astype(F32)
    col = jnp.tile(jnp.arange(GRID_W), n_rows).astype(F32)
    axis_dim = HEAD_DIM // 2
    inv_freq = ROPE_BASE ** (-jnp.arange(0, axis_dim, 2, dtype=F32) / axis_dim)
    ang_r, ang_c = row[:, None] * inv_freq[None, :], col[:, None] * inv_freq[None, :]
    cos = jnp.concatenate([jnp.cos(ang_r), jnp.cos(ang_r), jnp.cos(ang_c), jnp.cos(ang_c)], axis=1)
    sin = jnp.concatenate([-jnp.sin(ang_r), jnp.sin(ang_r), -jnp.sin(ang_c), jnp.sin(ang_c)], axis=1)
    return cos, sin


def _mix_and_route(x3d, layer, mods, w, ssm_mats, rope, ctx):
    batch, seq, d = x3d.shape
    n = batch * seq
    sh1, sc1, g1, sh2, sc2, _ = mods
    x2d = x3d.reshape(n, d)
    latent = ctx is not None
    z, u = _inproj_call(x2d, w['norm1_g'][layer][None], sh1, sc1, w['w_in'], layer, rope[0], rope[1], latent, seq)
    if latent:
        attn = _attn_lat_call(z, w['attn_sink'][layer], ctx['k'], ctx['v'], layer, batch, seq)
        k_new = v_new = None
        r0, h0, state_layer = ctx['ret'], ctx['ssm'], layer
    else:
        attn, k_new, v_new = _attn_ctx_call(z, w['attn_sink'][layer], batch, seq)
        r0, h0, state_layer = None, jnp.zeros((1, SSM_NQ, 2, batch, SSM_SW), F32), 0
    ret, r_fin = _ret_call(z, w['ret_log_gamma'][layer], w['ret_gn_g'][layer][None], r0, state_layer, batch, seq)
    zs, h_fin = _ssm_call(u, *ssm_mats, h0, layer, state_layer, batch, seq)
    x1 = _outproj_call(attn, ret, zs, w['ssm_glu_w'], w['ssm_glu_b'], w['w_out'], x2d, g1, layer)
    x1 = x1.reshape(batch, seq, d)
    xg, gate_slot, tok = _route_call(x1, w['norm2_g'][layer][None], sh2, sc2, w['router_wt'], layer)
    return x1, (xg, gate_slot, tok), (k_new, v_new, r_fin, h_fin)


def kernel(x_prompt, x_sample, cache_attn_k, cache_attn_v, state_ret, state_ssm_re, state_ssm_im, c, c_ctx, mod_w, mod_b, norm1_g, norm2_g, w_in, w_out, attn_sink, ret_log_gamma, ret_gn_g, ssm_lam_re, ssm_lam_im, ssm_log_step, ssm_b_re, ssm_b_im, ssm_c_re, ssm_c_im, ssm_d, ssm_glu_w, ssm_glu_b, router_w, moe_w_gate, moe_w_up, moe_w_down, final_norm_g):
    depth = w_in.shape[0]
    batch, seq, d = x_prompt.shape
    dec_batch, dec_seq, _ = x_sample.shape
    past = cache_attn_k.shape[2]
    G, P = N_SSM_GROUPS, SSM_STATE
    assert dec_seq % PROJ_ROWS == 0 and (batch * seq) % PROJ_ROWS == 0

    cv = jnp.concatenate([c_ctx[None, :], c, jnp.zeros((MOD_ROWS - 1 - dec_batch, d), F32)], axis=0)
    mod = _mod_call(cv, mod_w, mod_b)
    ssm_mats = _ssm_tile_matrices(*_ssm_group_matrices(
        ssm_lam_re, ssm_lam_im, ssm_log_step, ssm_b_re, ssm_b_im, ssm_c_re, ssm_c_im, ssm_d))
    rope = _rope_tables(dec_seq)
    no_rope = (jnp.ones((PROJ_ROWS, LANES), F32), jnp.zeros((PROJ_ROWS, LANES), F32))
    w = {
        'norm1_g': norm1_g, 'norm2_g': norm2_g, 'w_in': w_in.astype(BF16), 'w_out': w_out.astype(BF16),
        'attn_sink': attn_sink,
        'ret_log_gamma': ret_log_gamma, 'ret_gn_g': ret_gn_g, 'ssm_glu_w': ssm_glu_w,
        'ssm_glu_b': ssm_glu_b.reshape(depth, 1, SSM_WIDTH), 'router_wt': router_w.transpose(0, 2, 1),
    }
    h0_lat = jnp.concatenate([state_ssm_re.reshape(dec_batch, depth, 2, SSM_NQ, SSM_SW // 2),
                              state_ssm_im.reshape(dec_batch, depth, 2, SSM_NQ, SSM_SW // 2)], axis=-1)
    h0_lat = h0_lat.transpose(1, 3, 2, 0, 4)
    ctx = {
        'k': cache_attn_k.reshape(dec_batch, depth, past, KV_W),
        'v': cache_attn_v.reshape(dec_batch, depth, past, KV_W),
        'ret': state_ret, 'ssm': h0_lat,
    }
    xp, xs = x_prompt, x_sample
    ks, vs, rets, ssms = [], [], [], []
    for layer in range(depth):
        m = mod[layer]
        mods_p = [m[0:1, i * d:(i + 1) * d][:, None, :] for i in range(6)]
        mods_s = [m[1:1 + dec_batch, i * d:(i + 1) * d][:, None, :] for i in range(6)]
        xp1, (xg_p, gs_p, tok_p), (k_l, v_l, r_l, h_l) = _mix_and_route(xp, layer, mods_p, w, ssm_mats, no_rope, None)
        xs1, (xg_s, gs_s, tok_s), _ = _mix_and_route(xs, layer, mods_s, w, ssm_mats, rope, ctx)
        cap_p, cap_s = xg_p.shape[2], xg_s.shape[2]
        y_p, y_s = _expert_call(
            xg_p.reshape(N_EXPERTS, batch * cap_p, d), gs_p,
            xg_s.reshape(N_EXPERTS, dec_batch * cap_s, d), gs_s,
            moe_w_gate, moe_w_up, moe_w_down, layer)
        xp = _combine_call(y_p.reshape(N_EXPERTS, batch, cap_p, d), tok_p, xp1, mods_p[5])
        xs = _combine_call(y_s.reshape(N_EXPERTS, dec_batch, cap_s, d), tok_s, xs1, mods_s[5])
        ks.append(k_l.reshape(batch, seq, N_KV_HEADS, HEAD_DIM))
        vs.append(v_l.reshape(batch, seq, N_KV_HEADS, HEAD_DIM))
        rets.append(r_l)
        ssms.append(h_l)
    y_prompt = _norm_call(xp.reshape(batch * seq, d), final_norm_g[None]).reshape(batch, seq, d)
    y_sample = _norm_call(xs.reshape(dec_batch * dec_seq, d), final_norm_g[None]).reshape(dec_batch, dec_seq, d)
    h_all = jnp.stack(ssms, axis=0).reshape(depth, SSM_NQ, 2, batch, 2, SSM_GL, P)
    h_all = h_all.transpose(4, 3, 0, 2, 1, 5, 6).reshape(2, batch, depth, 2, G, P)
    return (y_prompt, y_sample, jnp.stack(ks, axis=1), jnp.stack(vs, axis=1), jnp.stack(rets, axis=1),
            h_all[0], h_all[1])
```

```python
import functools
import math

import jax
import jax.numpy as jnp
from jax import lax
from jax.experimental import pallas as pl
from jax.experimental.pallas import tpu as pltpu

F32 = jnp.float32
BF16 = jnp.bfloat16
HIGHEST = lax.Precision.HIGHEST

D_MODEL = 2048
GRID_W = 64
EPS = 1e-6
HEAD_DIM = 128
ATTN_W = D_MODEL // 2
N_HEADS = ATTN_W // HEAD_DIM
N_KV_HEADS = N_HEADS // 4
Q_PER_KV = N_HEADS // N_KV_HEADS
KV_W = N_KV_HEADS * HEAD_DIM
ATTN_BLOCK = 128
RET_W = D_MODEL // 4
RET_DK = 128
N_RET_HEADS = RET_W // RET_DK
RET_CHUNK = 128
SSM_WIDTH = D_MODEL // 4
SSM_GROUP = 16
N_SSM_GROUPS = SSM_WIDTH // SSM_GROUP
SSM_STATE = 64
N_EXPERTS = 16
EXPERT_FF = D_MODEL // 2
EC_CAPACITY = 2
ROPE_BASE = 10000.0
IN_W = ATTN_W + 2 * KV_W + 4 * RET_W + SSM_WIDTH

LANES = 128
HEAD_Q0 = 0
HEAD_K0 = ATTN_W // LANES
HEAD_V0 = HEAD_K0 + KV_W // LANES
HEAD_RQ0 = HEAD_V0 + KV_W // LANES
HEAD_RK0 = HEAD_RQ0 + RET_W // LANES
HEAD_RV0 = HEAD_RK0 + RET_W // LANES
HEAD_RG0 = HEAD_RV0 + RET_W // LANES
HEAD_U0 = HEAD_RG0 + RET_W // LANES

SSM_CHUNK = 8
SSM_GL = LANES // SSM_GROUP
SSM_NQ = SSM_WIDTH // LANES
SSM_FLAT = SSM_CHUNK * LANES
SSM_SW = SSM_GL * 2 * SSM_STATE

MOD_ROWS = 16
PROJ_ROWS = 512
PROJ_COLS = 512
VMEM_LIMIT = 56 * 1024 * 1024


def _cparams(sem):
    return pltpu.CompilerParams(dimension_semantics=sem, vmem_limit_bytes=VMEM_LIMIT)


def _silu(x):
    return x * jax.nn.sigmoid(x)


def _nt_dot(a, b):
    return lax.dot_general(a, b, (((1,), (1,)), ((), ())), preferred_element_type=F32)


def _tn_dot(a, b):
    return lax.dot_general(a, b, (((0,), (0,)), ((), ())), preferred_element_type=F32)


def _mod_kernel(cv_ref, w_ref, b_ref, o_ref):
    a = _silu(cv_ref[...])
    o_ref[0] = jnp.dot(a, w_ref[0], preferred_element_type=F32, precision=HIGHEST) + b_ref[0]


def _mod_call(cv, mod_w, mod_b):
    depth, d, e = mod_w.shape
    tn = 1024
    return pl.pallas_call(
        _mod_kernel,
        grid=(depth, e // tn),
        in_specs=[
            pl.BlockSpec((MOD_ROWS, d), lambda l, j: (0, 0)),
            pl.BlockSpec((1, d, tn), lambda l, j: (l, 0, j)),
            pl.BlockSpec((1, 1, tn), lambda l, j: (l, 0, j)),
        ],
        out_specs=pl.BlockSpec((1, MOD_ROWS, tn), lambda l, j: (l, 0, j)),
        out_shape=jax.ShapeDtypeStruct((depth, MOD_ROWS, e), F32),
        compiler_params=_cparams(("parallel", "parallel")),
        name="mod",
    )(cv, mod_w, mod_b.reshape(depth, 1, e))


def _rope_head(z, cos, sin_signed):
    lane = lax.broadcasted_iota(jnp.int32, z.shape, 1)
    partner = jnp.where((lane % 64) < 32, pltpu.roll(z, LANES - 32, 1), pltpu.roll(z, 32, 1))
    return z * cos + partner * sin_signed


def _inproj_kernel(x_ref, g_ref, sh_ref, sc_ref, w_ref, cos_ref, sin_ref, z_ref, u_ref, *, use_rope):
    x = x_ref[...]
    ms = jnp.mean(x * x, axis=-1, keepdims=True)
    hn = x * lax.rsqrt(ms + EPS) * g_ref[...]
    h = (hn * (1.0 + sc_ref[0]) + sh_ref[0]).astype(BF16)
    tn = PROJ_COLS
    heads_per_tile = tn // LANES
    for jt in range(IN_W // tn):
        acc = jnp.dot(h, w_ref[0, :, jt * tn:(jt + 1) * tn], preferred_element_type=F32)
        head0 = jt * heads_per_tile
        if head0 >= HEAD_U0:
            u_ref[:, (head0 - HEAD_U0) * LANES:(head0 - HEAD_U0) * LANES + tn] = acc
            continue
        for k in range(heads_per_tile):
            head = head0 + k
            zk = acc[:, k * LANES:(k + 1) * LANES]
            if HEAD_RK0 <= head < HEAD_RV0:
                zk = zk * RET_DK ** -0.5
            if use_rope and (head < HEAD_V0 or HEAD_RQ0 <= head < HEAD_RV0):
                zk = _rope_head(zk, cos_ref[...], sin_ref[...])
            z_ref[:, head * LANES:(head + 1) * LANES] = zk.astype(z_ref.dtype)


def _inproj_call(x2d, g, shift, scale, w_in_bf16, layer, cos, sin_signed, use_rope, seq):
    n, d = x2d.shape
    tm = PROJ_ROWS
    z_w = IN_W - SSM_WIDTH
    nb = shift.shape[0]
    rows_per_mod = n // nb
    tiles_per_seq = seq // tm
    mod_idx = lambda i: (i * tm // rows_per_mod, 0, 0)
    pos_idx = (lambda i: (i % tiles_per_seq, 0)) if use_rope else (lambda i: (0, 0))
    return pl.pallas_call(
        functools.partial(_inproj_kernel, use_rope=use_rope),
        grid=(n // tm,),
        in_specs=[
            pl.BlockSpec((tm, d), lambda i: (i, 0)),
            pl.BlockSpec((1, d), lambda i: (0, 0)),
            pl.BlockSpec((1, 1, d), mod_idx),
            pl.BlockSpec((1, 1, d), mod_idx),
            pl.BlockSpec((1, d, IN_W), lambda i: (layer, 0, 0), pipeline_mode=pl.Buffered(1)),
            pl.BlockSpec((tm, LANES), pos_idx),
            pl.BlockSpec((tm, LANES), pos_idx),
        ],
        out_specs=[
            pl.BlockSpec((tm, z_w), lambda i: (i, 0)),
            pl.BlockSpec((tm, SSM_WIDTH), lambda i: (i, 0)),
        ],
        out_shape=[jax.ShapeDtypeStruct((n, z_w), BF16), jax.ShapeDtypeStruct((n, SSM_WIDTH), F32)],
        compiler_params=_cparams(("parallel",)),
        name="inproj",
    )(x2d, g, shift, scale, w_in_bf16, cos, sin_signed)


def _sink_attend(s, sink_col, v):
    m = jnp.maximum(jnp.max(s, axis=-1, keepdims=True), sink_col)
    e = jnp.exp(s - m).astype(BF16)
    v_aug = jnp.concatenate([v, jnp.ones(v.shape, v.dtype)], axis=1)
    oa = jnp.dot(e, v_aug, preferred_element_type=F32)
    den = oa[:, HEAD_DIM:HEAD_DIM + 1] + jnp.exp(sink_col - m)
    return oa[:, :HEAD_DIM] * (1.0 / den)


def _stack_q(q_ref, kh):
    return jnp.concatenate(
        [q_ref[:, (kh * Q_PER_KV + g) * HEAD_DIM:(kh * Q_PER_KV + g + 1) * HEAD_DIM] for g in range(Q_PER_KV)], axis=0)


def _sink_col(sink_ref, kh, rows):
    return jnp.concatenate(
        [jnp.full((rows, 1), sink_ref[kh * Q_PER_KV + g], F32) for g in range(Q_PER_KV)], axis=0)


def _attn_ctx_kernel(sink_ref, q_ref, k_ref, v_ref, k_all_ref, v_all_ref, o_ref, kn_ref, vn_ref):
    del k_all_ref, v_all_ref
    rows = q_ref.shape[0]
    scale = HEAD_DIM ** -0.5
    for kh in range(N_KV_HEADS):
        hs = slice(kh * HEAD_DIM, (kh + 1) * HEAD_DIM)
        s = _nt_dot(_stack_q(q_ref, kh), k_ref[:, hs]) * scale
        o = _sink_attend(s, _sink_col(sink_ref, kh, rows), v_ref[:, hs])
        for g in range(Q_PER_KV):
            c0 = (kh * Q_PER_KV + g) * HEAD_DIM
            o_ref[:, c0:c0 + HEAD_DIM] = o[g * rows:(g + 1) * rows].astype(o_ref.dtype)
    for kh in range(N_KV_HEADS):
        hs = slice(kh * HEAD_DIM, (kh + 1) * HEAD_DIM)
        kn_ref[0, 0, :, kh, :] = k_ref[:, hs].astype(F32)
        vn_ref[0, 0, :, kh, :] = v_ref[:, hs].astype(F32)


def _attn_ctx_call(z, sink, k_all, v_all, layer, batch, seq):
    n = z.shape[0]
    kcol, vcol = HEAD_K0 * LANES // KV_W, HEAD_V0 * LANES // KV_W
    kv_out = pl.BlockSpec((1, 1, seq, N_KV_HEADS, HEAD_DIM), lambda b: (b, layer, 0, 0, 0))
    return pl.pallas_call(
        _attn_ctx_kernel,
        grid=(batch,),
        in_specs=[
            pl.BlockSpec(memory_space=pltpu.SMEM),
            pl.BlockSpec((seq, ATTN_W), lambda b: (b, 0)),
            pl.BlockSpec((seq, KV_W), lambda b: (b, kcol)),
            pl.BlockSpec((seq, KV_W), lambda b: (b, vcol)),
            pl.BlockSpec(memory_space=pl.ANY),
            pl.BlockSpec(memory_space=pl.ANY),
        ],
        out_specs=[pl.BlockSpec((seq, ATTN_W), lambda b: (b, 0)), kv_out, kv_out],
        out_shape=[
            jax.ShapeDtypeStruct((n, ATTN_W), BF16),
            jax.ShapeDtypeStruct(k_all.shape, F32),
            jax.ShapeDtypeStruct(v_all.shape, F32),
        ],
        input_output_aliases={4: 1, 5: 2},
        compiler_params=_cparams(("parallel",)),
        name="attn_ctx",
    )(sink, z, z, z, k_all, v_all)


def _attn_lat_kernel(sink_ref, q_ref, kp_ref, kc_ref, kn_ref, vp_ref, vc_ref, vn_ref, kx_ref, vx_ref, o_ref, *, nblk):
    i = pl.program_id(1)
    rows = q_ref.shape[0]
    scale = HEAD_DIM ** -0.5
    r = lax.broadcasted_iota(jnp.int32, (rows, ATTN_BLOCK), 0)
    c = lax.broadcasted_iota(jnp.int32, (rows, ATTN_BLOCK), 1)
    neg = jnp.float32(-jnp.inf)
    bias_prev = jnp.where(c >= r, jnp.where(i > 0, 0.0, neg), neg)
    bias_next = jnp.where(c <= r, jnp.where(i < nblk - 1, 0.0, neg), neg)
    n_ctx = kx_ref.shape[2]
    bias = jnp.concatenate([bias_prev, jnp.zeros((rows, ATTN_BLOCK), F32), bias_next,
                            jnp.zeros((rows, n_ctx), F32)], axis=1)
    bias = jnp.concatenate([bias] * Q_PER_KV, axis=0)
    for kh in range(N_KV_HEADS):
        hs = slice(kh * HEAD_DIM, (kh + 1) * HEAD_DIM)
        kcat = jnp.concatenate([kp_ref[:, hs], kc_ref[:, hs], kn_ref[:, hs], kx_ref[0, 0, :, hs].astype(BF16)], axis=0)
        vcat = jnp.concatenate([vp_ref[:, hs], vc_ref[:, hs], vn_ref[:, hs], vx_ref[0, 0, :, hs].astype(BF16)], axis=0)
        s = _nt_dot(_stack_q(q_ref, kh), kcat) * scale + bias
        o = _sink_attend(s, _sink_col(sink_ref, kh, rows), vcat)
        for g in range(Q_PER_KV):
            c0 = (kh * Q_PER_KV + g) * HEAD_DIM
            o_ref[:, c0:c0 + HEAD_DIM] = o[g * rows:(g + 1) * rows].astype(o_ref.dtype)


def _attn_lat_call(z, sink, cache_k, cache_v, layer, batch, seq):
    n = z.shape[0]
    nblk = seq // ATTN_BLOCK
    kcol, vcol = HEAD_K0 * LANES // KV_W, HEAD_V0 * LANES // KV_W
    past = cache_k.shape[2]

    def kv_spec(col, off):
        return pl.BlockSpec((ATTN_BLOCK, KV_W),
                            lambda b, i: (b * nblk + jnp.clip(i + off, 0, nblk - 1), col))

    cache_spec = pl.BlockSpec((1, 1, past, KV_W), lambda b, i: (b, layer, 0, 0))
    return pl.pallas_call(
        functools.partial(_attn_lat_kernel, nblk=nblk),
        grid=(batch, nblk),
        in_specs=[
            pl.BlockSpec(memory_space=pltpu.SMEM),
            pl.BlockSpec((ATTN_BLOCK, ATTN_W), lambda b, i: (b * nblk + i, 0)),
            kv_spec(kcol, -1), kv_spec(kcol, 0), kv_spec(kcol, 1),
            kv_spec(vcol, -1), kv_spec(vcol, 0), kv_spec(vcol, 1),
            cache_spec, cache_spec,
        ],
        out_specs=pl.BlockSpec((ATTN_BLOCK, ATTN_W), lambda b, i: (b * nblk + i, 0)),
        out_shape=jax.ShapeDtypeStruct((n, ATTN_W), BF16),
        compiler_params=_cparams(("parallel", "parallel")),
        name="attn_lat",
    )(sink, z, z, z, z, z, z, z, cache_k, cache_v)


def _ret_kernel(lg_ref, q_ref, k_ref, v_ref, rg_ref, gn_ref, *rest, n_chunks, has_r0):
    r0_ref = rest[0] if has_r0 else None
    o_ref, rfin_ref, of_scr, ob_scr, r_scr, dec_scr, qd_scr, kd_scr, cd_scr = rest[1:] if has_r0 else rest
    C, dk = RET_CHUNK, RET_DK
    ii = lax.broadcasted_iota(jnp.int32, (C, C), 0).astype(F32)
    jj = lax.broadcasted_iota(jnp.int32, (C, C), 1).astype(F32)
    pos = lax.broadcasted_iota(jnp.int32, (C, dk), 0).astype(F32)
    for h in range(N_RET_HEADS):
        for d in range(2):
            lg = lg_ref[d, h]
            diff = (ii - jj) if d == 0 else (jj - ii)
            dec_scr[d, h] = jnp.where(diff >= 0, jnp.exp(jnp.maximum(diff, 0.0) * lg), 0.0)
            if d == 0:
                qd_scr[d, h] = jnp.exp((pos + 1.0) * lg)
                kd_scr[d, h] = jnp.exp((C - 1.0 - pos) * lg)
            else:
                qd_scr[d, h] = jnp.exp((C - pos) * lg)
                kd_scr[d, h] = jnp.exp(pos * lg)
            cd_scr[d, h] = jnp.exp(jnp.full((8, dk), float(C), F32) * lg)
            r_scr[d, h] = r0_ref[0, 0, d, h] if has_r0 else jnp.zeros((dk, dk), F32)

    def body(t, carry):
        for h in range(N_RET_HEADS):
            hs = slice(h * dk, (h + 1) * dk)
            for d in range(2):
                n = t if d == 0 else n_chunks - 1 - t
                rows = pl.ds(pl.multiple_of(n * C, C), C)
                q, k, v = q_ref[rows, hs], k_ref[rows, hs], v_ref[rows, hs]
                r = r_scr[d, h]
                s = _nt_dot(q, k) * dec_scr[d, h]
                intra = jnp.dot(s.astype(BF16), v, preferred_element_type=F32)
                cross = jnp.dot((q.astype(F32) * qd_scr[d, h]).astype(BF16), r.astype(BF16),
                                preferred_element_type=F32)
                kv = _tn_dot((k.astype(F32) * kd_scr[d, h]).astype(BF16), v)
                (of_scr if d == 0 else ob_scr)[rows, hs] = intra + cross
                r_scr[d, h] = cd_scr[d, h, 0:1, :] * r + kv
        return carry

    lax.fori_loop(0, n_chunks, body, 0)
    rfin_ref[0] = r_scr[...]
    for h in range(N_RET_HEADS):
        hs = slice(h * dk, (h + 1) * dk)
        o = of_scr[:, hs] + ob_scr[:, hs]
        mu = jnp.mean(o, axis=-1, keepdims=True)
        var = jnp.mean(jnp.square(o - mu), axis=-1, keepdims=True)
        on = (o - mu) * lax.rsqrt(var + EPS) * gn_ref[:, hs]
        o_ref[:, hs] = (_silu(rg_ref[:, hs].astype(F32)) * on).astype(o_ref.dtype)


def _ret_call(z, log_gamma, gn_g, r0, state_layer, batch, seq):
    n = z.shape[0]
    dk, H = RET_DK, N_RET_HEADS

    def col(head0):
        return pl.BlockSpec((seq, RET_W), lambda b: (b, head0 * LANES // RET_W))

    has_r0 = r0 is not None
    state_specs = [pl.BlockSpec((1, 1, 2, H, dk, dk), lambda b: (b, state_layer, 0, 0, 0, 0))] if has_r0 else []
    state_args = (r0,) if has_r0 else ()
    return pl.pallas_call(
        functools.partial(_ret_kernel, n_chunks=seq // RET_CHUNK, has_r0=has_r0),
        grid=(batch,),
        in_specs=[
            pl.BlockSpec(memory_space=pltpu.SMEM),
            col(HEAD_RQ0), col(HEAD_RK0), col(HEAD_RV0), col(HEAD_RG0),
            pl.BlockSpec((1, RET_W), lambda b: (0, 0)),
        ] + state_specs,
        out_specs=[
            pl.BlockSpec((seq, RET_W), lambda b: (b, 0)),
            pl.BlockSpec((1, 2, H, dk, dk), lambda b: (b, 0, 0, 0, 0)),
        ],
        out_shape=[
            jax.ShapeDtypeStruct((n, RET_W), BF16),
            jax.ShapeDtypeStruct((batch, 2, H, dk, dk), F32),
        ],
        scratch_shapes=[
            pltpu.VMEM((seq, RET_W), F32), pltpu.VMEM((seq, RET_W), F32),
            pltpu.VMEM((2, H, dk, dk), F32), pltpu.VMEM((2, H, RET_CHUNK, RET_CHUNK), F32),
            pltpu.VMEM((2, H, RET_CHUNK, dk), F32), pltpu.VMEM((2, H, RET_CHUNK, dk), F32),
            pltpu.VMEM((2, H, 8, dk), F32),
        ],
        compiler_params=_cparams(("parallel",)),
        name="ret",
    )(log_gamma, z, z, z, z, gn_g, *state_args)


def _ssm_group_matrices(lam_re, lam_im, log_step, b_re, b_im, c_re, c_im, d_skip):
    depth, _, G, P = lam_re.shape
    H, T = SSM_GROUP, SSM_CHUNK
    dt = jnp.exp(log_step)[..., None]
    ar, ai = lam_re * dt, lam_im * dt
    mag = jnp.exp(ar)
    lbr, lbi = mag * jnp.cos(ai), mag * jnp.sin(ai)
    den = lam_re * lam_re + lam_im * lam_im
    fr = ((lbr - 1.0) * lam_re + lbi * lam_im) / den
    fi = (lbi * lam_re - (lbr - 1.0) * lam_im) / den
    bbr = fr[..., None] * b_re - fi[..., None] * b_im
    bbi = fr[..., None] * b_im + fi[..., None] * b_re
    m = jnp.arange(T + 1, dtype=F32)[:, None]
    pmag = jnp.exp(ar[..., None, :] * m)
    pwr, pwi = pmag * jnp.cos(ai[..., None, :] * m), pmag * jnp.sin(ai[..., None, :] * m)
    xr = pwr[..., None] * bbr[..., None, :, :] - pwi[..., None] * bbi[..., None, :, :]
    xi = pwr[..., None] * bbi[..., None, :, :] + pwi[..., None] * bbr[..., None, :, :]
    kern = (jnp.einsum('ldghp,ldgmpi->ldgmhi', c_re, xr[..., :T, :, :], precision=HIGHEST)
            - jnp.einsum('ldghp,ldgmpi->ldgmhi', c_im, xi[..., :T, :, :], precision=HIGHEST))
    tok = jnp.arange(T)
    lag = tok[None, :] - tok[:, None]
    kf = kern[:, 0][:, :, jnp.clip(lag, 0, T - 1)] * (lag >= 0)[None, None, :, :, None, None].astype(F32)
    kb = kern[:, 1][:, :, jnp.clip(-lag, 0, T - 1)] * (lag <= 0)[None, None, :, :, None, None].astype(F32)
    tm = (kf + kb).transpose(0, 1, 2, 5, 3, 4)
    skip = (jnp.eye(T, dtype=F32)[None, None, :, None, :, None]
            * jnp.eye(H, dtype=F32)[None, None, None, :, None, :]
            * d_skip.reshape(depth, G, 1, H, 1, 1))
    tmat = (tm + skip).reshape(depth, G, T * H, T * H)
    rev = T - 1 - tok
    ef_r, ef_i = xr[:, 0][:, :, rev], xi[:, 0][:, :, rev]
    eb_r, eb_i = xr[:, 1][:, :, tok], xi[:, 1][:, :, tok]
    bmat = jnp.concatenate([e.transpose(0, 1, 2, 4, 3) for e in (ef_r, ef_i, eb_r, eb_i)], axis=-1)
    bmat = bmat.reshape(depth, G, T * H, 4 * P)

    def entry(d, powers):
        pr = pwr[:, d][:, :, powers][:, :, :, None, :]
        pi = pwi[:, d][:, :, powers][:, :, :, None, :]
        cr, ci = c_re[:, d][:, :, None], c_im[:, d][:, :, None]
        wr, wi = cr * pr - ci * pi, cr * pi + ci * pr
        return wr.transpose(0, 1, 4, 2, 3), -wi.transpose(0, 1, 4, 2, 3)

    cf_r, cf_i = entry(0, tok + 1)
    cb_r, cb_i = entry(1, T - tok)
    cmat = jnp.concatenate([cf_r, cf_i, cb_r, cb_i], axis=2).reshape(depth, G, 4 * P, T * H)
    lam_t = jnp.stack([pwr[..., T, :], pwi[..., T, :]], axis=2)
    return tmat, bmat, cmat, lam_t


def _ssm_tile_matrices(tmat, bmat, cmat, lam_t):
    depth = tmat.shape[0]
    T, H, GL, NQ, P = SSM_CHUNK, SSM_GROUP, SSM_GL, SSM_NQ, SSM_STATE
    tm = tmat.reshape(depth, N_SSM_GROUPS, T, H, T, H)
    lag_neg = tm[:, :, 1:, :, 0, :][:, :, ::-1]
    lag_pos = tm[:, :, 0].transpose(0, 1, 3, 2, 4)
    lags = jnp.concatenate([lag_neg, lag_pos], axis=2)
    lags = lags.reshape(depth, NQ, GL, 2 * T - 1, H, H).transpose(0, 1, 3, 2, 4, 5)
    eye = jnp.eye(GL, dtype=F32)[None, None, None, :, None, :, None]
    dmat = (lags[..., None, :] * eye).reshape(depth, NQ, 2 * T - 1, LANES, LANES)
    emat = bmat.astype(BF16).reshape(depth, NQ, GL, T, H, 2, 2, P).transpose(0, 1, 5, 6, 3, 2, 4, 7)
    emat = emat.reshape(depth, NQ, 2, 2, SSM_FLAT, 1, P)
    emat = jnp.broadcast_to(emat, (depth, NQ, 2, 2, SSM_FLAT, 2, P)).reshape(depth, NQ, 2, 2, SSM_FLAT, 2 * P)
    wmat = cmat.astype(BF16).reshape(depth, NQ, GL, 2, 2, P, T, H).transpose(0, 1, 3, 4, 5, 6, 2, 7)
    wmat = wmat.reshape(depth, NQ, 2, 2, 1, P, SSM_FLAT)
    wmat = jnp.broadcast_to(wmat, (depth, NQ, 2, 2, 2, P, SSM_FLAT)).reshape(depth, NQ, 2, 2, 2 * P, SSM_FLAT)
    lamq = lam_t.reshape(depth, 2, 2, NQ, GL * P).transpose(0, 3, 1, 2, 4)
    return dmat.astype(BF16), emat, wmat, lamq


def _gelu_tanh(x):
    return 0.5 * x * (1.0 + jnp.tanh(math.sqrt(2.0 / math.pi) * (x + 0.044715 * (x * x * x))))


def _ssm_kernel(u_ref, d_ref, e_ref, w_ref, lam_ref, h0_ref, y_ref, hfin_ref,
                xf_scr, hb_scr, st_scr, y_scr, tq_scr, bq_scr, cq_scr, *, n_chunks, batch, seq):
    d = pl.program_id(1)
    T, GL = SSM_CHUNK, SSM_GL

    @pl.when(d == 0)
    def _():
        for s in range(batch):
            for tau in range(T):
                xf_scr[tau, pl.ds(s, n_chunks, stride=batch), :] = u_ref[pl.ds(s * seq + tau, n_chunks, stride=T), :]
        for j in range(T):
            for i in range(T):
                tq_scr[j * LANES:(j + 1) * LANES, i * LANES:(i + 1) * LANES] = d_ref[0, 0, i - j + T - 1].astype(BF16)

    ch_bits = SSM_GROUP.bit_length() - 1
    row_group = (lax.broadcasted_iota(jnp.int32, (SSM_FLAT, LANES), 0) >> ch_bits) & (GL - 1)
    col_group = (lax.broadcasted_iota(jnp.int32, (LANES, SSM_FLAT), 1) >> ch_bits) & (GL - 1)
    st_bits = SSM_STATE.bit_length() - 1
    lane_half = lax.broadcasted_iota(jnp.int32, (SSM_FLAT, LANES), 1) >> st_bits
    row_half = lax.broadcasted_iota(jnp.int32, (LANES, SSM_FLAT), 0) >> st_bits
    tiles_per_part = GL * SSM_STATE // LANES
    for part in range(2):
        emat, wmat = e_ref[0, 0, 0, part], w_ref[0, 0, 0, part]
        for m in range(tiles_per_part):
            ts = slice((part * tiles_per_part + m) * LANES, (part * tiles_per_part + m + 1) * LANES)
            bq_scr[:, ts] = jnp.where(row_group == 2 * m + lane_half, emat, 0.0).astype(BF16)
            cq_scr[ts, :] = jnp.where(col_group == 2 * m + row_half, wmat, 0.0).astype(BF16)

    xf = jnp.concatenate([xf_scr[tau] for tau in range(T)], axis=1).astype(BF16)
    hb_scr[...] = jnp.dot(xf, bq_scr[...], preferred_element_type=F32)
    HW = SSM_SW // 2
    lam_re, lam_im = lam_ref[0, 0, 0, 0:1, :], lam_ref[0, 0, 0, 1:2, :]

    def body(t, s):
        s_re, s_im = s
        c = jnp.where(d == 0, t, n_chunks - 1 - t)
        rows = pl.ds(pl.multiple_of(c * batch, batch), batch)
        st_scr[rows, 0:HW] = s_re
        st_scr[rows, HW:SSM_SW] = s_im
        return (lam_re * s_re - lam_im * s_im + hb_scr[rows, 0:HW],
                lam_re * s_im + lam_im * s_re + hb_scr[rows, HW:SSM_SW])

    h0 = h0_ref[0, 0, 0]
    f_re, f_im = lax.fori_loop(0, n_chunks, body, (h0[:, 0:HW], h0[:, HW:SSM_SW]))
    hfin_ref[0, 0] = jnp.concatenate([f_re, f_im], axis=1)
    part = jnp.dot(st_scr[...].astype(BF16), cq_scr[...], preferred_element_type=F32)

    @pl.when(d == 0)
    def _():
        y_scr[...] = part + jnp.dot(xf, tq_scr[...], preferred_element_type=F32)

    @pl.when(d == 1)
    def _():
        y = _gelu_tanh(y_scr[...] + part)
        for tau in range(T):
            xf_scr[tau] = y[:, tau * LANES:(tau + 1) * LANES]
        for s in range(batch):
            for tau in range(T):
                y_ref[pl.ds(s * seq + tau, n_chunks, stride=T), :] = xf_scr[tau, pl.ds(s, n_chunks, stride=batch), :]


def _ssm_call(u, dmat, emat, wmat, lamq, h0, layer, state_layer, batch, seq):
    n = batch * seq
    n_chunks = seq // SSM_CHUNK
    m = batch * n_chunks
    n_lags = 2 * SSM_CHUNK - 1
    return pl.pallas_call(
        functools.partial(_ssm_kernel, n_chunks=n_chunks, batch=batch, seq=seq),
        grid=(SSM_NQ, 2),
        in_specs=[
            pl.BlockSpec((n, LANES), lambda q, d: (0, q), pipeline_mode=pl.Buffered(1)),
            pl.BlockSpec((1, 1, n_lags, LANES, LANES), lambda q, d: (layer, q, 0, 0, 0)),
            pl.BlockSpec((1, 1, 1, 2, SSM_FLAT, LANES), lambda q, d: (layer, q, d, 0, 0, 0)),
            pl.BlockSpec((1, 1, 1, 2, LANES, SSM_FLAT), lambda q, d: (layer, q, d, 0, 0, 0)),
            pl.BlockSpec((1, 1, 1, 2, SSM_SW // 2), lambda q, d: (layer, q, d, 0, 0)),
            pl.BlockSpec((1, 1, 1, batch, SSM_SW), lambda q, d: (state_layer, q, d, 0, 0)),
        ],
        out_specs=[
            pl.BlockSpec((n, LANES), lambda q, d: (0, q)),
            pl.BlockSpec((1, 1, batch, SSM_SW), lambda q, d: (q, d, 0, 0)),
        ],
        out_shape=[
            jax.ShapeDtypeStruct((n, SSM_WIDTH), F32),
            jax.ShapeDtypeStruct((SSM_NQ, 2, batch, SSM_SW), F32),
        ],
        scratch_shapes=[pltpu.VMEM((SSM_CHUNK, m, LANES), F32), pltpu.VMEM((m, SSM_SW), F32),
                        pltpu.VMEM((m, SSM_SW), F32), pltpu.VMEM((m, SSM_FLAT), F32),
                        pltpu.VMEM((SSM_FLAT, SSM_FLAT), BF16), pltpu.VMEM((SSM_FLAT, SSM_SW), BF16),
                        pltpu.VMEM((SSM_SW, SSM_FLAT), BF16)],
        compiler_params=_cparams(("arbitrary", "arbitrary")),
        name="ssm",
    )(u, dmat, emat, wmat, lamq, h0)


def _outproj_kernel(attn_ref, ret_ref, zs_ref, gw_ref, gb_ref, w_ref, x_ref, g1_ref, o_ref):
    zs = zs_ref[...]
    gl = jnp.dot(zs.astype(BF16), gw_ref[0].astype(BF16), preferred_element_type=F32) + gb_ref[0]
    mix = jnp.concatenate([attn_ref[...], ret_ref[...], (zs * jax.nn.sigmoid(gl)).astype(BF16)], axis=1)
    tn = PROJ_COLS
    for jt in range(o_ref.shape[1] // tn):
        cs = slice(jt * tn, (jt + 1) * tn)
        acc = jnp.dot(mix, w_ref[0, :, cs], preferred_element_type=F32)
        o_ref[:, cs] = x_ref[:, cs] + g1_ref[0, :, cs] * acc


def _outproj_call(attn, ret, zs, glu_w, glu_b, w_out_bf16, x2d, gate, layer):
    n, d = x2d.shape
    tm = PROJ_ROWS
    rows_per_gate = n // gate.shape[0]
    return pl.pallas_call(
        _outproj_kernel,
        grid=(n // tm,),
        in_specs=[
            pl.BlockSpec((tm, ATTN_W), lambda i: (i, 0)),
            pl.BlockSpec((tm, RET_W), lambda i: (i, 0)),
            pl.BlockSpec((tm, SSM_WIDTH), lambda i: (i, 0)),
            pl.BlockSpec((1, SSM_WIDTH, SSM_WIDTH), lambda i: (layer, 0, 0)),
            pl.BlockSpec((1, 1, SSM_WIDTH), lambda i: (layer, 0, 0)),
            pl.BlockSpec((1, d, d), lambda i: (layer, 0, 0), pipeline_mode=pl.Buffered(1)),
            pl.BlockSpec((tm, d), lambda i: (i, 0)),
            pl.BlockSpec((1, 1, d), lambda i: (i * tm // rows_per_gate, 0, 0)),
        ],
        out_specs=pl.BlockSpec((tm, d), lambda i: (i, 0)),
        out_shape=jax.ShapeDtypeStruct((n, d), F32),
        compiler_params=_cparams(("parallel",)),
        name="outproj",
    )(attn, ret, zs, glu_w, glu_b, w_out_bf16, x2d, gate)


GATHER_ROWS = 512
TOKEN_SPLIT_BITS = 5
TOKEN_SPLIT = 1 << TOKEN_SPLIT_BITS


def _route_kernel(x_ref, g_ref, sh_ref, sc_ref, rw_ref, xg_ref, gs_ref, tok_ref, h_scr, *, cap, per_batch_mod):
    bb, t = x_ref.shape[0], x_ref.shape[1]
    E = N_EXPERTS
    R = bb * E
    rc = 256
    affs = []
    for bi in range(bb):
        mi = bi if per_batch_mod else 0
        logits = []
        for c0 in range(0, t, rc):
            x = x_ref[bi, c0:c0 + rc, :]
            ms = jnp.mean(x * x, axis=-1, keepdims=True)
            h = x * lax.rsqrt(ms + EPS) * g_ref[...]
            h = h * (1.0 + sc_ref[mi]) + sh_ref[mi]
            h_scr[bi, c0:c0 + rc, :] = h.astype(BF16)
            logits.append(lax.dot_general(rw_ref[0], h, (((1,), (1,)), ((), ())),
                                          preferred_element_type=F32, precision=HIGHEST))
        lg = jnp.concatenate(logits, axis=1)
        ex = jnp.exp(lg - jnp.max(lg, axis=0, keepdims=True))
        affs.append(ex / jnp.sum(ex, axis=0, keepdims=True))
    aff = jnp.concatenate(affs, axis=0)

    def count_ge(v):
        return jnp.sum(jnp.where(aff >= v, 1.0, 0.0), axis=1, keepdims=True)

    def bisect_bits(_, lohi):
        lo, hi = lohi
        mid = lo + ((hi - lo + 1) >> 1)
        ok = count_ge(lax.bitcast_convert_type(mid, F32)) >= float(cap)
        return jnp.where(ok, mid, lo), jnp.where(ok, hi, mid - 1)

    lo_b, _ = lax.fori_loop(0, 31, bisect_bits,
                            (jnp.zeros((R, 1), jnp.int32), jnp.full((R, 1), 0x7F800000, jnp.int32)))

    def bisect_val(_, lohi):
        lo, hi = lohi
        mid = lo + (hi - lo) * 0.5
        ok = count_ge(mid) >= float(cap)
        return jnp.where(ok, mid, lo), jnp.where(ok, hi, mid)

    _, hi_v = lax.fori_loop(0, 8, bisect_val,
                            (lax.bitcast_convert_type(lo_b, F32), lax.bitcast_convert_type(lo_b + 1, F32)))
    thr = jnp.max(jnp.where(aff < hi_v, aff, 0.0), axis=1, keepdims=True)
    gt = aff > thr
    eq = aff == thr
    need = float(cap) - jnp.sum(jnp.where(gt, 1.0, 0.0), axis=1, keepdims=True)
    tri = (lax.broadcasted_iota(jnp.int32, (LANES, LANES), 0)
           < lax.broadcasted_iota(jnp.int32, (LANES, LANES), 1)).astype(BF16)

    def prefix_count(flags):
        out, before = [], jnp.zeros((R, 1), F32)
        for c0 in range(0, t, LANES):
            blk = flags[:, c0:c0 + LANES]
            out.append(jnp.dot(blk.astype(BF16), tri, preferred_element_type=F32) + before)
            before = before + jnp.sum(blk, axis=1, keepdims=True)
        return jnp.concatenate(out, axis=1)

    sel = gt | (eq & (prefix_count(jnp.where(eq, 1.0, 0.0)) < need))
    pos = prefix_count(jnp.where(sel, 1.0, 0.0))
    base = ((lax.broadcasted_iota(jnp.int32, (R, t), 0) & (E - 1)) * cap).astype(F32)
    gpos = jnp.where(sel, pos + base, -1.0)
    a_hi = aff.astype(BF16)
    a_mid = (aff - a_hi.astype(F32)).astype(BF16)
    a_lo = (aff - a_hi.astype(F32) - a_mid.astype(F32)).astype(BF16)
    tok_i = lax.broadcasted_iota(jnp.int32, (8, t), 1)
    row_i = lax.broadcasted_iota(jnp.int32, (8, t), 0)
    tok_parts = jnp.where(row_i == 0, tok_i >> TOKEN_SPLIT_BITS,
                          jnp.where(row_i == 1, tok_i & (TOKEN_SPLIT - 1), 0))
    tok_parts = tok_parts.astype(F32).astype(BF16)
    n_grp = max(1, GATHER_ROWS // cap)
    for bi in range(bb):
        r0 = bi * E
        aff_parts = jnp.concatenate([a_hi[r0:r0 + E], a_mid[r0:r0 + E], a_lo[r0:r0 + E],
                                     jnp.zeros((LANES - 3 * E, t), BF16)], axis=0)
        for e0 in range(0, E, n_grp):
            onehots = []
            for e in range(e0, e0 + n_grp):
                slot = (e * cap + lax.broadcasted_iota(jnp.int32, (cap, t), 0)).astype(F32)
                onehots.append(jnp.where(gpos[r0 + e:r0 + e + 1, :] == slot, 1.0, 0.0).astype(BF16))
            onehot = jnp.concatenate(onehots, axis=0)
            xg = jnp.dot(onehot, h_scr[bi], preferred_element_type=F32)
            gate = _nt_dot(onehot, aff_parts)
            for k in range(n_grp):
                e, rs = e0 + k, slice(k * cap, (k + 1) * cap)
                xg_ref[e, bi] = xg[rs].astype(BF16)
                gs_ref[e, bi * cap:(bi + 1) * cap, :] = (
                    gate[rs, e:e + 1] + gate[rs, E + e:E + e + 1] + gate[rs, 2 * E + e:2 * E + e + 1])
            tk = _nt_dot(tok_parts, onehot)
            tok_ref[bi, :, e0 * cap:(e0 + n_grp) * cap] = TOKEN_SPLIT * tk[0:1, :] + tk[1:2, :]


ROUTE_TOKENS = 1024


def _route_call(x3d, g, shift, scale, router_wt, layer):
    batch, t, d = x3d.shape
    cap = EC_CAPACITY * t // N_EXPERTS
    bb = max(1, ROUTE_TOKENS // t)
    per_batch_mod = shift.shape[0] > 1
    mod_spec = pl.BlockSpec((bb if per_batch_mod else 1, 1, d), (lambda b: (b, 0, 0)) if per_batch_mod
                            else (lambda b: (0, 0, 0)))
    return pl.pallas_call(
        functools.partial(_route_kernel, cap=cap, per_batch_mod=per_batch_mod),
        grid=(batch // bb,),
        in_specs=[
            pl.BlockSpec((bb, t, d), lambda b: (b, 0, 0)),
            pl.BlockSpec((1, d), lambda b: (0, 0)),
            mod_spec, mod_spec,
            pl.BlockSpec((1, N_EXPERTS, d), lambda b: (layer, 0, 0)),
        ],
        out_specs=[
            pl.BlockSpec((N_EXPERTS, bb, cap, d), lambda b: (0, b, 0, 0)),
            pl.BlockSpec((N_EXPERTS, bb * cap, 1), lambda b: (0, b, 0)),
            pl.BlockSpec((bb, 1, N_EXPERTS * cap), lambda b: (b, 0, 0)),
        ],
        out_shape=[
            jax.ShapeDtypeStruct((N_EXPERTS, batch, cap, d), BF16),
            jax.ShapeDtypeStruct((N_EXPERTS, batch * cap, 1), F32),
            jax.ShapeDtypeStruct((batch, 1, N_EXPERTS * cap), F32),
        ],
        scratch_shapes=[pltpu.VMEM((bb, t, d), BF16)],
        compiler_params=_cparams(("parallel",)),
        name="route",
    )(x3d, g, shift, scale, router_wt)


EXPERT_UP_TILE = 512
EXPERT_DOWN_TILE = 512


def _expert_kernel(xa_ref, xb_ref, wg_ref, wu_ref, wd_ref, ga_ref, gb_ref, oa_ref, ob_ref, h_scr, *, n_up):
    s = pl.program_id(1)
    tf = EXPERT_UP_TILE
    for k in range(n_up):
        @pl.when(s == k)
        def _(k=k):
            wg, wu = wg_ref[0, 0].astype(BF16), wu_ref[0, 0].astype(BF16)
            for i, x_ref in enumerate((xa_ref, xb_ref)):
                x = x_ref[0]
                a = jnp.dot(x, wg, preferred_element_type=F32)
                b = jnp.dot(x, wu, preferred_element_type=F32)
                h_scr[i, :, k * tf:(k + 1) * tf] = (_silu(a) * b).astype(BF16)

    @pl.when(s >= n_up)
    def _():
        wd = wd_ref[0, 0].astype(BF16)
        for i, (g_ref, o_ref) in enumerate(((ga_ref, oa_ref), (gb_ref, ob_ref))):
            y = jnp.dot(h_scr[i], wd, preferred_element_type=F32)
            o_ref[0] = (y * g_ref[0]).astype(o_ref.dtype)


def _expert_call(xg_a, gate_a, xg_b, gate_b, w_gate, w_up, w_down, layer):
    E, m, d = xg_a.shape
    assert xg_b.shape == xg_a.shape
    ff = w_gate.shape[-1]
    tf, tn = EXPERT_UP_TILE, EXPERT_DOWN_TILE
    n_up, n_down = ff // tf, d // tn
    def up_expert(e, s):
        return jnp.where(s >= n_up, jnp.minimum(e + 1, E - 1), e)

    x_spec = pl.BlockSpec((1, m, d), lambda e, s: (up_expert(e, s), 0, 0))
    g_spec = pl.BlockSpec((1, m, 1), lambda e, s: (e, 0, 0))
    o_spec = pl.BlockSpec((1, m, tn), lambda e, s: (e, 0, jnp.maximum(s - n_up, 0)))
    up_spec = pl.BlockSpec((1, 1, d, tf), lambda e, s: (layer, up_expert(e, s), 0, jnp.where(s >= n_up, 0, s)))
    return pl.pallas_call(
        functools.partial(_expert_kernel, n_up=n_up),
        grid=(E, n_up + n_down),
        in_specs=[
            x_spec, x_spec, up_spec, up_spec,
            pl.BlockSpec((1, 1, ff, tn), lambda e, s: (layer, e, 0, jnp.maximum(s - n_up, 0))),
            g_spec, g_spec,
        ],
        out_specs=[o_spec, o_spec],
        out_shape=[jax.ShapeDtypeStruct((E, m, d), BF16), jax.ShapeDtypeStruct((E, m, d), BF16)],
        scratch_shapes=[pltpu.VMEM((2, m, ff), BF16)],
        compiler_params=_cparams(("parallel", "arbitrary")),
        name="expert",
    )(xg_a, xg_b, w_gate, w_up, w_down, gate_a, gate_b)


COMBINE_TILE_ELEMS = 1024 * 1024


def _combine_kernel(y_ref, tok_ref, x_ref, g2_ref, o_ref, oh_scr):
    j = pl.program_id(1)
    t = x_ref.shape[1]
    n_slots = tok_ref.shape[2]

    @pl.when(j == 0)
    def _():
        tok_i = lax.broadcasted_iota(jnp.int32, (t, n_slots), 0).astype(F32)
        oh_scr[...] = jnp.where(tok_i == tok_ref[0], 1.0, 0.0).astype(BF16)

    y = jnp.concatenate([y_ref[e, 0] for e in range(N_EXPERTS)], axis=0)
    moe = jnp.dot(oh_scr[...], y, preferred_element_type=F32)
    o_ref[0] = x_ref[0] + g2_ref[0] * moe


def _combine_call(y4d, tok, x3d, gate):
    batch, t, d = x3d.shape
    cap = y4d.shape[2]
    tn = min(d, COMBINE_TILE_ELEMS // t)
    nb = gate.shape[0]
    gate_idx = (lambda b, j: (b, 0, j)) if nb > 1 else (lambda b, j: (0, 0, j))
    return pl.pallas_call(
        _combine_kernel,
        grid=(batch, d // tn),
        in_specs=[
            pl.BlockSpec((N_EXPERTS, 1, cap, tn), lambda b, j: (0, b, 0, j)),
            pl.BlockSpec((1, 1, N_EXPERTS * cap), lambda b, j: (b, 0, 0)),
            pl.BlockSpec((1, t, tn), lambda b, j: (b, 0, j)),
            pl.BlockSpec((1, 1, tn), gate_idx),
        ],
        out_specs=pl.BlockSpec((1, t, tn), lambda b, j: (b, 0, j)),
        out_shape=jax.ShapeDtypeStruct((batch, t, d), F32),
        scratch_shapes=[pltpu.VMEM((t, N_EXPERTS * cap), BF16)],
        compiler_params=_cparams(("parallel", "arbitrary")),
        name="combine",
    )(y4d, tok, x3d, gate)


def _norm_kernel(x_ref, g_ref, o_ref):
    x = x_ref[...]
    o_ref[...] = x * lax.rsqrt(jnp.mean(x * x, axis=-1, keepdims=True) + EPS) * g_ref[...]


def _norm_call(x2d, g):
    n, d = x2d.shape
    tm = 512
    return pl.pallas_call(
        _norm_kernel,
        grid=(n // tm,),
        in_specs=[pl.BlockSpec((tm, d), lambda i: (i, 0)), pl.BlockSpec((1, d), lambda i: (0, 0))],
        out_specs=pl.BlockSpec((tm, d), lambda i: (i, 0)),
        out_shape=jax.ShapeDtypeStruct((n, d), F32),
        compiler_params=_cparams(("parallel",)),
        name="final_norm",
    )(x2d, g)


def _rope_tables(n_tokens):
    n_rows = n_tokens // GRID_W
    row = jnp.repeat(jnp.arange(n_rows), GRID_W).astype(F32)
    col = jnp.tile(jnp.arange(GRID_W), n_rows).astype(F32)
    axis_dim = HEAD_DIM // 2
    inv_freq = ROPE_BASE ** (-jnp.arange(0, axis_dim, 2, dtype=F32) / axis_dim)
    ang_r, ang_c = row[:, None] * inv_freq[None, :], col[:, None] * inv_freq[None, :]
    cos = jnp.concatenate([jnp.cos(ang_r), jnp.cos(ang_r), jnp.cos(ang_c), jnp.cos(ang_c)], axis=1)
    sin = jnp.concatenate([-jnp.sin(ang_r), jnp.sin(ang_r), -jnp.sin(ang_c), jnp.sin(ang_c)], axis=1)
    return cos, sin


def _mix_and_route(x3d, layer, mods, w, ssm_mats, rope, ctx, kv_all=None):
    batch, seq, d = x3d.shape
    n = batch * seq
    sh1, sc1, g1, sh2, sc2, _ = mods
    x2d = x3d.reshape(n, d)
    latent = ctx is not None
    z, u = _inproj_call(x2d, w['norm1_g'][layer][None], sh1, sc1, w['w_in'], layer, rope[0], rope[1], latent, seq)
    if latent:
        attn = _attn_lat_call(z, w['attn_sink'][layer], ctx['k'], ctx['v'], layer, batch, seq)
        k_new = v_new = None
        r0, h0, state_layer = ctx['ret'], ctx['ssm'], layer
    else:
        attn, k_new, v_new = _attn_ctx_call(z, w['attn_sink'][layer], kv_all[0], kv_all[1], layer, batch, seq)
        r0, h0, state_layer = None, jnp.zeros((1, SSM_NQ, 2, batch, SSM_SW), F32), 0
    ret, r_fin = _ret_call(z, w['ret_log_gamma'][layer], w['ret_gn_g'][layer][None], r0, state_layer, batch, seq)
    zs, h_fin = _ssm_call(u, *ssm_mats, h0, layer, state_layer, batch, seq)
    x1 = _outproj_call(attn, ret, zs, w['ssm_glu_w'], w['ssm_glu_b'], w['w_out'], x2d, g1, layer)
    x1 = x1.reshape(batch, seq, d)
    xg, gate_slot, tok = _route_call(x1, w['norm2_g'][layer][None], sh2, sc2, w['router_wt'], layer)
    return x1, (xg, gate_slot, tok), (k_new, v_new, r_fin, h_fin)


def kernel(x_prompt, x_sample, cache_attn_k, cache_attn_v, state_ret, state_ssm_re, state_ssm_im, c, c_ctx, mod_w, mod_b, norm1_g, norm2_g, w_in, w_out, attn_sink, ret_log_gamma, ret_gn_g, ssm_lam_re, ssm_lam_im, ssm_log_step, ssm_b_re, ssm_b_im, ssm_c_re, ssm_c_im, ssm_d, ssm_glu_w, ssm_glu_b, router_w, moe_w_gate, moe_w_up, moe_w_down, final_norm_g):
    depth = w_in.shape[0]
    batch, seq, d = x_prompt.shape
    dec_batch, dec_seq, _ = x_sample.shape
    past = cache_attn_k.shape[2]
    G, P = N_SSM_GROUPS, SSM_STATE
    assert dec_seq % PROJ_ROWS == 0 and (batch * seq) % PROJ_ROWS == 0

    cv = jnp.concatenate([c_ctx[None, :], c, jnp.zeros((MOD_ROWS - 1 - dec_batch, d), F32)], axis=0)
    mod = _mod_call(cv, mod_w, mod_b)
    ssm_mats = _ssm_tile_matrices(*_ssm_group_matrices(
        ssm_lam_re, ssm_lam_im, ssm_log_step, ssm_b_re, ssm_b_im, ssm_c_re, ssm_c_im, ssm_d))
    rope = _rope_tables(dec_seq)
    no_rope = (jnp.ones((PROJ_ROWS, LANES), F32), jnp.zeros((PROJ_ROWS, LANES), F32))
    w = {
        'norm1_g': norm1_g, 'norm2_g': norm2_g, 'w_in': w_in.astype(BF16), 'w_out': w_out.astype(BF16),
        'attn_sink': attn_sink,
        'ret_log_gamma': ret_log_gamma, 'ret_gn_g': ret_gn_g, 'ssm_glu_w': ssm_glu_w,
        'ssm_glu_b': ssm_glu_b.reshape(depth, 1, SSM_WIDTH), 'router_wt': router_w.transpose(0, 2, 1),
    }
    h0_lat = jnp.concatenate([state_ssm_re.reshape(dec_batch, depth, 2, SSM_NQ, SSM_SW // 2),
                              state_ssm_im.reshape(dec_batch, depth, 2, SSM_NQ, SSM_SW // 2)], axis=-1)
    h0_lat = h0_lat.transpose(1, 3, 2, 0, 4)
    ctx = {
        'k': cache_attn_k.reshape(dec_batch, depth, past, KV_W),
        'v': cache_attn_v.reshape(dec_batch, depth, past, KV_W),
        'ret': state_ret, 'ssm': h0_lat,
    }
    xp, xs = x_prompt, x_sample
    k_all = jnp.zeros((batch, depth, seq, N_KV_HEADS, HEAD_DIM), F32)
    v_all = jnp.zeros((batch, depth, seq, N_KV_HEADS, HEAD_DIM), F32)
    rets, ssms = [], []
    for layer in range(depth):
        m = mod[layer]
        mods_p = [m[0:1, i * d:(i + 1) * d][:, None, :] for i in range(6)]
        mods_s = [m[1:1 + dec_batch, i * d:(i + 1) * d][:, None, :] for i in range(6)]
        xp1, (xg_p, gs_p, tok_p), (k_all, v_all, r_l, h_l) = _mix_and_route(
            xp, layer, mods_p, w, ssm_mats, no_rope, None, (k_all, v_all))
        xs1, (xg_s, gs_s, tok_s), _ = _mix_and_route(xs, layer, mods_s, w, ssm_mats, rope, ctx)
        cap_p, cap_s = xg_p.shape[2], xg_s.shape[2]
        y_p, y_s = _expert_call(
            xg_p.reshape(N_EXPERTS, batch * cap_p, d), gs_p,
            xg_s.reshape(N_EXPERTS, dec_batch * cap_s, d), gs_s,
            moe_w_gate, moe_w_up, moe_w_down, layer)
        xp = _combine_call(y_p.reshape(N_EXPERTS, batch, cap_p, d), tok_p, xp1, mods_p[5])
        xs = _combine_call(y_s.reshape(N_EXPERTS, dec_batch, cap_s, d), tok_s, xs1, mods_s[5])
        rets.append(r_l)
        ssms.append(h_l)
    y_prompt = _norm_call(xp.reshape(batch * seq, d), final_norm_g[None]).reshape(batch, seq, d)
    y_sample = _norm_call(xs.reshape(dec_batch * dec_seq, d), final_norm_g[None]).reshape(dec_batch, dec_seq, d)
    h_all = jnp.stack(ssms, axis=0).reshape(depth, SSM_NQ, 2, batch, 2, SSM_GL, P)
    h_all = h_all.transpose(4, 3, 0, 2, 1, 5, 6).reshape(2, batch, depth, 2, G, P)
    return (y_prompt, y_sample, k_all, v_all, jnp.stack(rets, axis=1),
            h_all[0], h_all[1])
```

```python
import functools
import math

import jax
import jax.numpy as jnp
from jax import lax
from jax.experimental import pallas as pl
from jax.experimental.pallas import tpu as pltpu

F32 = jnp.float32
BF16 = jnp.bfloat16
HIGHEST = lax.Precision.HIGHEST

D_MODEL = 2048
GRID_W = 64
EPS = 1e-6
HEAD_DIM = 128
ATTN_W = D_MODEL // 2
N_HEADS = ATTN_W // HEAD_DIM
N_KV_HEADS = N_HEADS // 4
Q_PER_KV = N_HEADS // N_KV_HEADS
KV_W = N_KV_HEADS * HEAD_DIM
ATTN_BLOCK = 128
RET_W = D_MODEL // 4
RET_DK = 128
N_RET_HEADS = RET_W // RET_DK
RET_CHUNK = 128
SSM_WIDTH = D_MODEL // 4
SSM_GROUP = 16
N_SSM_GROUPS = SSM_WIDTH // SSM_GROUP
SSM_STATE = 64
N_EXPERTS = 16
EXPERT_FF = D_MODEL // 2
EC_CAPACITY = 2
ROPE_BASE = 10000.0
IN_W = ATTN_W + 2 * KV_W + 4 * RET_W + SSM_WIDTH

LANES = 128
HEAD_Q0 = 0
HEAD_K0 = ATTN_W // LANES
HEAD_V0 = HEAD_K0 + KV_W // LANES
HEAD_RQ0 = HEAD_V0 + KV_W // LANES
HEAD_RK0 = HEAD_RQ0 + RET_W // LANES
HEAD_RV0 = HEAD_RK0 + RET_W // LANES
HEAD_RG0 = HEAD_RV0 + RET_W // LANES
HEAD_U0 = HEAD_RG0 + RET_W // LANES

SSM_CHUNK = 8
SSM_GL = LANES // SSM_GROUP
SSM_NQ = SSM_WIDTH // LANES
SSM_FLAT = SSM_CHUNK * LANES
SSM_SW = SSM_GL * 2 * SSM_STATE

MOD_ROWS = 16
PROJ_ROWS = 512
PROJ_COLS = 512
VMEM_LIMIT = 56 * 1024 * 1024


def _cparams(sem):
    return pltpu.CompilerParams(dimension_semantics=sem, vmem_limit_bytes=VMEM_LIMIT)


def _silu(x):
    return x * jax.nn.sigmoid(x)


def _nt_dot(a, b):
    return lax.dot_general(a, b, (((1,), (1,)), ((), ())), preferred_element_type=F32)


def _tn_dot(a, b):
    return lax.dot_general(a, b, (((0,), (0,)), ((), ())), preferred_element_type=F32)


def _mod_kernel(cv_ref, w_ref, b_ref, o_ref):
    a = _silu(cv_ref[...])
    o_ref[0] = jnp.dot(a, w_ref[0], preferred_element_type=F32, precision=HIGHEST) + b_ref[0]


def _mod_call(cv, mod_w, mod_b):
    depth, d, e = mod_w.shape
    tn = 1024
    return pl.pallas_call(
        _mod_kernel,
        grid=(depth, e // tn),
        in_specs=[
            pl.BlockSpec((MOD_ROWS, d), lambda l, j: (0, 0)),
            pl.BlockSpec((1, d, tn), lambda l, j: (l, 0, j)),
            pl.BlockSpec((1, 1, tn), lambda l, j: (l, 0, j)),
        ],
        out_specs=pl.BlockSpec((1, MOD_ROWS, tn), lambda l, j: (l, 0, j)),
        out_shape=jax.ShapeDtypeStruct((depth, MOD_ROWS, e), F32),
        compiler_params=_cparams(("parallel", "parallel")),
        name="mod",
    )(cv, mod_w, mod_b.reshape(depth, 1, e))


def _rope_head(z, cos, sin_signed):
    lane = lax.broadcasted_iota(jnp.int32, z.shape, 1)
    partner = jnp.where((lane % 64) < 32, pltpu.roll(z, LANES - 32, 1), pltpu.roll(z, 32, 1))
    return z * cos + partner * sin_signed


def _inproj_kernel(x_ref, g_ref, sh_ref, sc_ref, w_ref, cos_ref, sin_ref, z_ref, u_ref, *, use_rope):
    x = x_ref[...]
    ms = jnp.mean(x * x, axis=-1, keepdims=True)
    hn = x * lax.rsqrt(ms + EPS) * g_ref[...]
    h = (hn * (1.0 + sc_ref[0]) + sh_ref[0]).astype(BF16)
    tn = PROJ_COLS
    heads_per_tile = tn // LANES
    for jt in range(IN_W // tn):
        acc = jnp.dot(h, w_ref[0, :, jt * tn:(jt + 1) * tn], preferred_element_type=F32)
        head0 = jt * heads_per_tile
        if head0 >= HEAD_U0:
            u_ref[:, (head0 - HEAD_U0) * LANES:(head0 - HEAD_U0) * LANES + tn] = acc
            continue
        for k in range(heads_per_tile):
            head = head0 + k
            zk = acc[:, k * LANES:(k + 1) * LANES]
            if HEAD_RK0 <= head < HEAD_RV0:
                zk = zk * RET_DK ** -0.5
            if use_rope and (head < HEAD_V0 or HEAD_RQ0 <= head < HEAD_RV0):
                zk = _rope_head(zk, cos_ref[...], sin_ref[...])
            z_ref[:, head * LANES:(head + 1) * LANES] = zk.astype(z_ref.dtype)


def _inproj_call(x2d, g, shift, scale, w_in_bf16, layer, cos, sin_signed, use_rope, seq):
    n, d = x2d.shape
    tm = PROJ_ROWS
    z_w = IN_W - SSM_WIDTH
    nb = shift.shape[0]
    rows_per_mod = n // nb
    tiles_per_seq = seq // tm
    mod_idx = lambda i: (i * tm // rows_per_mod, 0, 0)
    pos_idx = (lambda i: (i % tiles_per_seq, 0)) if use_rope else (lambda i: (0, 0))
    return pl.pallas_call(
        functools.partial(_inproj_kernel, use_rope=use_rope),
        grid=(n // tm,),
        in_specs=[
            pl.BlockSpec((tm, d), lambda i: (i, 0)),
            pl.BlockSpec((1, d), lambda i: (0, 0)),
            pl.BlockSpec((1, 1, d), mod_idx),
            pl.BlockSpec((1, 1, d), mod_idx),
            pl.BlockSpec((1, d, IN_W), lambda i: (layer, 0, 0), pipeline_mode=pl.Buffered(1)),
            pl.BlockSpec((tm, LANES), pos_idx),
            pl.BlockSpec((tm, LANES), pos_idx),
        ],
        out_specs=[
            pl.BlockSpec((tm, z_w), lambda i: (i, 0)),
            pl.BlockSpec((tm, SSM_WIDTH), lambda i: (i, 0)),
        ],
        out_shape=[jax.ShapeDtypeStruct((n, z_w), BF16), jax.ShapeDtypeStruct((n, SSM_WIDTH), F32)],
        compiler_params=_cparams(("parallel",)),
        name="inproj",
    )(x2d, g, shift, scale, w_in_bf16, cos, sin_signed)


def _sink_attend(s, sink_col, v):
    m = jnp.maximum(jnp.max(s, axis=-1, keepdims=True), sink_col)
    e = jnp.exp(s - m).astype(BF16)
    v_aug = jnp.concatenate([v, jnp.ones(v.shape, v.dtype)], axis=1)
    oa = jnp.dot(e, v_aug, preferred_element_type=F32)
    den = oa[:, HEAD_DIM:HEAD_DIM + 1] + jnp.exp(sink_col - m)
    return oa[:, :HEAD_DIM] * (1.0 / den)


def _stack_q(q_ref, kh):
    return jnp.concatenate(
        [q_ref[:, (kh * Q_PER_KV + g) * HEAD_DIM:(kh * Q_PER_KV + g + 1) * HEAD_DIM] for g in range(Q_PER_KV)], axis=0)


def _sink_col(sink_ref, kh, rows):
    return jnp.concatenate(
        [jnp.full((rows, 1), sink_ref[kh * Q_PER_KV + g], F32) for g in range(Q_PER_KV)], axis=0)


def _attn_ctx_kernel(sink_ref, q_ref, k_ref, v_ref, k_all_ref, v_all_ref, o_ref, kn_ref, vn_ref):
    del k_all_ref, v_all_ref
    rows = q_ref.shape[0]
    scale = HEAD_DIM ** -0.5
    for kh in range(N_KV_HEADS):
        hs = slice(kh * HEAD_DIM, (kh + 1) * HEAD_DIM)
        s = _nt_dot(_stack_q(q_ref, kh), k_ref[:, hs]) * scale
        o = _sink_attend(s, _sink_col(sink_ref, kh, rows), v_ref[:, hs])
        for g in range(Q_PER_KV):
            c0 = (kh * Q_PER_KV + g) * HEAD_DIM
            o_ref[:, c0:c0 + HEAD_DIM] = o[g * rows:(g + 1) * rows].astype(o_ref.dtype)
    for kh in range(N_KV_HEADS):
        hs = slice(kh * HEAD_DIM, (kh + 1) * HEAD_DIM)
        kn_ref[0, 0, pl.ds(kh, rows, stride=N_KV_HEADS), :] = k_ref[:, hs].astype(F32)
        vn_ref[0, 0, pl.ds(kh, rows, stride=N_KV_HEADS), :] = v_ref[:, hs].astype(F32)


def _attn_ctx_call(z, sink, k_all, v_all, layer, batch, seq):
    n = z.shape[0]
    kcol, vcol = HEAD_K0 * LANES // KV_W, HEAD_V0 * LANES // KV_W
    kv_out = pl.BlockSpec((1, 1, seq * N_KV_HEADS, HEAD_DIM), lambda b: (b, layer, 0, 0))
    return pl.pallas_call(
        _attn_ctx_kernel,
        grid=(batch,),
        in_specs=[
            pl.BlockSpec(memory_space=pltpu.SMEM),
            pl.BlockSpec((seq, ATTN_W), lambda b: (b, 0)),
            pl.BlockSpec((seq, KV_W), lambda b: (b, kcol)),
            pl.BlockSpec((seq, KV_W), lambda b: (b, vcol)),
            pl.BlockSpec(memory_space=pl.ANY),
            pl.BlockSpec(memory_space=pl.ANY),
        ],
        out_specs=[pl.BlockSpec((seq, ATTN_W), lambda b: (b, 0)), kv_out, kv_out],
        out_shape=[
            jax.ShapeDtypeStruct((n, ATTN_W), BF16),
            jax.ShapeDtypeStruct(k_all.shape, F32),
            jax.ShapeDtypeStruct(v_all.shape, F32),
        ],
        input_output_aliases={4: 1, 5: 2},
        compiler_params=_cparams(("parallel",)),
        name="attn_ctx",
    )(sink, z, z, z, k_all, v_all)


def _attn_lat_kernel(sink_ref, q_ref, kp_ref, kc_ref, kn_ref, vp_ref, vc_ref, vn_ref, kx_ref, vx_ref, o_ref, *, nblk):
    i = pl.program_id(1)
    rows = q_ref.shape[0]
    scale = HEAD_DIM ** -0.5
    r = lax.broadcasted_iota(jnp.int32, (rows, ATTN_BLOCK), 0)
    c = lax.broadcasted_iota(jnp.int32, (rows, ATTN_BLOCK), 1)
    neg = jnp.float32(-jnp.inf)
    bias_prev = jnp.where(c >= r, jnp.where(i > 0, 0.0, neg), neg)
    bias_next = jnp.where(c <= r, jnp.where(i < nblk - 1, 0.0, neg), neg)
    n_ctx = kx_ref.shape[2]
    bias = jnp.concatenate([bias_prev, jnp.zeros((rows, ATTN_BLOCK), F32), bias_next,
                            jnp.zeros((rows, n_ctx), F32)], axis=1)
    bias = jnp.concatenate([bias] * Q_PER_KV, axis=0)
    for kh in range(N_KV_HEADS):
        hs = slice(kh * HEAD_DIM, (kh + 1) * HEAD_DIM)
        kcat = jnp.concatenate([kp_ref[:, hs], kc_ref[:, hs], kn_ref[:, hs], kx_ref[0, 0, :, hs].astype(BF16)], axis=0)
        vcat = jnp.concatenate([vp_ref[:, hs], vc_ref[:, hs], vn_ref[:, hs], vx_ref[0, 0, :, hs].astype(BF16)], axis=0)
        s = _nt_dot(_stack_q(q_ref, kh), kcat) * scale + bias
        o = _sink_attend(s, _sink_col(sink_ref, kh, rows), vcat)
        for g in range(Q_PER_KV):
            c0 = (kh * Q_PER_KV + g) * HEAD_DIM
            o_ref[:, c0:c0 + HEAD_DIM] = o[g * rows:(g + 1) * rows].astype(o_ref.dtype)


def _attn_lat_call(z, sink, cache_k, cache_v, layer, batch, seq):
    n = z.shape[0]
    nblk = seq // ATTN_BLOCK
    kcol, vcol = HEAD_K0 * LANES // KV_W, HEAD_V0 * LANES // KV_W
    past = cache_k.shape[2]

    def kv_spec(col, off):
        return pl.BlockSpec((ATTN_BLOCK, KV_W),
                            lambda b, i: (b * nblk + jnp.clip(i + off, 0, nblk - 1), col))

    cache_spec = pl.BlockSpec((1, 1, past, KV_W), lambda b, i: (b, layer, 0, 0))
    return pl.pallas_call(
        functools.partial(_attn_lat_kernel, nblk=nblk),
        grid=(batch, nblk),
        in_specs=[
            pl.BlockSpec(memory_space=pltpu.SMEM),
            pl.BlockSpec((ATTN_BLOCK, ATTN_W), lambda b, i: (b * nblk + i, 0)),
            kv_spec(kcol, -1), kv_spec(kcol, 0), kv_spec(kcol, 1),
            kv_spec(vcol, -1), kv_spec(vcol, 0), kv_spec(vcol, 1),
            cache_spec, cache_spec,
        ],
        out_specs=pl.BlockSpec((ATTN_BLOCK, ATTN_W), lambda b, i: (b * nblk + i, 0)),
        out_shape=jax.ShapeDtypeStruct((n, ATTN_W), BF16),
        compiler_params=_cparams(("parallel", "parallel")),
        name="attn_lat",
    )(sink, z, z, z, z, z, z, z, cache_k, cache_v)


def _ret_kernel(lg_ref, q_ref, k_ref, v_ref, rg_ref, gn_ref, *rest, n_chunks, has_r0, has_r_all):
    r0_ref = rest[0] if has_r0 else None
    o_ref, rfin_ref, of_scr, ob_scr, r_scr, dec_scr, qd_scr, kd_scr, cd_scr = rest[has_r0 + has_r_all:]
    C, dk = RET_CHUNK, RET_DK
    ii = lax.broadcasted_iota(jnp.int32, (C, C), 0).astype(F32)
    jj = lax.broadcasted_iota(jnp.int32, (C, C), 1).astype(F32)
    pos = lax.broadcasted_iota(jnp.int32, (C, dk), 0).astype(F32)
    for h in range(N_RET_HEADS):
        for d in range(2):
            lg = lg_ref[d, h]
            diff = (ii - jj) if d == 0 else (jj - ii)
            dec_scr[d, h] = jnp.where(diff >= 0, jnp.exp(jnp.maximum(diff, 0.0) * lg), 0.0)
            if d == 0:
                qd_scr[d, h] = jnp.exp((pos + 1.0) * lg)
                kd_scr[d, h] = jnp.exp((C - 1.0 - pos) * lg)
            else:
                qd_scr[d, h] = jnp.exp((C - pos) * lg)
                kd_scr[d, h] = jnp.exp(pos * lg)
            cd_scr[d, h] = jnp.exp(jnp.full((8, dk), float(C), F32) * lg)
            r_scr[d, h] = r0_ref[0, 0, d, h] if has_r0 else jnp.zeros((dk, dk), F32)

    def body(t, carry):
        for h in range(N_RET_HEADS):
            hs = slice(h * dk, (h + 1) * dk)
            for d in range(2):
                n = t if d == 0 else n_chunks - 1 - t
                rows = pl.ds(pl.multiple_of(n * C, C), C)
                q, k, v = q_ref[rows, hs], k_ref[rows, hs], v_ref[rows, hs]
                r = r_scr[d, h]
                s = _nt_dot(q, k) * dec_scr[d, h]
                intra = jnp.dot(s.astype(BF16), v, preferred_element_type=F32)
                cross = jnp.dot((q.astype(F32) * qd_scr[d, h]).astype(BF16), r.astype(BF16),
                                preferred_element_type=F32)
                kv = _tn_dot((k.astype(F32) * kd_scr[d, h]).astype(BF16), v)
                (of_scr if d == 0 else ob_scr)[rows, hs] = intra + cross
                r_scr[d, h] = cd_scr[d, h, 0:1, :] * r + kv
        return carry

    lax.fori_loop(0, n_chunks, body, 0)
    rfin_ref[0, 0] = r_scr[...]
    for h in range(N_RET_HEADS):
        hs = slice(h * dk, (h + 1) * dk)
        o = of_scr[:, hs] + ob_scr[:, hs]
        mu = jnp.mean(o, axis=-1, keepdims=True)
        var = jnp.mean(jnp.square(o - mu), axis=-1, keepdims=True)
        on = (o - mu) * lax.rsqrt(var + EPS) * gn_ref[:, hs]
        o_ref[:, hs] = (_silu(rg_ref[:, hs].astype(F32)) * on).astype(o_ref.dtype)


def _ret_call(z, log_gamma, gn_g, r0, state_layer, batch, seq, r_all=None, out_layer=0):
    n = z.shape[0]
    dk, H = RET_DK, N_RET_HEADS

    def col(head0):
        return pl.BlockSpec((seq, RET_W), lambda b: (b, head0 * LANES // RET_W))

    has_r0, has_r_all = r0 is not None, r_all is not None
    state_specs = [pl.BlockSpec((1, 1, 2, H, dk, dk), lambda b: (b, state_layer, 0, 0, 0, 0))] if has_r0 else []
    state_specs += [pl.BlockSpec(memory_space=pl.ANY)] if has_r_all else []
    state_args = ((r0,) if has_r0 else ()) + ((r_all,) if has_r_all else ())
    n_fixed = 6
    return pl.pallas_call(
        functools.partial(_ret_kernel, n_chunks=seq // RET_CHUNK, has_r0=has_r0, has_r_all=has_r_all),
        grid=(batch,),
        in_specs=[
            pl.BlockSpec(memory_space=pltpu.SMEM),
            col(HEAD_RQ0), col(HEAD_RK0), col(HEAD_RV0), col(HEAD_RG0),
            pl.BlockSpec((1, RET_W), lambda b: (0, 0)),
        ] + state_specs,
        out_specs=[
            pl.BlockSpec((seq, RET_W), lambda b: (b, 0)),
            pl.BlockSpec((1, 1, 2, H, dk, dk), lambda b: (b, out_layer, 0, 0, 0, 0)),
        ],
        out_shape=[
            jax.ShapeDtypeStruct((n, RET_W), BF16),
            jax.ShapeDtypeStruct(r_all.shape if has_r_all else (batch, 1, 2, H, dk, dk), F32),
        ],
        input_output_aliases={n_fixed + has_r0: 1} if has_r_all else {},
        scratch_shapes=[
            pltpu.VMEM((seq, RET_W), F32), pltpu.VMEM((seq, RET_W), F32),
            pltpu.VMEM((2, H, dk, dk), F32), pltpu.VMEM((2, H, RET_CHUNK, RET_CHUNK), F32),
            pltpu.VMEM((2, H, RET_CHUNK, dk), F32), pltpu.VMEM((2, H, RET_CHUNK, dk), F32),
            pltpu.VMEM((2, H, 8, dk), F32),
        ],
        compiler_params=_cparams(("parallel",)),
        name="ret",
    )(log_gamma, z, z, z, z, gn_g, *state_args)


def _ssm_group_matrices(lam_re, lam_im, log_step, b_re, b_im, c_re, c_im, d_skip):
    depth, _, G, P = lam_re.shape
    H, T = SSM_GROUP, SSM_CHUNK
    dt = jnp.exp(log_step)[..., None]
    ar, ai = lam_re * dt, lam_im * dt
    mag = jnp.exp(ar)
    lbr, lbi = mag * jnp.cos(ai), mag * jnp.sin(ai)
    den = lam_re * lam_re + lam_im * lam_im
    fr = ((lbr - 1.0) * lam_re + lbi * lam_im) / den
    fi = (lbi * lam_re - (lbr - 1.0) * lam_im) / den
    bbr = fr[..., None] * b_re - fi[..., None] * b_im
    bbi = fr[..., None] * b_im + fi[..., None] * b_re
    m = jnp.arange(T + 1, dtype=F32)[:, None]
    pmag = jnp.exp(ar[..., None, :] * m)
    pwr, pwi = pmag * jnp.cos(ai[..., None, :] * m), pmag * jnp.sin(ai[..., None, :] * m)
    xr = pwr[..., None] * bbr[..., None, :, :] - pwi[..., None] * bbi[..., None, :, :]
    xi = pwr[..., None] * bbi[..., None, :, :] + pwi[..., None] * bbr[..., None, :, :]
    kern = (jnp.einsum('ldghp,ldgmpi->ldgmhi', c_re, xr[..., :T, :, :], precision=HIGHEST)
            - jnp.einsum('ldghp,ldgmpi->ldgmhi', c_im, xi[..., :T, :, :], precision=HIGHEST))
    tok = jnp.arange(T)
    lag = tok[None, :] - tok[:, None]
    kf = kern[:, 0][:, :, jnp.clip(lag, 0, T - 1)] * (lag >= 0)[None, None, :, :, None, None].astype(F32)
    kb = kern[:, 1][:, :, jnp.clip(-lag, 0, T - 1)] * (lag <= 0)[None, None, :, :, None, None].astype(F32)
    tm = (kf + kb).transpose(0, 1, 2, 5, 3, 4)
    skip = (jnp.eye(T, dtype=F32)[None, None, :, None, :, None]
            * jnp.eye(H, dtype=F32)[None, None, None, :, None, :]
            * d_skip.reshape(depth, G, 1, H, 1, 1))
    tmat = (tm + skip).reshape(depth, G, T * H, T * H)
    rev = T - 1 - tok
    ef_r, ef_i = xr[:, 0][:, :, rev], xi[:, 0][:, :, rev]
    eb_r, eb_i = xr[:, 1][:, :, tok], xi[:, 1][:, :, tok]
    bmat = jnp.concatenate([e.transpose(0, 1, 2, 4, 3) for e in (ef_r, ef_i, eb_r, eb_i)], axis=-1)
    bmat = bmat.reshape(depth, G, T * H, 4 * P)

    def entry(d, powers):
        pr = pwr[:, d][:, :, powers][:, :, :, None, :]
        pi = pwi[:, d][:, :, powers][:, :, :, None, :]
        cr, ci = c_re[:, d][:, :, None], c_im[:, d][:, :, None]
        wr, wi = cr * pr - ci * pi, cr * pi + ci * pr
        return wr.transpose(0, 1, 4, 2, 3), -wi.transpose(0, 1, 4, 2, 3)

    cf_r, cf_i = entry(0, tok + 1)
    cb_r, cb_i = entry(1, T - tok)
    cmat = jnp.concatenate([cf_r, cf_i, cb_r, cb_i], axis=2).reshape(depth, G, 4 * P, T * H)
    lam_t = jnp.stack([pwr[..., T, :], pwi[..., T, :]], axis=2)
    return tmat, bmat, cmat, lam_t


def _ssm_tile_matrices(tmat, bmat, cmat, lam_t):
    depth = tmat.shape[0]
    T, H, GL, NQ, P = SSM_CHUNK, SSM_GROUP, SSM_GL, SSM_NQ, SSM_STATE
    tm = tmat.reshape(depth, N_SSM_GROUPS, T, H, T, H)
    lag_neg = tm[:, :, 1:, :, 0, :][:, :, ::-1]
    lag_pos = tm[:, :, 0].transpose(0, 1, 3, 2, 4)
    lags = jnp.concatenate([lag_neg, lag_pos], axis=2)
    lags = lags.reshape(depth, NQ, GL, 2 * T - 1, H, H).transpose(0, 1, 3, 2, 4, 5)
    eye = jnp.eye(GL, dtype=F32)[None, None, None, :, None, :, None]
    dmat = (lags[..., None, :] * eye).reshape(depth, NQ, 2 * T - 1, LANES, LANES)
    emat = bmat.astype(BF16).reshape(depth, NQ, GL, T, H, 2, 2, P).transpose(0, 1, 5, 6, 3, 2, 4, 7)
    emat = emat.reshape(depth, NQ, 2, 2, SSM_FLAT, 1, P)
    emat = jnp.broadcast_to(emat, (depth, NQ, 2, 2, SSM_FLAT, 2, P)).reshape(depth, NQ, 2, 2, SSM_FLAT, 2 * P)
    wmat = cmat.astype(BF16).reshape(depth, NQ, GL, 2, 2, P, T, H).transpose(0, 1, 3, 4, 5, 6, 2, 7)
    wmat = wmat.reshape(depth, NQ, 2, 2, 1, P, SSM_FLAT)
    wmat = jnp.broadcast_to(wmat, (depth, NQ, 2, 2, 2, P, SSM_FLAT)).reshape(depth, NQ, 2, 2, 2 * P, SSM_FLAT)
    lamq = lam_t.reshape(depth, 2, 2, NQ, GL * P).transpose(0, 3, 1, 2, 4)
    return dmat.astype(BF16), emat, wmat, lamq


def _gelu_tanh(x):
    return 0.5 * x * (1.0 + jnp.tanh(math.sqrt(2.0 / math.pi) * (x + 0.044715 * (x * x * x))))


def _ssm_kernel(u_ref, d_ref, e_ref, w_ref, lam_ref, h0_ref, y_ref, hfin_ref,
                xf_scr, hb_scr, st_scr, y_scr, tq_scr, bq_scr, cq_scr, *, n_chunks, batch, seq):
    d = pl.program_id(1)
    T, GL = SSM_CHUNK, SSM_GL

    @pl.when(d == 0)
    def _():
        for s in range(batch):
            for tau in range(T):
                xf_scr[tau, pl.ds(s, n_chunks, stride=batch), :] = u_ref[pl.ds(s * seq + tau, n_chunks, stride=T), :]
        for j in range(T):
            for i in range(T):
                tq_scr[j * LANES:(j + 1) * LANES, i * LANES:(i + 1) * LANES] = d_ref[0, 0, i - j + T - 1].astype(BF16)

    ch_bits = SSM_GROUP.bit_length() - 1
    row_group = (lax.broadcasted_iota(jnp.int32, (SSM_FLAT, LANES), 0) >> ch_bits) & (GL - 1)
    col_group = (lax.broadcasted_iota(jnp.int32, (LANES, SSM_FLAT), 1) >> ch_bits) & (GL - 1)
    st_bits = SSM_STATE.bit_length() - 1
    lane_half = lax.broadcasted_iota(jnp.int32, (SSM_FLAT, LANES), 1) >> st_bits
    row_half = lax.broadcasted_iota(jnp.int32, (LANES, SSM_FLAT), 0) >> st_bits
    tiles_per_part = GL * SSM_STATE // LANES
    for part in range(2):
        emat, wmat = e_ref[0, 0, 0, part], w_ref[0, 0, 0, part]
        for m in range(tiles_per_part):
            ts = slice((part * tiles_per_part + m) * LANES, (part * tiles_per_part + m + 1) * LANES)
            bq_scr[:, ts] = jnp.where(row_group == 2 * m + lane_half, emat, 0.0).astype(BF16)
            cq_scr[ts, :] = jnp.where(col_group == 2 * m + row_half, wmat, 0.0).astype(BF16)

    xf = jnp.concatenate([xf_scr[tau] for tau in range(T)], axis=1).astype(BF16)
    hb_scr[...] = jnp.dot(xf, bq_scr[...], preferred_element_type=F32)
    HW = SSM_SW // 2
    lam_re, lam_im = lam_ref[0, 0, 0, 0:1, :], lam_ref[0, 0, 0, 1:2, :]

    def body(t, s):
        s_re, s_im = s
        c = jnp.where(d == 0, t, n_chunks - 1 - t)
        rows = pl.ds(pl.multiple_of(c * batch, batch), batch)
        st_scr[rows, 0:HW] = s_re
        st_scr[rows, HW:SSM_SW] = s_im
        return (lam_re * s_re - lam_im * s_im + hb_scr[rows, 0:HW],
                lam_re * s_im + lam_im * s_re + hb_scr[rows, HW:SSM_SW])

    h0 = h0_ref[0, 0, 0]
    f_re, f_im = lax.fori_loop(0, n_chunks, body, (h0[:, 0:HW], h0[:, HW:SSM_SW]))
    hfin_ref[0, 0] = jnp.concatenate([f_re, f_im], axis=1)
    part = jnp.dot(st_scr[...].astype(BF16), cq_scr[...], preferred_element_type=F32)

    @pl.when(d == 0)
    def _():
        y_scr[...] = part + jnp.dot(xf, tq_scr[...], preferred_element_type=F32)

    @pl.when(d == 1)
    def _():
        y = _gelu_tanh(y_scr[...] + part)
        for tau in range(T):
            xf_scr[tau] = y[:, tau * LANES:(tau + 1) * LANES]
        for s in range(batch):
            for tau in range(T):
                y_ref[pl.ds(s * seq + tau, n_chunks, stride=T), :] = xf_scr[tau, pl.ds(s, n_chunks, stride=batch), :]


def _ssm_call(u, dmat, emat, wmat, lamq, h0, layer, state_layer, batch, seq):
    n = batch * seq
    n_chunks = seq // SSM_CHUNK
    m = batch * n_chunks
    n_lags = 2 * SSM_CHUNK - 1
    return pl.pallas_call(
        functools.partial(_ssm_kernel, n_chunks=n_chunks, batch=batch, seq=seq),
        grid=(SSM_NQ, 2),
        in_specs=[
            pl.BlockSpec((n, LANES), lambda q, d: (0, q), pipeline_mode=pl.Buffered(1)),
            pl.BlockSpec((1, 1, n_lags, LANES, LANES), lambda q, d: (layer, q, 0, 0, 0)),
            pl.BlockSpec((1, 1, 1, 2, SSM_FLAT, LANES), lambda q, d: (layer, q, d, 0, 0, 0)),
            pl.BlockSpec((1, 1, 1, 2, LANES, SSM_FLAT), lambda q, d: (layer, q, d, 0, 0, 0)),
            pl.BlockSpec((1, 1, 1, 2, SSM_SW // 2), lambda q, d: (layer, q, d, 0, 0)),
            pl.BlockSpec((1, 1, 1, batch, SSM_SW), lambda q, d: (state_layer, q, d, 0, 0)),
        ],
        out_specs=[
            pl.BlockSpec((n, LANES), lambda q, d: (0, q)),
            pl.BlockSpec((1, 1, batch, SSM_SW), lambda q, d: (q, d, 0, 0)),
        ],
        out_shape=[
            jax.ShapeDtypeStruct((n, SSM_WIDTH), F32),
            jax.ShapeDtypeStruct((SSM_NQ, 2, batch, SSM_SW), F32),
        ],
        scratch_shapes=[pltpu.VMEM((SSM_CHUNK, m, LANES), F32), pltpu.VMEM((m, SSM_SW), F32),
                        pltpu.VMEM((m, SSM_SW), F32), pltpu.VMEM((m, SSM_FLAT), F32),
                        pltpu.VMEM((SSM_FLAT, SSM_FLAT), BF16), pltpu.VMEM((SSM_FLAT, SSM_SW), BF16),
                        pltpu.VMEM((SSM_SW, SSM_FLAT), BF16)],
        compiler_params=_cparams(("arbitrary", "arbitrary")),
        name="ssm",
    )(u, dmat, emat, wmat, lamq, h0)


def _outproj_kernel(attn_ref, ret_ref, zs_ref, gw_ref, gb_ref, w_ref, x_ref, g1_ref, o_ref):
    zs = zs_ref[...]
    gl = jnp.dot(zs.astype(BF16), gw_ref[0].astype(BF16), preferred_element_type=F32) + gb_ref[0]
    mix = jnp.concatenate([attn_ref[...], ret_ref[...], (zs * jax.nn.sigmoid(gl)).astype(BF16)], axis=1)
    tn = PROJ_COLS
    for jt in range(o_ref.shape[1] // tn):
        cs = slice(jt * tn, (jt + 1) * tn)
        acc = jnp.dot(mix, w_ref[0, :, cs], preferred_element_type=F32)
        o_ref[:, cs] = x_ref[:, cs] + g1_ref[0, :, cs] * acc


def _outproj_call(attn, ret, zs, glu_w, glu_b, w_out_bf16, x2d, gate, layer):
    n, d = x2d.shape
    tm = PROJ_ROWS
    rows_per_gate = n // gate.shape[0]
    return pl.pallas_call(
        _outproj_kernel,
        grid=(n // tm,),
        in_specs=[
            pl.BlockSpec((tm, ATTN_W), lambda i: (i, 0)),
            pl.BlockSpec((tm, RET_W), lambda i: (i, 0)),
            pl.BlockSpec((tm, SSM_WIDTH), lambda i: (i, 0)),
            pl.BlockSpec((1, SSM_WIDTH, SSM_WIDTH), lambda i: (layer, 0, 0)),
            pl.BlockSpec((1, 1, SSM_WIDTH), lambda i: (layer, 0, 0)),
            pl.BlockSpec((1, d, d), lambda i: (layer, 0, 0), pipeline_mode=pl.Buffered(1)),
            pl.BlockSpec((tm, d), lambda i: (i, 0)),
            pl.BlockSpec((1, 1, d), lambda i: (i * tm // rows_per_gate, 0, 0)),
        ],
        out_specs=pl.BlockSpec((tm, d), lambda i: (i, 0)),
        out_shape=jax.ShapeDtypeStruct((n, d), F32),
        compiler_params=_cparams(("parallel",)),
        name="outproj",
    )(attn, ret, zs, glu_w, glu_b, w_out_bf16, x2d, gate)


GATHER_ROWS = 512
TOKEN_SPLIT_BITS = 5
TOKEN_SPLIT = 1 << TOKEN_SPLIT_BITS


def _route_kernel(x_ref, g_ref, sh_ref, sc_ref, rw_ref, xg_ref, gs_ref, tok_ref, h_scr, *, cap, per_batch_mod):
    bb, t = x_ref.shape[0], x_ref.shape[1]
    E = N_EXPERTS
    R = bb * E
    rc = 256
    affs = []
    for bi in range(bb):
        mi = bi if per_batch_mod else 0
        logits = []
        for c0 in range(0, t, rc):
            x = x_ref[bi, c0:c0 + rc, :]
            ms = jnp.mean(x * x, axis=-1, keepdims=True)
            h = x * lax.rsqrt(ms + EPS) * g_ref[...]
            h = h * (1.0 + sc_ref[mi]) + sh_ref[mi]
            h_scr[bi, c0:c0 + rc, :] = h.astype(BF16)
            logits.append(lax.dot_general(rw_ref[0], h, (((1,), (1,)), ((), ())),
                                          preferred_element_type=F32, precision=HIGHEST))
        lg = jnp.concatenate(logits, axis=1)
        ex = jnp.exp(lg - jnp.max(lg, axis=0, keepdims=True))
        affs.append(ex / jnp.sum(ex, axis=0, keepdims=True))
    aff = jnp.concatenate(affs, axis=0)

    def count_ge(v):
        return jnp.sum(jnp.where(aff >= v, 1.0, 0.0), axis=1, keepdims=True)

    def bisect_bits(_, lohi):
        lo, hi = lohi
        mid = lo + ((hi - lo + 1) >> 1)
        ok = count_ge(lax.bitcast_convert_type(mid, F32)) >= float(cap)
        return jnp.where(ok, mid, lo), jnp.where(ok, hi, mid - 1)

    lo_b, _ = lax.fori_loop(0, 31, bisect_bits,
                            (jnp.zeros((R, 1), jnp.int32), jnp.full((R, 1), 0x7F800000, jnp.int32)))

    def bisect_val(_, lohi):
        lo, hi = lohi
        mid = lo + (hi - lo) * 0.5
        ok = count_ge(mid) >= float(cap)
        return jnp.where(ok, mid, lo), jnp.where(ok, hi, mid)

    _, hi_v = lax.fori_loop(0, 8, bisect_val,
                            (lax.bitcast_convert_type(lo_b, F32), lax.bitcast_convert_type(lo_b + 1, F32)))
    thr = jnp.max(jnp.where(aff < hi_v, aff, 0.0), axis=1, keepdims=True)
    gt = aff > thr
    eq = aff == thr
    need = float(cap) - jnp.sum(jnp.where(gt, 1.0, 0.0), axis=1, keepdims=True)
    tri = (lax.broadcasted_iota(jnp.int32, (LANES, LANES), 0)
           < lax.broadcasted_iota(jnp.int32, (LANES, LANES), 1)).astype(BF16)

    def prefix_count(flags):
        out, before = [], jnp.zeros((R, 1), F32)
        for c0 in range(0, t, LANES):
            blk = flags[:, c0:c0 + LANES]
            out.append(jnp.dot(blk.astype(BF16), tri, preferred_element_type=F32) + before)
            before = before + jnp.sum(blk, axis=1, keepdims=True)
        return jnp.concatenate(out, axis=1)

    sel = gt | (eq & (prefix_count(jnp.where(eq, 1.0, 0.0)) < need))
    pos = prefix_count(jnp.where(sel, 1.0, 0.0))
    base = ((lax.broadcasted_iota(jnp.int32, (R, t), 0) & (E - 1)) * cap).astype(F32)
    gpos = jnp.where(sel, pos + base, -1.0)
    a_hi = aff.astype(BF16)
    a_mid = (aff - a_hi.astype(F32)).astype(BF16)
    a_lo = (aff - a_hi.astype(F32) - a_mid.astype(F32)).astype(BF16)
    tok_i = lax.broadcasted_iota(jnp.int32, (8, t), 1)
    row_i = lax.broadcasted_iota(jnp.int32, (8, t), 0)
    tok_parts = jnp.where(row_i == 0, tok_i >> TOKEN_SPLIT_BITS,
                          jnp.where(row_i == 1, tok_i & (TOKEN_SPLIT - 1), 0))
    tok_parts = tok_parts.astype(F32).astype(BF16)
    n_grp = max(1, GATHER_ROWS // cap)
    for bi in range(bb):
        r0 = bi * E
        aff_parts = jnp.concatenate([a_hi[r0:r0 + E], a_mid[r0:r0 + E], a_lo[r0:r0 + E],
                                     jnp.zeros((LANES - 3 * E, t), BF16)], axis=0)
        for e0 in range(0, E, n_grp):
            onehots = []
            for e in range(e0, e0 + n_grp):
                slot = (e * cap + lax.broadcasted_iota(jnp.int32, (cap, t), 0)).astype(F32)
                onehots.append(jnp.where(gpos[r0 + e:r0 + e + 1, :] == slot, 1.0, 0.0).astype(BF16))
            onehot = jnp.concatenate(onehots, axis=0)
            xg = jnp.dot(onehot, h_scr[bi], preferred_element_type=F32)
            gate = _nt_dot(onehot, aff_parts)
            for k in range(n_grp):
                e, rs = e0 + k, slice(k * cap, (k + 1) * cap)
                xg_ref[e, bi] = xg[rs].astype(BF16)
                gs_ref[e, bi * cap:(bi + 1) * cap, :] = (
                    gate[rs, e:e + 1] + gate[rs, E + e:E + e + 1] + gate[rs, 2 * E + e:2 * E + e + 1])
            tk = _nt_dot(tok_parts, onehot)
            tok_ref[bi, :, e0 * cap:(e0 + n_grp) * cap] = TOKEN_SPLIT * tk[0:1, :] + tk[1:2, :]


ROUTE_TOKENS = 1024


def _route_call(x3d, g, shift, scale, router_wt, layer):
    batch, t, d = x3d.shape
    cap = EC_CAPACITY * t // N_EXPERTS
    bb = max(1, ROUTE_TOKENS // t)
    per_batch_mod = shift.shape[0] > 1
    mod_spec = pl.BlockSpec((bb if per_batch_mod else 1, 1, d), (lambda b: (b, 0, 0)) if per_batch_mod
                            else (lambda b: (0, 0, 0)))
    return pl.pallas_call(
        functools.partial(_route_kernel, cap=cap, per_batch_mod=per_batch_mod),
        grid=(batch // bb,),
        in_specs=[
            pl.BlockSpec((bb, t, d), lambda b: (b, 0, 0)),
            pl.BlockSpec((1, d), lambda b: (0, 0)),
            mod_spec, mod_spec,
            pl.BlockSpec((1, N_EXPERTS, d), lambda b: (layer, 0, 0)),
        ],
        out_specs=[
            pl.BlockSpec((N_EXPERTS, bb, cap, d), lambda b: (0, b, 0, 0)),
            pl.BlockSpec((N_EXPERTS, bb * cap, 1), lambda b: (0, b, 0)),
            pl.BlockSpec((bb, 1, N_EXPERTS * cap), lambda b: (b, 0, 0)),
        ],
        out_shape=[
            jax.ShapeDtypeStruct((N_EXPERTS, batch, cap, d), BF16),
            jax.ShapeDtypeStruct((N_EXPERTS, batch * cap, 1), F32),
            jax.ShapeDtypeStruct((batch, 1, N_EXPERTS * cap), F32),
        ],
        scratch_shapes=[pltpu.VMEM((bb, t, d), BF16)],
        compiler_params=_cparams(("parallel",)),
        name="route",
    )(x3d, g, shift, scale, router_wt)


EXPERT_UP_TILE = 512
EXPERT_DOWN_TILE = 512


def _expert_kernel(xa_ref, xb_ref, wg_ref, wu_ref, wd_ref, ga_ref, gb_ref, oa_ref, ob_ref, h_scr, *, n_up):
    s = pl.program_id(1)
    tf = EXPERT_UP_TILE
    for k in range(n_up):
        @pl.when(s == k)
        def _(k=k):
            wg, wu = wg_ref[0, 0].astype(BF16), wu_ref[0, 0].astype(BF16)
            for i, x_ref in enumerate((xa_ref, xb_ref)):
                x = x_ref[0]
                a = jnp.dot(x, wg, preferred_element_type=F32)
                b = jnp.dot(x, wu, preferred_element_type=F32)
                h_scr[i, :, k * tf:(k + 1) * tf] = (_silu(a) * b).astype(BF16)

    @pl.when(s >= n_up)
    def _():
        wd = wd_ref[0, 0].astype(BF16)
        for i, (g_ref, o_ref) in enumerate(((ga_ref, oa_ref), (gb_ref, ob_ref))):
            y = jnp.dot(h_scr[i], wd, preferred_element_type=F32)
            o_ref[0] = (y * g_ref[0]).astype(o_ref.dtype)


def _expert_call(xg_a, gate_a, xg_b, gate_b, w_gate, w_up, w_down, layer):
    E, m, d = xg_a.shape
    assert xg_b.shape == xg_a.shape
    ff = w_gate.shape[-1]
    tf, tn = EXPERT_UP_TILE, EXPERT_DOWN_TILE
    n_up, n_down = ff // tf, d // tn
    def up_expert(e, s):
        return jnp.where(s >= n_up, jnp.minimum(e + 1, E - 1), e)

    x_spec = pl.BlockSpec((1, m, d), lambda e, s: (up_expert(e, s), 0, 0))
    g_spec = pl.BlockSpec((1, m, 1), lambda e, s: (e, 0, 0))
    o_spec = pl.BlockSpec((1, m, tn), lambda e, s: (e, 0, jnp.maximum(s - n_up, 0)))
    up_spec = pl.BlockSpec((1, 1, d, tf), lambda e, s: (layer, up_expert(e, s), 0, jnp.where(s >= n_up, 0, s)))
    return pl.pallas_call(
        functools.partial(_expert_kernel, n_up=n_up),
        grid=(E, n_up + n_down),
        in_specs=[
            x_spec, x_spec, up_spec, up_spec,
            pl.BlockSpec((1, 1, ff, tn), lambda e, s: (layer, e, 0, jnp.maximum(s - n_up, 0))),
            g_spec, g_spec,
        ],
        out_specs=[o_spec, o_spec],
        out_shape=[jax.ShapeDtypeStruct((E, m, d), BF16), jax.ShapeDtypeStruct((E, m, d), BF16)],
        scratch_shapes=[pltpu.VMEM((2, m, ff), BF16)],
        compiler_params=_cparams(("parallel", "arbitrary")),
        name="expert",
    )(xg_a, xg_b, w_gate, w_up, w_down, gate_a, gate_b)


COMBINE_TILE_ELEMS = 1024 * 1024


def _combine_kernel(y_ref, tok_ref, x_ref, g2_ref, *rest, final_norm):
    fg_ref = rest[0] if final_norm else None
    o_ref, oh_scr = rest[1:] if final_norm else rest
    j = pl.program_id(1)
    t = x_ref.shape[1]
    n_slots = tok_ref.shape[2]

    @pl.when(j == 0)
    def _():
        tok_i = lax.broadcasted_iota(jnp.int32, (t, n_slots), 0).astype(F32)
        oh_scr[...] = jnp.where(tok_i == tok_ref[0], 1.0, 0.0).astype(BF16)

    y = jnp.concatenate([y_ref[e, 0] for e in range(N_EXPERTS)], axis=0)
    moe = jnp.dot(oh_scr[...], y, preferred_element_type=F32)
    x2 = x_ref[0] + g2_ref[0] * moe
    if final_norm:
        x2 = x2 * lax.rsqrt(jnp.mean(x2 * x2, axis=-1, keepdims=True) + EPS) * fg_ref[...]
    o_ref[0] = x2


def _combine_tile(t, d):
    return min(d, COMBINE_TILE_ELEMS // t)


def _combine_call(y4d, tok, x3d, gate, final_g=None):
    batch, t, d = x3d.shape
    cap = y4d.shape[2]
    tn = _combine_tile(t, d)
    final_norm = final_g is not None
    assert tn == d or not final_norm
    nb = gate.shape[0]
    gate_idx = (lambda b, j: (b, 0, j)) if nb > 1 else (lambda b, j: (0, 0, j))
    norm_specs = [pl.BlockSpec((1, d), lambda b, j: (0, 0))] if final_norm else []
    norm_args = (final_g,) if final_norm else ()
    return pl.pallas_call(
        functools.partial(_combine_kernel, final_norm=final_norm),
        grid=(batch, d // tn),
        in_specs=[
            pl.BlockSpec((N_EXPERTS, 1, cap, tn), lambda b, j: (0, b, 0, j)),
            pl.BlockSpec((1, 1, N_EXPERTS * cap), lambda b, j: (b, 0, 0)),
            pl.BlockSpec((1, t, tn), lambda b, j: (b, 0, j)),
            pl.BlockSpec((1, 1, tn), gate_idx),
        ] + norm_specs,
        out_specs=pl.BlockSpec((1, t, tn), lambda b, j: (b, 0, j)),
        out_shape=jax.ShapeDtypeStruct((batch, t, d), F32),
        scratch_shapes=[pltpu.VMEM((t, N_EXPERTS * cap), BF16)],
        compiler_params=_cparams(("parallel", "arbitrary")),
        name="combine",
    )(y4d, tok, x3d, gate, *norm_args)


def _norm_kernel(x_ref, g_ref, o_ref):
    x = x_ref[...]
    o_ref[...] = x * lax.rsqrt(jnp.mean(x * x, axis=-1, keepdims=True) + EPS) * g_ref[...]


def _norm_call(x2d, g):
    n, d = x2d.shape
    tm = 512
    return pl.pallas_call(
        _norm_kernel,
        grid=(n // tm,),
        in_specs=[pl.BlockSpec((tm, d), lambda i: (i, 0)), pl.BlockSpec((1, d), lambda i: (0, 0))],
        out_specs=pl.BlockSpec((tm, d), lambda i: (i, 0)),
        out_shape=jax.ShapeDtypeStruct((n, d), F32),
        compiler_params=_cparams(("parallel",)),
        name="final_norm",
    )(x2d, g)


def _rope_tables(n_tokens):
    n_rows = n_tokens // GRID_W
    row = jnp.repeat(jnp.arange(n_rows), GRID_W).astype(F32)
    col = jnp.tile(jnp.arange(GRID_W), n_rows).astype(F32)
    axis_dim = HEAD_DIM // 2
    inv_freq = ROPE_BASE ** (-jnp.arange(0, axis_dim, 2, dtype=F32) / axis_dim)
    ang_r, ang_c = row[:, None] * inv_freq[None, :], col[:, None] * inv_freq[None, :]
    cos = jnp.concatenate([jnp.cos(ang_r), jnp.cos(ang_r), jnp.cos(ang_c), jnp.cos(ang_c)], axis=1)
    sin = jnp.concatenate([-jnp.sin(ang_r), jnp.sin(ang_r), -jnp.sin(ang_c), jnp.sin(ang_c)], axis=1)
    return cos, sin


def _mix_and_route(x3d, layer, mods, w, ssm_mats, rope, ctx, ctx_out=None):
    batch, seq, d = x3d.shape
    n = batch * seq
    sh1, sc1, g1, sh2, sc2, _ = mods
    x2d = x3d.reshape(n, d)
    latent = ctx is not None
    z, u = _inproj_call(x2d, w['norm1_g'][layer][None], sh1, sc1, w['w_in'], layer, rope[0], rope[1], latent, seq)
    if latent:
        attn = _attn_lat_call(z, w['attn_sink'][layer], ctx['k'], ctx['v'], layer, batch, seq)
        k_new = v_new = None
        r0, h0, state_layer = ctx['ret'], ctx['ssm'], layer
    else:
        attn, k_new, v_new = _attn_ctx_call(z, w['attn_sink'][layer], ctx_out[0], ctx_out[1], layer, batch, seq)
        r0, h0, state_layer = None, jnp.zeros((1, SSM_NQ, 2, batch, SSM_SW), F32), 0
    r_all = None if latent else ctx_out[2]
    ret, r_fin = _ret_call(z, w['ret_log_gamma'][layer], w['ret_gn_g'][layer][None], r0, state_layer, batch, seq,
                           r_all, 0 if latent else layer)
    zs, h_fin = _ssm_call(u, *ssm_mats, h0, layer, state_layer, batch, seq)
    x1 = _outproj_call(attn, ret, zs, w['ssm_glu_w'], w['ssm_glu_b'], w['w_out'], x2d, g1, layer)
    x1 = x1.reshape(batch, seq, d)
    xg, gate_slot, tok = _route_call(x1, w['norm2_g'][layer][None], sh2, sc2, w['router_wt'], layer)
    return x1, (xg, gate_slot, tok), (k_new, v_new, r_fin, h_fin)


def kernel(x_prompt, x_sample, cache_attn_k, cache_attn_v, state_ret, state_ssm_re, state_ssm_im, c, c_ctx, mod_w, mod_b, norm1_g, norm2_g, w_in, w_out, attn_sink, ret_log_gamma, ret_gn_g, ssm_lam_re, ssm_lam_im, ssm_log_step, ssm_b_re, ssm_b_im, ssm_c_re, ssm_c_im, ssm_d, ssm_glu_w, ssm_glu_b, router_w, moe_w_gate, moe_w_up, moe_w_down, final_norm_g):
    depth = w_in.shape[0]
    batch, seq, d = x_prompt.shape
    dec_batch, dec_seq, _ = x_sample.shape
    past = cache_attn_k.shape[2]
    G, P = N_SSM_GROUPS, SSM_STATE
    assert dec_seq % PROJ_ROWS == 0 and (batch * seq) % PROJ_ROWS == 0

    cv = jnp.concatenate([c_ctx[None, :], c, jnp.zeros((MOD_ROWS - 1 - dec_batch, d), F32)], axis=0)
    mod = _mod_call(cv, mod_w, mod_b)
    ssm_mats = _ssm_tile_matrices(*_ssm_group_matrices(
        ssm_lam_re, ssm_lam_im, ssm_log_step, ssm_b_re, ssm_b_im, ssm_c_re, ssm_c_im, ssm_d))
    rope = _rope_tables(dec_seq)
    no_rope = (jnp.ones((PROJ_ROWS, LANES), F32), jnp.zeros((PROJ_ROWS, LANES), F32))
    w = {
        'norm1_g': norm1_g, 'norm2_g': norm2_g, 'w_in': w_in.astype(BF16), 'w_out': w_out.astype(BF16),
        'attn_sink': attn_sink,
        'ret_log_gamma': ret_log_gamma, 'ret_gn_g': ret_gn_g, 'ssm_glu_w': ssm_glu_w,
        'ssm_glu_b': ssm_glu_b.reshape(depth, 1, SSM_WIDTH), 'router_wt': router_w.transpose(0, 2, 1),
    }
    h0_lat = jnp.concatenate([state_ssm_re.reshape(dec_batch, depth, 2, SSM_NQ, SSM_SW // 2),
                              state_ssm_im.reshape(dec_batch, depth, 2, SSM_NQ, SSM_SW // 2)], axis=-1)
    h0_lat = h0_lat.transpose(1, 3, 2, 0, 4)
    ctx = {
        'k': cache_attn_k.reshape(dec_batch, depth, past, KV_W),
        'v': cache_attn_v.reshape(dec_batch, depth, past, KV_W),
        'ret': state_ret, 'ssm': h0_lat,
    }
    xp, xs = x_prompt, x_sample
    k_all = jnp.zeros((batch, depth, seq * N_KV_HEADS, HEAD_DIM), F32)
    v_all = jnp.zeros((batch, depth, seq * N_KV_HEADS, HEAD_DIM), F32)
    r_all = jnp.zeros((batch, depth, 2, N_RET_HEADS, RET_DK, RET_DK), F32)
    ssms = []
    for layer in range(depth):
        m = mod[layer]
        mods_p = [m[0:1, i * d:(i + 1) * d][:, None, :] for i in range(6)]
        mods_s = [m[1:1 + dec_batch, i * d:(i + 1) * d][:, None, :] for i in range(6)]
        xp1, (xg_p, gs_p, tok_p), (k_all, v_all, r_all, h_l) = _mix_and_route(
            xp, layer, mods_p, w, ssm_mats, no_rope, None, (k_all, v_all, r_all))
        xs1, (xg_s, gs_s, tok_s), _ = _mix_and_route(xs, layer, mods_s, w, ssm_mats, rope, ctx)
        cap_p, cap_s = xg_p.shape[2], xg_s.shape[2]
        y_p, y_s = _expert_call(
            xg_p.reshape(N_EXPERTS, batch * cap_p, d), gs_p,
            xg_s.reshape(N_EXPERTS, dec_batch * cap_s, d), gs_s,
            moe_w_gate, moe_w_up, moe_w_down, layer)
        last = layer == depth - 1
        fuse_p, fuse_s = last and _combine_tile(seq, d) == d, last and _combine_tile(dec_seq, d) == d
        xp = _combine_call(y_p.reshape(N_EXPERTS, batch, cap_p, d), tok_p, xp1, mods_p[5],
                           final_norm_g[None] if fuse_p else None)
        xs = _combine_call(y_s.reshape(N_EXPERTS, dec_batch, cap_s, d), tok_s, xs1, mods_s[5],
                           final_norm_g[None] if fuse_s else None)
        ssms.append(h_l)
    y_prompt = xp if fuse_p else _norm_call(xp.reshape(batch * seq, d), final_norm_g[None]).reshape(batch, seq, d)
    y_sample = xs if fuse_s else _norm_call(
        xs.reshape(dec_batch * dec_seq, d), final_norm_g[None]).reshape(dec_batch, dec_seq, d)
    h_all = jnp.stack(ssms, axis=0).reshape(depth, SSM_NQ, 2, batch, 2, SSM_GL, P)
    h_all = h_all.transpose(4, 3, 0, 2, 1, 5, 6).reshape(2, batch, depth, 2, G, P)
    kv_shape = (batch, depth, seq, N_KV_HEADS, HEAD_DIM)
    return (y_prompt, y_sample, k_all.reshape(kv_shape), v_all.reshape(kv_shape), r_all,
            h_all[0], h_all[1])
```

```python
import functools
import math

import jax
import jax.numpy as jnp
from jax import lax
from jax.experimental import pallas as pl
from jax.experimental.pallas import tpu as pltpu

F32 = jnp.float32
BF16 = jnp.bfloat16
HIGHEST = lax.Precision.HIGHEST

D_MODEL = 2048
GRID_W = 64
EPS = 1e-6
HEAD_DIM = 128
ATTN_W = D_MODEL // 2
N_HEADS = ATTN_W // HEAD_DIM
N_KV_HEADS = N_HEADS // 4
Q_PER_KV = N_HEADS // N_KV_HEADS
KV_W = N_KV_HEADS * HEAD_DIM
ATTN_BLOCK = 128
RET_W = D_MODEL // 4
RET_DK = 128
N_RET_HEADS = RET_W // RET_DK
RET_CHUNK = 128
SSM_WIDTH = D_MODEL // 4
SSM_GROUP = 16
N_SSM_GROUPS = SSM_WIDTH // SSM_GROUP
SSM_STATE = 64
N_EXPERTS = 16
EXPERT_FF = D_MODEL // 2
EC_CAPACITY = 2
ROPE_BASE = 10000.0
IN_W = ATTN_W + 2 * KV_W + 4 * RET_W + SSM_WIDTH

LANES = 128
HEAD_Q0 = 0
HEAD_K0 = ATTN_W // LANES
HEAD_V0 = HEAD_K0 + KV_W // LANES
HEAD_RQ0 = HEAD_V0 + KV_W // LANES
HEAD_RK0 = HEAD_RQ0 + RET_W // LANES
HEAD_RV0 = HEAD_RK0 + RET_W // LANES
HEAD_RG0 = HEAD_RV0 + RET_W // LANES
HEAD_U0 = HEAD_RG0 + RET_W // LANES

SSM_CHUNK = 8
SSM_GL = LANES // SSM_GROUP
SSM_NQ = SSM_WIDTH // LANES
SSM_FLAT = SSM_CHUNK * LANES
SSM_SW = SSM_GL * 2 * SSM_STATE

MOD_ROWS = 16
PROJ_ROWS = 512
PROJ_COLS = 512
VMEM_LIMIT = 56 * 1024 * 1024


def _cparams(sem):
    return pltpu.CompilerParams(dimension_semantics=sem, vmem_limit_bytes=VMEM_LIMIT)


def _silu(x):
    return x * jax.nn.sigmoid(x)


def _nt_dot(a, b):
    return lax.dot_general(a, b, (((1,), (1,)), ((), ())), preferred_element_type=F32)


def _tn_dot(a, b):
    return lax.dot_general(a, b, (((0,), (0,)), ((), ())), preferred_element_type=F32)


def _mod_kernel(cv_ref, w_ref, b_ref, o_ref):
    a = _silu(cv_ref[...])
    o_ref[0] = jnp.dot(a, w_ref[0], preferred_element_type=F32, precision=HIGHEST) + b_ref[0]


def _mod_call(cv, mod_w, mod_b):
    depth, d, e = mod_w.shape
    tn = 1024
    return pl.pallas_call(
        _mod_kernel,
        grid=(depth, e // tn),
        in_specs=[
            pl.BlockSpec((MOD_ROWS, d), lambda l, j: (0, 0)),
            pl.BlockSpec((1, d, tn), lambda l, j: (l, 0, j)),
            pl.BlockSpec((1, 1, tn), lambda l, j: (l, 0, j)),
        ],
        out_specs=pl.BlockSpec((1, MOD_ROWS, tn), lambda l, j: (l, 0, j)),
        out_shape=jax.ShapeDtypeStruct((depth, MOD_ROWS, e), F32),
        compiler_params=_cparams(("parallel", "parallel")),
        name="mod",
    )(cv, mod_w, mod_b.reshape(depth, 1, e))


def _rope_head(z, cos, sin_signed):
    lane = lax.broadcasted_iota(jnp.int32, z.shape, 1)
    partner = jnp.where((lane % 64) < 32, pltpu.roll(z, LANES - 32, 1), pltpu.roll(z, 32, 1))
    return z * cos + partner * sin_signed


def _inproj_kernel(x_ref, g_ref, sh_ref, sc_ref, w_ref, cos_ref, sin_ref, z_ref, u_ref, *, use_rope):
    x = x_ref[...]
    ms = jnp.mean(x * x, axis=-1, keepdims=True)
    hn = x * lax.rsqrt(ms + EPS) * g_ref[...]
    h = (hn * (1.0 + sc_ref[0]) + sh_ref[0]).astype(BF16)
    tn = PROJ_COLS
    heads_per_tile = tn // LANES
    for jt in range(IN_W // tn):
        acc = jnp.dot(h, w_ref[0, :, jt * tn:(jt + 1) * tn], preferred_element_type=F32)
        head0 = jt * heads_per_tile
        if head0 >= HEAD_U0:
            u_ref[:, (head0 - HEAD_U0) * LANES:(head0 - HEAD_U0) * LANES + tn] = acc
            continue
        for k in range(heads_per_tile):
            head = head0 + k
            zk = acc[:, k * LANES:(k + 1) * LANES]
            if HEAD_RK0 <= head < HEAD_RV0:
                zk = zk * RET_DK ** -0.5
            if use_rope and (head < HEAD_V0 or HEAD_RQ0 <= head < HEAD_RV0):
                zk = _rope_head(zk, cos_ref[...], sin_ref[...])
            z_ref[:, head * LANES:(head + 1) * LANES] = zk.astype(z_ref.dtype)


def _inproj_call(x2d, g, shift, scale, w_in_bf16, layer, cos, sin_signed, use_rope, seq):
    n, d = x2d.shape
    tm = PROJ_ROWS
    z_w = IN_W - SSM_WIDTH
    nb = shift.shape[0]
    rows_per_mod = n // nb
    tiles_per_seq = seq // tm
    mod_idx = lambda i: (i * tm // rows_per_mod, 0, 0)
    pos_idx = (lambda i: (i % tiles_per_seq, 0)) if use_rope else (lambda i: (0, 0))
    return pl.pallas_call(
        functools.partial(_inproj_kernel, use_rope=use_rope),
        grid=(n // tm,),
        in_specs=[
            pl.BlockSpec((tm, d), lambda i: (i, 0)),
            pl.BlockSpec((1, d), lambda i: (0, 0)),
            pl.BlockSpec((1, 1, d), mod_idx),
            pl.BlockSpec((1, 1, d), mod_idx),
            pl.BlockSpec((1, d, IN_W), lambda i: (layer, 0, 0), pipeline_mode=pl.Buffered(1)),
            pl.BlockSpec((tm, LANES), pos_idx),
            pl.BlockSpec((tm, LANES), pos_idx),
        ],
        out_specs=[
            pl.BlockSpec((tm, z_w), lambda i: (i, 0)),
            pl.BlockSpec((tm, SSM_WIDTH), lambda i: (i, 0)),
        ],
        out_shape=[jax.ShapeDtypeStruct((n, z_w), BF16), jax.ShapeDtypeStruct((n, SSM_WIDTH), F32)],
        compiler_params=_cparams(("parallel",)),
        name="inproj",
    )(x2d, g, shift, scale, w_in_bf16, cos, sin_signed)


def _sink_attend(s, sink_col, v):
    m = jnp.maximum(jnp.max(s, axis=-1, keepdims=True), sink_col)
    e = jnp.exp(s - m).astype(BF16)
    v_aug = jnp.concatenate([v, jnp.ones(v.shape, v.dtype)], axis=1)
    oa = jnp.dot(e, v_aug, preferred_element_type=F32)
    den = oa[:, HEAD_DIM:HEAD_DIM + 1] + jnp.exp(sink_col - m)
    return oa[:, :HEAD_DIM] * (1.0 / den)


def _stack_q(q_ref, kh):
    return jnp.concatenate(
        [q_ref[:, (kh * Q_PER_KV + g) * HEAD_DIM:(kh * Q_PER_KV + g + 1) * HEAD_DIM] for g in range(Q_PER_KV)], axis=0)


def _sink_col(sink_ref, kh, rows):
    return jnp.concatenate(
        [jnp.full((rows, 1), sink_ref[kh * Q_PER_KV + g], F32) for g in range(Q_PER_KV)], axis=0)


def _attn_ctx_kernel(sink_ref, q_ref, k_ref, v_ref, k_all_ref, v_all_ref, o_ref, kn_ref, vn_ref):
    del k_all_ref, v_all_ref
    rows = q_ref.shape[0]
    scale = HEAD_DIM ** -0.5
    for kh in range(N_KV_HEADS):
        hs = slice(kh * HEAD_DIM, (kh + 1) * HEAD_DIM)
        s = _nt_dot(_stack_q(q_ref, kh), k_ref[:, hs]) * scale
        o = _sink_attend(s, _sink_col(sink_ref, kh, rows), v_ref[:, hs])
        for g in range(Q_PER_KV):
            c0 = (kh * Q_PER_KV + g) * HEAD_DIM
            o_ref[:, c0:c0 + HEAD_DIM] = o[g * rows:(g + 1) * rows].astype(o_ref.dtype)
    for kh in range(N_KV_HEADS):
        hs = slice(kh * HEAD_DIM, (kh + 1) * HEAD_DIM)
        kn_ref[0, 0, pl.ds(kh, rows, stride=N_KV_HEADS), :] = k_ref[:, hs].astype(F32)
        vn_ref[0, 0, pl.ds(kh, rows, stride=N_KV_HEADS), :] = v_ref[:, hs].astype(F32)


def _attn_ctx_call(z, sink, k_all, v_all, layer, batch, seq):
    n = z.shape[0]
    kcol, vcol = HEAD_K0 * LANES // KV_W, HEAD_V0 * LANES // KV_W
    kv_out = pl.BlockSpec((1, 1, seq * N_KV_HEADS, HEAD_DIM), lambda b: (b, layer, 0, 0))
    return pl.pallas_call(
        _attn_ctx_kernel,
        grid=(batch,),
        in_specs=[
            pl.BlockSpec(memory_space=pltpu.SMEM),
            pl.BlockSpec((seq, ATTN_W), lambda b: (b, 0)),
            pl.BlockSpec((seq, KV_W), lambda b: (b, kcol)),
            pl.BlockSpec((seq, KV_W), lambda b: (b, vcol)),
            pl.BlockSpec(memory_space=pl.ANY),
            pl.BlockSpec(memory_space=pl.ANY),
        ],
        out_specs=[pl.BlockSpec((seq, ATTN_W), lambda b: (b, 0)), kv_out, kv_out],
        out_shape=[
            jax.ShapeDtypeStruct((n, ATTN_W), BF16),
            jax.ShapeDtypeStruct(k_all.shape, F32),
            jax.ShapeDtypeStruct(v_all.shape, F32),
        ],
        input_output_aliases={4: 1, 5: 2},
        compiler_params=_cparams(("parallel",)),
        name="attn_ctx",
    )(sink, z, z, z, k_all, v_all)


def _attn_lat_kernel(sink_ref, q_ref, kp_ref, kc_ref, kn_ref, vp_ref, vc_ref, vn_ref, kx_ref, vx_ref, o_ref, *, nblk):
    i = pl.program_id(1)
    rows = q_ref.shape[0]
    scale = HEAD_DIM ** -0.5
    r = lax.broadcasted_iota(jnp.int32, (rows, ATTN_BLOCK), 0)
    c = lax.broadcasted_iota(jnp.int32, (rows, ATTN_BLOCK), 1)
    neg = jnp.float32(-jnp.inf)
    bias_prev = jnp.where(c >= r, jnp.where(i > 0, 0.0, neg), neg)
    bias_next = jnp.where(c <= r, jnp.where(i < nblk - 1, 0.0, neg), neg)
    n_ctx = kx_ref.shape[2]
    bias = jnp.concatenate([bias_prev, jnp.zeros((rows, ATTN_BLOCK), F32), bias_next,
                            jnp.zeros((rows, n_ctx), F32)], axis=1)
    bias = jnp.concatenate([bias] * Q_PER_KV, axis=0)
    for kh in range(N_KV_HEADS):
        hs = slice(kh * HEAD_DIM, (kh + 1) * HEAD_DIM)
        kcat = jnp.concatenate([kp_ref[:, hs], kc_ref[:, hs], kn_ref[:, hs], kx_ref[0, 0, :, hs].astype(BF16)], axis=0)
        vcat = jnp.concatenate([vp_ref[:, hs], vc_ref[:, hs], vn_ref[:, hs], vx_ref[0, 0, :, hs].astype(BF16)], axis=0)
        s = _nt_dot(_stack_q(q_ref, kh), kcat) * scale + bias
        o = _sink_attend(s, _sink_col(sink_ref, kh, rows), vcat)
        for g in range(Q_PER_KV):
            c0 = (kh * Q_PER_KV + g) * HEAD_DIM
            o_ref[:, c0:c0 + HEAD_DIM] = o[g * rows:(g + 1) * rows].astype(o_ref.dtype)


def _attn_lat_call(z, sink, cache_k, cache_v, layer, batch, seq):
    n = z.shape[0]
    nblk = seq // ATTN_BLOCK
    kcol, vcol = HEAD_K0 * LANES // KV_W, HEAD_V0 * LANES // KV_W
    past = cache_k.shape[2]

    def kv_spec(col, off):
        return pl.BlockSpec((ATTN_BLOCK, KV_W),
                            lambda b, i: (b * nblk + jnp.clip(i + off, 0, nblk - 1), col))

    cache_spec = pl.BlockSpec((1, 1, past, KV_W), lambda b, i: (b, layer, 0, 0))
    return pl.pallas_call(
        functools.partial(_attn_lat_kernel, nblk=nblk),
        grid=(batch, nblk),
        in_specs=[
            pl.BlockSpec(memory_space=pltpu.SMEM),
            pl.BlockSpec((ATTN_BLOCK, ATTN_W), lambda b, i: (b * nblk + i, 0)),
            kv_spec(kcol, -1), kv_spec(kcol, 0), kv_spec(kcol, 1),
            kv_spec(vcol, -1), kv_spec(vcol, 0), kv_spec(vcol, 1),
            cache_spec, cache_spec,
        ],
        out_specs=pl.BlockSpec((ATTN_BLOCK, ATTN_W), lambda b, i: (b * nblk + i, 0)),
        out_shape=jax.ShapeDtypeStruct((n, ATTN_W), BF16),
        compiler_params=_cparams(("parallel", "parallel")),
        name="attn_lat",
    )(sink, z, z, z, z, z, z, z, cache_k, cache_v)


def _ret_kernel(lg_ref, q_ref, k_ref, v_ref, rg_ref, gn_ref, *rest, n_chunks, has_r0, has_r_all):
    r0_ref = rest[0] if has_r0 else None
    o_ref, rfin_ref, of_scr, ob_scr, r_scr, dec_scr, qd_scr, kd_scr, cd_scr = rest[has_r0 + has_r_all:]
    C, dk = RET_CHUNK, RET_DK
    ii = lax.broadcasted_iota(jnp.int32, (C, C), 0).astype(F32)
    jj = lax.broadcasted_iota(jnp.int32, (C, C), 1).astype(F32)
    pos = lax.broadcasted_iota(jnp.int32, (C, dk), 0).astype(F32)
    for h in range(N_RET_HEADS):
        for d in range(2):
            lg = lg_ref[d, h]
            diff = (ii - jj) if d == 0 else (jj - ii)
            dec_scr[d, h] = jnp.where(diff >= 0, jnp.exp(jnp.maximum(diff, 0.0) * lg), 0.0)
            if d == 0:
                qd_scr[d, h] = jnp.exp((pos + 1.0) * lg)
                kd_scr[d, h] = jnp.exp((C - 1.0 - pos) * lg)
            else:
                qd_scr[d, h] = jnp.exp((C - pos) * lg)
                kd_scr[d, h] = jnp.exp(pos * lg)
            cd_scr[d, h] = jnp.exp(jnp.full((8, dk), float(C), F32) * lg)
            r_scr[d, h] = r0_ref[0, 0, d, h] if has_r0 else jnp.zeros((dk, dk), F32)

    def body(t, carry):
        for h in range(N_RET_HEADS):
            hs = slice(h * dk, (h + 1) * dk)
            for d in range(2):
                n = t if d == 0 else n_chunks - 1 - t
                rows = pl.ds(pl.multiple_of(n * C, C), C)
                q, k, v = q_ref[rows, hs], k_ref[rows, hs], v_ref[rows, hs]
                r = r_scr[d, h]
                s = _nt_dot(q, k) * dec_scr[d, h]
                intra = jnp.dot(s.astype(BF16), v, preferred_element_type=F32)
                cross = jnp.dot((q.astype(F32) * qd_scr[d, h]).astype(BF16), r.astype(BF16),
                                preferred_element_type=F32)
                kv = _tn_dot((k.astype(F32) * kd_scr[d, h]).astype(BF16), v)
                (of_scr if d == 0 else ob_scr)[rows, hs] = intra + cross
                r_scr[d, h] = cd_scr[d, h, 0:1, :] * r + kv
        return carry

    lax.fori_loop(0, n_chunks, body, 0)
    rfin_ref[0, 0] = r_scr[...]
    for h in range(N_RET_HEADS):
        hs = slice(h * dk, (h + 1) * dk)
        o = of_scr[:, hs] + ob_scr[:, hs]
        mu = jnp.mean(o, axis=-1, keepdims=True)
        var = jnp.mean(jnp.square(o - mu), axis=-1, keepdims=True)
        on = (o - mu) * lax.rsqrt(var + EPS) * gn_ref[:, hs]
        o_ref[:, hs] = (_silu(rg_ref[:, hs].astype(F32)) * on).astype(o_ref.dtype)


def _ret_call(z, log_gamma, gn_g, r0, state_layer, batch, seq, r_all=None, out_layer=0):
    n = z.shape[0]
    dk, H = RET_DK, N_RET_HEADS

    def col(head0):
        return pl.BlockSpec((seq, RET_W), lambda b: (b, head0 * LANES // RET_W))

    has_r0, has_r_all = r0 is not None, r_all is not None
    state_specs = [pl.BlockSpec((1, 1, 2, H, dk, dk), lambda b: (b, state_layer, 0, 0, 0, 0))] if has_r0 else []
    state_specs += [pl.BlockSpec(memory_space=pl.ANY)] if has_r_all else []
    state_args = ((r0,) if has_r0 else ()) + ((r_all,) if has_r_all else ())
    n_fixed = 6
    return pl.pallas_call(
        functools.partial(_ret_kernel, n_chunks=seq // RET_CHUNK, has_r0=has_r0, has_r_all=has_r_all),
        grid=(batch,),
        in_specs=[
            pl.BlockSpec(memory_space=pltpu.SMEM),
            col(HEAD_RQ0), col(HEAD_RK0), col(HEAD_RV0), col(HEAD_RG0),
            pl.BlockSpec((1, RET_W), lambda b: (0, 0)),
        ] + state_specs,
        out_specs=[
            pl.BlockSpec((seq, RET_W), lambda b: (b, 0)),
            pl.BlockSpec((1, 1, 2, H, dk, dk), lambda b: (b, out_layer, 0, 0, 0, 0)),
        ],
        out_shape=[
            jax.ShapeDtypeStruct((n, RET_W), BF16),
            jax.ShapeDtypeStruct(r_all.shape if has_r_all else (batch, 1, 2, H, dk, dk), F32),
        ],
        input_output_aliases={n_fixed + has_r0: 1} if has_r_all else {},
        scratch_shapes=[
            pltpu.VMEM((seq, RET_W), F32), pltpu.VMEM((seq, RET_W), F32),
            pltpu.VMEM((2, H, dk, dk), F32), pltpu.VMEM((2, H, RET_CHUNK, RET_CHUNK), F32),
            pltpu.VMEM((2, H, RET_CHUNK, dk), F32), pltpu.VMEM((2, H, RET_CHUNK, dk), F32),
            pltpu.VMEM((2, H, 8, dk), F32),
        ],
        compiler_params=_cparams(("parallel",)),
        name="ret",
    )(log_gamma, z, z, z, z, gn_g, *state_args)


def _ssm_group_matrices(lam_re, lam_im, log_step, b_re, b_im, c_re, c_im, d_skip):
    depth, _, G, P = lam_re.shape
    H, T = SSM_GROUP, SSM_CHUNK
    dt = jnp.exp(log_step)[..., None]
    ar, ai = lam_re * dt, lam_im * dt
    mag = jnp.exp(ar)
    lbr, lbi = mag * jnp.cos(ai), mag * jnp.sin(ai)
    den = lam_re * lam_re + lam_im * lam_im
    fr = ((lbr - 1.0) * lam_re + lbi * lam_im) / den
    fi = (lbi * lam_re - (lbr - 1.0) * lam_im) / den
    bbr = fr[..., None] * b_re - fi[..., None] * b_im
    bbi = fr[..., None] * b_im + fi[..., None] * b_re
    m = jnp.arange(T + 1, dtype=F32)[:, None]
    pmag = jnp.exp(ar[..., None, :] * m)
    pwr, pwi = pmag * jnp.cos(ai[..., None, :] * m), pmag * jnp.sin(ai[..., None, :] * m)
    xr = pwr[..., None] * bbr[..., None, :, :] - pwi[..., None] * bbi[..., None, :, :]
    xi = pwr[..., None] * bbi[..., None, :, :] + pwi[..., None] * bbr[..., None, :, :]
    kern = (jnp.einsum('ldghp,ldgmpi->ldgmhi', c_re, xr[..., :T, :, :], precision=HIGHEST)
            - jnp.einsum('ldghp,ldgmpi->ldgmhi', c_im, xi[..., :T, :, :], precision=HIGHEST))
    tok = jnp.arange(T)
    lag = tok[None, :] - tok[:, None]
    kf = kern[:, 0][:, :, jnp.clip(lag, 0, T - 1)] * (lag >= 0)[None, None, :, :, None, None].astype(F32)
    kb = kern[:, 1][:, :, jnp.clip(-lag, 0, T - 1)] * (lag <= 0)[None, None, :, :, None, None].astype(F32)
    tm = (kf + kb).transpose(0, 1, 2, 5, 3, 4)
    skip = (jnp.eye(T, dtype=F32)[None, None, :, None, :, None]
            * jnp.eye(H, dtype=F32)[None, None, None, :, None, :]
            * d_skip.reshape(depth, G, 1, H, 1, 1))
    tmat = (tm + skip).reshape(depth, G, T * H, T * H)
    rev = T - 1 - tok
    ef_r, ef_i = xr[:, 0][:, :, rev], xi[:, 0][:, :, rev]
    eb_r, eb_i = xr[:, 1][:, :, tok], xi[:, 1][:, :, tok]
    bmat = jnp.concatenate([e.transpose(0, 1, 2, 4, 3) for e in (ef_r, ef_i, eb_r, eb_i)], axis=-1)
    bmat = bmat.reshape(depth, G, T * H, 4 * P)

    def entry(d, powers):
        pr = pwr[:, d][:, :, powers][:, :, :, None, :]
        pi = pwi[:, d][:, :, powers][:, :, :, None, :]
        cr, ci = c_re[:, d][:, :, None], c_im[:, d][:, :, None]
        wr, wi = cr * pr - ci * pi, cr * pi + ci * pr
        return wr.transpose(0, 1, 4, 2, 3), -wi.transpose(0, 1, 4, 2, 3)

    cf_r, cf_i = entry(0, tok + 1)
    cb_r, cb_i = entry(1, T - tok)
    cmat = jnp.concatenate([cf_r, cf_i, cb_r, cb_i], axis=2).reshape(depth, G, 4 * P, T * H)
    lam_t = jnp.stack([pwr[..., T, :], pwi[..., T, :]], axis=2)
    return tmat, bmat, cmat, lam_t


def _ssm_tile_matrices(tmat, bmat, cmat, lam_t):
    depth = tmat.shape[0]
    T, H, GL, NQ, P = SSM_CHUNK, SSM_GROUP, SSM_GL, SSM_NQ, SSM_STATE
    tm = tmat.reshape(depth, N_SSM_GROUPS, T, H, T, H)
    lag_neg = tm[:, :, 1:, :, 0, :][:, :, ::-1]
    lag_pos = tm[:, :, 0].transpose(0, 1, 3, 2, 4)
    lags = jnp.concatenate([lag_neg, lag_pos], axis=2)
    lags = lags.reshape(depth, NQ, GL, 2 * T - 1, H, H).transpose(0, 1, 3, 2, 4, 5)
    eye = jnp.eye(GL, dtype=F32)[None, None, None, :, None, :, None]
    dmat = (lags[..., None, :] * eye).reshape(depth, NQ, 2 * T - 1, LANES, LANES)
    emat = bmat.astype(BF16).reshape(depth, NQ, GL, T, H, 2, 2, P).transpose(0, 1, 5, 6, 3, 2, 4, 7)
    emat = emat.reshape(depth, NQ, 2, 2, SSM_FLAT, 1, P)
    emat = jnp.broadcast_to(emat, (depth, NQ, 2, 2, SSM_FLAT, 2, P)).reshape(depth, NQ, 2, 2, SSM_FLAT, 2 * P)
    wmat = cmat.astype(BF16).reshape(depth, NQ, GL, 2, 2, P, T, H).transpose(0, 1, 3, 4, 5, 6, 2, 7)
    wmat = wmat.reshape(depth, NQ, 2, 2, 1, P, SSM_FLAT)
    wmat = jnp.broadcast_to(wmat, (depth, NQ, 2, 2, 2, P, SSM_FLAT)).reshape(depth, NQ, 2, 2, 2 * P, SSM_FLAT)
    lamq = lam_t.reshape(depth, 2, 2, NQ, GL * P).transpose(0, 3, 1, 2, 4)
    return dmat.astype(BF16), emat, wmat, lamq


def _gelu_tanh(x):
    return 0.5 * x * (1.0 + jnp.tanh(math.sqrt(2.0 / math.pi) * (x + 0.044715 * (x * x * x))))


def _ssm_kernel(u_ref, d_ref, e_ref, w_ref, lam_ref, h0_ref, y_ref, hfin_ref,
                xf_scr, hb_scr, st_scr, y_scr, tq_scr, bq_scr, cq_scr, *, n_chunks, batch, seq):
    d = pl.program_id(1)
    T, GL = SSM_CHUNK, SSM_GL

    @pl.when(d == 0)
    def _():
        for s in range(batch):
            for tau in range(T):
                xf_scr[tau, pl.ds(s, n_chunks, stride=batch), :] = u_ref[pl.ds(s * seq + tau, n_chunks, stride=T), :]
        for j in range(T):
            for i in range(T):
                tq_scr[j * LANES:(j + 1) * LANES, i * LANES:(i + 1) * LANES] = d_ref[0, 0, i - j + T - 1].astype(BF16)

    ch_bits = SSM_GROUP.bit_length() - 1
    row_group = (lax.broadcasted_iota(jnp.int32, (SSM_FLAT, LANES), 0) >> ch_bits) & (GL - 1)
    col_group = (lax.broadcasted_iota(jnp.int32, (LANES, SSM_FLAT), 1) >> ch_bits) & (GL - 1)
    st_bits = SSM_STATE.bit_length() - 1
    lane_half = lax.broadcasted_iota(jnp.int32, (SSM_FLAT, LANES), 1) >> st_bits
    row_half = lax.broadcasted_iota(jnp.int32, (LANES, SSM_FLAT), 0) >> st_bits
    tiles_per_part = GL * SSM_STATE // LANES
    for part in range(2):
        emat, wmat = e_ref[0, 0, 0, part], w_ref[0, 0, 0, part]
        for m in range(tiles_per_part):
            ts = slice((part * tiles_per_part + m) * LANES, (part * tiles_per_part + m + 1) * LANES)
            bq_scr[:, ts] = jnp.where(row_group == 2 * m + lane_half, emat, 0.0).astype(BF16)
            cq_scr[ts, :] = jnp.where(col_group == 2 * m + row_half, wmat, 0.0).astype(BF16)

    xf = jnp.concatenate([xf_scr[tau] for tau in range(T)], axis=1).astype(BF16)
    hb_scr[...] = jnp.dot(xf, bq_scr[...], preferred_element_type=F32)
    HW = SSM_SW // 2
    lam_re, lam_im = lam_ref[0, 0, 0, 0:1, :], lam_ref[0, 0, 0, 1:2, :]

    def body(t, s):
        s_re, s_im = s
        c = jnp.where(d == 0, t, n_chunks - 1 - t)
        rows = pl.ds(pl.multiple_of(c * batch, batch), batch)
        st_scr[rows, 0:HW] = s_re
        st_scr[rows, HW:SSM_SW] = s_im
        return (lam_re * s_re - lam_im * s_im + hb_scr[rows, 0:HW],
                lam_re * s_im + lam_im * s_re + hb_scr[rows, HW:SSM_SW])

    h0 = h0_ref[0, 0, 0]
    f_re, f_im = lax.fori_loop(0, n_chunks, body, (h0[:, 0:HW], h0[:, HW:SSM_SW]))
    hfin_ref[0, 0] = jnp.concatenate([f_re, f_im], axis=1)
    part = jnp.dot(st_scr[...].astype(BF16), cq_scr[...], preferred_element_type=F32)

    @pl.when(d == 0)
    def _():
        y_scr[...] = part + jnp.dot(xf, tq_scr[...], preferred_element_type=F32)

    @pl.when(d == 1)
    def _():
        y = _gelu_tanh(y_scr[...] + part)
        for tau in range(T):
            xf_scr[tau] = y[:, tau * LANES:(tau + 1) * LANES]
        for s in range(batch):
            for tau in range(T):
                y_ref[pl.ds(s * seq + tau, n_chunks, stride=T), :] = xf_scr[tau, pl.ds(s, n_chunks, stride=batch), :]


def _ssm_call(u, dmat, emat, wmat, lamq, h0, layer, state_layer, batch, seq):
    n = batch * seq
    n_chunks = seq // SSM_CHUNK
    m = batch * n_chunks
    n_lags = 2 * SSM_CHUNK - 1
    return pl.pallas_call(
        functools.partial(_ssm_kernel, n_chunks=n_chunks, batch=batch, seq=seq),
        grid=(SSM_NQ, 2),
        in_specs=[
            pl.BlockSpec((n, LANES), lambda q, d: (0, q), pipeline_mode=pl.Buffered(1)),
            pl.BlockSpec((1, 1, n_lags, LANES, LANES), lambda q, d: (layer, q, 0, 0, 0)),
            pl.BlockSpec((1, 1, 1, 2, SSM_FLAT, LANES), lambda q, d: (layer, q, d, 0, 0, 0)),
            pl.BlockSpec((1, 1, 1, 2, LANES, SSM_FLAT), lambda q, d: (layer, q, d, 0, 0, 0)),
            pl.BlockSpec((1, 1, 1, 2, SSM_SW // 2), lambda q, d: (layer, q, d, 0, 0)),
            pl.BlockSpec((1, 1, 1, batch, SSM_SW), lambda q, d: (state_layer, q, d, 0, 0)),
        ],
        out_specs=[
            pl.BlockSpec((n, LANES), lambda q, d: (0, q)),
            pl.BlockSpec((1, 1, batch, SSM_SW), lambda q, d: (q, d, 0, 0)),
        ],
        out_shape=[
            jax.ShapeDtypeStruct((n, SSM_WIDTH), F32),
            jax.ShapeDtypeStruct((SSM_NQ, 2, batch, SSM_SW), F32),
        ],
        scratch_shapes=[pltpu.VMEM((SSM_CHUNK, m, LANES), F32), pltpu.VMEM((m, SSM_SW), F32),
                        pltpu.VMEM((m, SSM_SW), F32), pltpu.VMEM((m, SSM_FLAT), F32),
                        pltpu.VMEM((SSM_FLAT, SSM_FLAT), BF16), pltpu.VMEM((SSM_FLAT, SSM_SW), BF16),
                        pltpu.VMEM((SSM_SW, SSM_FLAT), BF16)],
        compiler_params=_cparams(("arbitrary", "arbitrary")),
        name="ssm",
    )(u, dmat, emat, wmat, lamq, h0)


def _outproj_kernel(attn_ref, ret_ref, zs_ref, gw_ref, gb_ref, w_ref, x_ref, g1_ref, o_ref):
    zs = zs_ref[...]
    gl = jnp.dot(zs.astype(BF16), gw_ref[0].astype(BF16), preferred_element_type=F32) + gb_ref[0]
    mix = jnp.concatenate([attn_ref[...], ret_ref[...], (zs * jax.nn.sigmoid(gl)).astype(BF16)], axis=1)
    tn = PROJ_COLS
    for jt in range(o_ref.shape[1] // tn):
        cs = slice(jt * tn, (jt + 1) * tn)
        acc = jnp.dot(mix, w_ref[0, :, cs], preferred_element_type=F32)
        o_ref[:, cs] = x_ref[:, cs] + g1_ref[0, :, cs] * acc


def _outproj_call(attn, ret, zs, glu_w, glu_b, w_out_bf16, x2d, gate, layer):
    n, d = x2d.shape
    tm = PROJ_ROWS
    rows_per_gate = n // gate.shape[0]
    return pl.pallas_call(
        _outproj_kernel,
        grid=(n // tm,),
        in_specs=[
            pl.BlockSpec((tm, ATTN_W), lambda i: (i, 0)),
            pl.BlockSpec((tm, RET_W), lambda i: (i, 0)),
            pl.BlockSpec((tm, SSM_WIDTH), lambda i: (i, 0)),
            pl.BlockSpec((1, SSM_WIDTH, SSM_WIDTH), lambda i: (layer, 0, 0)),
            pl.BlockSpec((1, 1, SSM_WIDTH), lambda i: (layer, 0, 0)),
            pl.BlockSpec((1, d, d), lambda i: (layer, 0, 0), pipeline_mode=pl.Buffered(1)),
            pl.BlockSpec((tm, d), lambda i: (i, 0)),
            pl.BlockSpec((1, 1, d), lambda i: (i * tm // rows_per_gate, 0, 0)),
        ],
        out_specs=pl.BlockSpec((tm, d), lambda i: (i, 0)),
        out_shape=jax.ShapeDtypeStruct((n, d), F32),
        compiler_params=_cparams(("parallel",)),
        name="outproj",
    )(attn, ret, zs, glu_w, glu_b, w_out_bf16, x2d, gate)


GATHER_ROWS = 512
TOKEN_SPLIT_BITS = 5
TOKEN_SPLIT = 1 << TOKEN_SPLIT_BITS


def _route_kernel(x_ref, g_ref, sh_ref, sc_ref, rw_ref, xg_ref, gs_ref, tok_ref, h_scr, *, cap, per_batch_mod):
    bb, t = x_ref.shape[0], x_ref.shape[1]
    E = N_EXPERTS
    R = bb * E
    rc = 256
    rw = rw_ref[0]
    rw_hi = rw.astype(BF16)
    rw_lo = (rw - rw_hi.astype(F32)).astype(BF16)
    affs = []
    for bi in range(bb):
        mi = bi if per_batch_mod else 0
        logits = []
        for c0 in range(0, t, rc):
            x = x_ref[bi, c0:c0 + rc, :]
            ms = jnp.mean(x * x, axis=-1, keepdims=True)
            h = x * lax.rsqrt(ms + EPS) * g_ref[...]
            h = h * (1.0 + sc_ref[mi]) + sh_ref[mi]
            h_hi = h.astype(BF16)
            h_scr[bi, c0:c0 + rc, :] = h_hi
            h_lo = (h - h_hi.astype(F32)).astype(BF16)
            logits.append(_nt_dot(rw_hi, h_hi) + (_nt_dot(rw_hi, h_lo) + _nt_dot(rw_lo, h_hi)))
        lg = jnp.concatenate(logits, axis=1)
        ex = jnp.exp(lg - jnp.max(lg, axis=0, keepdims=True))
        affs.append(ex / jnp.sum(ex, axis=0, keepdims=True))
    aff = jnp.concatenate(affs, axis=0)

    def count_ge(v):
        return jnp.sum(jnp.where(aff >= v, 1.0, 0.0), axis=1, keepdims=True)

    def bisect_bits(_, lohi):
        lo, hi = lohi
        mid = lo + ((hi - lo + 1) >> 1)
        ok = count_ge(lax.bitcast_convert_type(mid, F32)) >= float(cap)
        return jnp.where(ok, mid, lo), jnp.where(ok, hi, mid - 1)

    lo_b, _ = lax.fori_loop(0, 31, bisect_bits,
                            (jnp.zeros((R, 1), jnp.int32), jnp.full((R, 1), 0x7F800000, jnp.int32)))

    def bisect_val(_, lohi):
        lo, hi = lohi
        mid = lo + (hi - lo) * 0.5
        ok = count_ge(mid) >= float(cap)
        return jnp.where(ok, mid, lo), jnp.where(ok, hi, mid)

    _, hi_v = lax.fori_loop(0, 8, bisect_val,
                            (lax.bitcast_convert_type(lo_b, F32), lax.bitcast_convert_type(lo_b + 1, F32)))
    thr = jnp.max(jnp.where(aff < hi_v, aff, 0.0), axis=1, keepdims=True)
    gt = aff > thr
    eq = aff == thr
    need = float(cap) - jnp.sum(jnp.where(gt, 1.0, 0.0), axis=1, keepdims=True)
    tri = (lax.broadcasted_iota(jnp.int32, (LANES, LANES), 0)
           < lax.broadcasted_iota(jnp.int32, (LANES, LANES), 1)).astype(BF16)

    def prefix_count(flags):
        out, before = [], jnp.zeros((R, 1), F32)
        for c0 in range(0, t, LANES):
            blk = flags[:, c0:c0 + LANES]
            out.append(jnp.dot(blk.astype(BF16), tri, preferred_element_type=F32) + before)
            before = before + jnp.sum(blk, axis=1, keepdims=True)
        return jnp.concatenate(out, axis=1)

    sel = gt | (eq & (prefix_count(jnp.where(eq, 1.0, 0.0)) < need))
    pos = prefix_count(jnp.where(sel, 1.0, 0.0))
    base = ((lax.broadcasted_iota(jnp.int32, (R, t), 0) & (E - 1)) * cap).astype(F32)
    gpos = jnp.where(sel, pos + base, -1.0)
    a_hi = aff.astype(BF16)
    a_mid = (aff - a_hi.astype(F32)).astype(BF16)
    a_lo = (aff - a_hi.astype(F32) - a_mid.astype(F32)).astype(BF16)
    tok_i = lax.broadcasted_iota(jnp.int32, (8, t), 1)
    row_i = lax.broadcasted_iota(jnp.int32, (8, t), 0)
    tok_parts = jnp.where(row_i == 0, tok_i >> TOKEN_SPLIT_BITS,
                          jnp.where(row_i == 1, tok_i & (TOKEN_SPLIT - 1), 0))
    tok_parts = tok_parts.astype(F32).astype(BF16)
    n_grp = max(1, GATHER_ROWS // cap)
    for bi in range(bb):
        r0 = bi * E
        aff_parts = jnp.concatenate([a_hi[r0:r0 + E], a_mid[r0:r0 + E], a_lo[r0:r0 + E],
                                     jnp.zeros((LANES - 3 * E, t), BF16)], axis=0)
        for e0 in range(0, E, n_grp):
            onehots = []
            for e in range(e0, e0 + n_grp):
                slot = (e * cap + lax.broadcasted_iota(jnp.int32, (cap, t), 0)).astype(F32)
                onehots.append(jnp.where(gpos[r0 + e:r0 + e + 1, :] == slot, 1.0, 0.0).astype(BF16))
            onehot = jnp.concatenate(onehots, axis=0)
            xg = jnp.dot(onehot, h_scr[bi], preferred_element_type=F32)
            gate = _nt_dot(onehot, aff_parts)
            for k in range(n_grp):
                e, rs = e0 + k, slice(k * cap, (k + 1) * cap)
                xg_ref[e, bi] = xg[rs].astype(BF16)
                gs_ref[e, bi * cap:(bi + 1) * cap, :] = (
                    gate[rs, e:e + 1] + gate[rs, E + e:E + e + 1] + gate[rs, 2 * E + e:2 * E + e + 1])
            tk = _nt_dot(tok_parts, onehot)
            tok_ref[bi, :, e0 * cap:(e0 + n_grp) * cap] = TOKEN_SPLIT * tk[0:1, :] + tk[1:2, :]


ROUTE_TOKENS = 1024


def _route_call(x3d, g, shift, scale, router_wt, layer):
    batch, t, d = x3d.shape
    cap = EC_CAPACITY * t // N_EXPERTS
    bb = max(1, ROUTE_TOKENS // t)
    per_batch_mod = shift.shape[0] > 1
    mod_spec = pl.BlockSpec((bb if per_batch_mod else 1, 1, d), (lambda b: (b, 0, 0)) if per_batch_mod
                            else (lambda b: (0, 0, 0)))
    return pl.pallas_call(
        functools.partial(_route_kernel, cap=cap, per_batch_mod=per_batch_mod),
        grid=(batch // bb,),
        in_specs=[
            pl.BlockSpec((bb, t, d), lambda b: (b, 0, 0)),
            pl.BlockSpec((1, d), lambda b: (0, 0)),
            mod_spec, mod_spec,
            pl.BlockSpec((1, N_EXPERTS, d), lambda b: (layer, 0, 0)),
        ],
        out_specs=[
            pl.BlockSpec((N_EXPERTS, bb, cap, d), lambda b: (0, b, 0, 0)),
            pl.BlockSpec((N_EXPERTS, bb * cap, 1), lambda b: (0, b, 0)),
            pl.BlockSpec((bb, 1, N_EXPERTS * cap), lambda b: (b, 0, 0)),
        ],
        out_shape=[
            jax.ShapeDtypeStruct((N_EXPERTS, batch, cap, d), BF16),
            jax.ShapeDtypeStruct((N_EXPERTS, batch * cap, 1), F32),
            jax.ShapeDtypeStruct((batch, 1, N_EXPERTS * cap), F32),
        ],
        scratch_shapes=[pltpu.VMEM((bb, t, d), BF16)],
        compiler_params=_cparams(("parallel",)),
        name="route",
    )(x3d, g, shift, scale, router_wt)


EXPERT_UP_TILE = 512
EXPERT_DOWN_TILE = 512


def _expert_kernel(xa_ref, xb_ref, wg_ref, wu_ref, wd_ref, ga_ref, gb_ref, oa_ref, ob_ref, h_scr, *, n_up):
    s = pl.program_id(1)
    tf = EXPERT_UP_TILE
    for k in range(n_up):
        @pl.when(s == k)
        def _(k=k):
            wg, wu = wg_ref[0, 0].astype(BF16), wu_ref[0, 0].astype(BF16)
            for i, x_ref in enumerate((xa_ref, xb_ref)):
                x = x_ref[0]
                a = jnp.dot(x, wg, preferred_element_type=F32)
                b = jnp.dot(x, wu, preferred_element_type=F32)
                h_scr[i, :, k * tf:(k + 1) * tf] = (_silu(a) * b).astype(BF16)

    @pl.when(s >= n_up)
    def _():
        wd = wd_ref[0, 0].astype(BF16)
        for i, (g_ref, o_ref) in enumerate(((ga_ref, oa_ref), (gb_ref, ob_ref))):
            y = jnp.dot(h_scr[i], wd, preferred_element_type=F32)
            o_ref[0] = (y * g_ref[0]).astype(o_ref.dtype)


def _expert_call(xg_a, gate_a, xg_b, gate_b, w_gate, w_up, w_down, layer):
    E, m, d = xg_a.shape
    assert xg_b.shape == xg_a.shape
    ff = w_gate.shape[-1]
    tf, tn = EXPERT_UP_TILE, EXPERT_DOWN_TILE
    n_up, n_down = ff // tf, d // tn
    def up_expert(e, s):
        return jnp.where(s >= n_up, jnp.minimum(e + 1, E - 1), e)

    x_spec = pl.BlockSpec((1, m, d), lambda e, s: (up_expert(e, s), 0, 0))
    g_spec = pl.BlockSpec((1, m, 1), lambda e, s: (e, 0, 0))
    o_spec = pl.BlockSpec((1, m, tn), lambda e, s: (e, 0, jnp.maximum(s - n_up, 0)))
    up_spec = pl.BlockSpec((1, 1, d, tf), lambda e, s: (layer, up_expert(e, s), 0, jnp.where(s >= n_up, 0, s)))
    return pl.pallas_call(
        functools.partial(_expert_kernel, n_up=n_up),
        grid=(E, n_up + n_down),
        in_specs=[
            x_spec, x_spec, up_spec, up_spec,
            pl.BlockSpec((1, 1, ff, tn), lambda e, s: (layer, e, 0, jnp.maximum(s - n_up, 0))),
            g_spec, g_spec,
        ],
        out_specs=[o_spec, o_spec],
        out_shape=[jax.ShapeDtypeStruct((E, m, d), BF16), jax.ShapeDtypeStruct((E, m, d), BF16)],
        scratch_shapes=[pltpu.VMEM((2, m, ff), BF16)],
        compiler_params=_cparams(("parallel", "arbitrary")),
        name="expert",
    )(xg_a, xg_b, w_gate, w_up, w_down, gate_a, gate_b)


COMBINE_TILE_ELEMS = 1024 * 1024


def _combine_kernel(y_ref, tok_ref, x_ref, g2_ref, *rest, final_norm):
    fg_ref = rest[0] if final_norm else None
    o_ref, oh_scr = rest[1:] if final_norm else rest
    j = pl.program_id(1)
    t = x_ref.shape[1]
    n_slots = tok_ref.shape[2]

    @pl.when(j == 0)
    def _():
        tok_i = lax.broadcasted_iota(jnp.int32, (t, n_slots), 0).astype(F32)
        oh_scr[...] = jnp.where(tok_i == tok_ref[0], 1.0, 0.0).astype(BF16)

    y = jnp.concatenate([y_ref[e, 0] for e in range(N_EXPERTS)], axis=0)
    moe = jnp.dot(oh_scr[...], y, preferred_element_type=F32)
    x2 = x_ref[0] + g2_ref[0] * moe
    if final_norm:
        x2 = x2 * lax.rsqrt(jnp.mean(x2 * x2, axis=-1, keepdims=True) + EPS) * fg_ref[...]
    o_ref[0] = x2


def _combine_tile(t, d):
    return min(d, COMBINE_TILE_ELEMS // t)


def _combine_call(y4d, tok, x3d, gate, final_g=None):
    batch, t, d = x3d.shape
    cap = y4d.shape[2]
    tn = _combine_tile(t, d)
    final_norm = final_g is not None
    assert tn == d or not final_norm
    nb = gate.shape[0]
    gate_idx = (lambda b, j: (b, 0, j)) if nb > 1 else (lambda b, j: (0, 0, j))
    norm_specs = [pl.BlockSpec((1, d), lambda b, j: (0, 0))] if final_norm else []
    norm_args = (final_g,) if final_norm else ()
    return pl.pallas_call(
        functools.partial(_combine_kernel, final_norm=final_norm),
        grid=(batch, d // tn),
        in_specs=[
            pl.BlockSpec((N_EXPERTS, 1, cap, tn), lambda b, j: (0, b, 0, j)),
            pl.BlockSpec((1, 1, N_EXPERTS * cap), lambda b, j: (b, 0, 0)),
            pl.BlockSpec((1, t, tn), lambda b, j: (b, 0, j)),
            pl.BlockSpec((1, 1, tn), gate_idx),
        ] + norm_specs,
        out_specs=pl.BlockSpec((1, t, tn), lambda b, j: (b, 0, j)),
        out_shape=jax.ShapeDtypeStruct((batch, t, d), F32),
        scratch_shapes=[pltpu.VMEM((t, N_EXPERTS * cap), BF16)],
        compiler_params=_cparams(("parallel", "arbitrary")),
        name="combine",
    )(y4d, tok, x3d, gate, *norm_args)


def _norm_kernel(x_ref, g_ref, o_ref):
    x = x_ref[...]
    o_ref[...] = x * lax.rsqrt(jnp.mean(x * x, axis=-1, keepdims=True) + EPS) * g_ref[...]


def _norm_call(x2d, g):
    n, d = x2d.shape
    tm = 512
    return pl.pallas_call(
        _norm_kernel,
        grid=(n // tm,),
        in_specs=[pl.BlockSpec((tm, d), lambda i: (i, 0)), pl.BlockSpec((1, d), lambda i: (0, 0))],
        out_specs=pl.BlockSpec((tm, d), lambda i: (i, 0)),
        out_shape=jax.ShapeDtypeStruct((n, d), F32),
        compiler_params=_cparams(("parallel",)),
        name="final_norm",
    )(x2d, g)


def _rope_tables(n_tokens):
    n_rows = n_tokens // GRID_W
    row = jnp.repeat(jnp.arange(n_rows), GRID_W).astype(F32)
    col = jnp.tile(jnp.arange(GRID_W), n_rows).astype(F32)
    axis_dim = HEAD_DIM // 2
    inv_freq = ROPE_BASE ** (-jnp.arange(0, axis_dim, 2, dtype=F32) / axis_dim)
    ang_r, ang_c = row[:, None] * inv_freq[None, :], col[:, None] * inv_freq[None, :]
    cos = jnp.concatenate([jnp.cos(ang_r), jnp.cos(ang_r), jnp.cos(ang_c), jnp.cos(ang_c)], axis=1)
    sin = jnp.concatenate([-jnp.sin(ang_r), jnp.sin(ang_r), -jnp.sin(ang_c), jnp.sin(ang_c)], axis=1)
    return cos, sin


def _mix_and_route(x3d, layer, mods, w, ssm_mats, rope, ctx, ctx_out=None):
    batch, seq, d = x3d.shape
    n = batch * seq
    sh1, sc1, g1, sh2, sc2, _ = mods
    x2d = x3d.reshape(n, d)
    latent = ctx is not None
    z, u = _inproj_call(x2d, w['norm1_g'][layer][None], sh1, sc1, w['w_in'], layer, rope[0], rope[1], latent, seq)
    if latent:
        attn = _attn_lat_call(z, w['attn_sink'][layer], ctx['k'], ctx['v'], layer, batch, seq)
        k_new = v_new = None
        r0, h0, state_layer = ctx['ret'], ctx['ssm'], layer
    else:
        attn, k_new, v_new = _attn_ctx_call(z, w['attn_sink'][layer], ctx_out[0], ctx_out[1], layer, batch, seq)
        r0, h0, state_layer = None, jnp.zeros((1, SSM_NQ, 2, batch, SSM_SW), F32), 0
    r_all = None if latent else ctx_out[2]
    ret, r_fin = _ret_call(z, w['ret_log_gamma'][layer], w['ret_gn_g'][layer][None], r0, state_layer, batch, seq,
                           r_all, 0 if latent else layer)
    zs, h_fin = _ssm_call(u, *ssm_mats, h0, layer, state_layer, batch, seq)
    x1 = _outproj_call(attn, ret, zs, w['ssm_glu_w'], w['ssm_glu_b'], w['w_out'], x2d, g1, layer)
    x1 = x1.reshape(batch, seq, d)
    xg, gate_slot, tok = _route_call(x1, w['norm2_g'][layer][None], sh2, sc2, w['router_wt'], layer)
    return x1, (xg, gate_slot, tok), (k_new, v_new, r_fin, h_fin)


def kernel(x_prompt, x_sample, cache_attn_k, cache_attn_v, state_ret, state_ssm_re, state_ssm_im, c, c_ctx, mod_w, mod_b, norm1_g, norm2_g, w_in, w_out, attn_sink, ret_log_gamma, ret_gn_g, ssm_lam_re, ssm_lam_im, ssm_log_step, ssm_b_re, ssm_b_im, ssm_c_re, ssm_c_im, ssm_d, ssm_glu_w, ssm_glu_b, router_w, moe_w_gate, moe_w_up, moe_w_down, final_norm_g):
    depth = w_in.shape[0]
    batch, seq, d = x_prompt.shape
    dec_batch, dec_seq, _ = x_sample.shape
    past = cache_attn_k.shape[2]
    G, P = N_SSM_GROUPS, SSM_STATE
    assert dec_seq % PROJ_ROWS == 0 and (batch * seq) % PROJ_ROWS == 0

    cv = jnp.concatenate([c_ctx[None, :], c, jnp.zeros((MOD_ROWS - 1 - dec_batch, d), F32)], axis=0)
    mod = _mod_call(cv, mod_w, mod_b)
    ssm_mats = _ssm_tile_matrices(*_ssm_group_matrices(
        ssm_lam_re, ssm_lam_im, ssm_log_step, ssm_b_re, ssm_b_im, ssm_c_re, ssm_c_im, ssm_d))
    rope = _rope_tables(dec_seq)
    no_rope = (jnp.ones((PROJ_ROWS, LANES), F32), jnp.zeros((PROJ_ROWS, LANES), F32))
    w = {
        'norm1_g': norm1_g, 'norm2_g': norm2_g, 'w_in': w_in.astype(BF16), 'w_out': w_out.astype(BF16),
        'attn_sink': attn_sink,
        'ret_log_gamma': ret_log_gamma, 'ret_gn_g': ret_gn_g, 'ssm_glu_w': ssm_glu_w,
        'ssm_glu_b': ssm_glu_b.reshape(depth, 1, SSM_WIDTH), 'router_wt': router_w.transpose(0, 2, 1),
    }
    h0_lat = jnp.concatenate([state_ssm_re.reshape(dec_batch, depth, 2, SSM_NQ, SSM_SW // 2),
                              state_ssm_im.reshape(dec_batch, depth, 2, SSM_NQ, SSM_SW // 2)], axis=-1)
    h0_lat = h0_lat.transpose(1, 3, 2, 0, 4)
    ctx = {
        'k': cache_attn_k.reshape(dec_batch, depth, past, KV_W),
        'v': cache_attn_v.reshape(dec_batch, depth, past, KV_W),
        'ret': state_ret, 'ssm': h0_lat,
    }
    xp, xs = x_prompt, x_sample
    k_all = jnp.zeros((batch, depth, seq * N_KV_HEADS, HEAD_DIM), F32)
    v_all = jnp.zeros((batch, depth, seq * N_KV_HEADS, HEAD_DIM), F32)
    r_all = jnp.zeros((batch, depth, 2, N_RET_HEADS, RET_DK, RET_DK), F32)
    ssms = []
    for layer in range(depth):
        m = mod[layer]
        mods_p = [m[0:1, i * d:(i + 1) * d][:, None, :] for i in range(6)]
        mods_s = [m[1:1 + dec_batch, i * d:(i + 1) * d][:, None, :] for i in range(6)]
        xp1, (xg_p, gs_p, tok_p), (k_all, v_all, r_all, h_l) = _mix_and_route(
            xp, layer, mods_p, w, ssm_mats, no_rope, None, (k_all, v_all, r_all))
        xs1, (xg_s, gs_s, tok_s), _ = _mix_and_route(xs, layer, mods_s, w, ssm_mats, rope, ctx)
        cap_p, cap_s = xg_p.shape[2], xg_s.shape[2]
        y_p, y_s = _expert_call(
            xg_p.reshape(N_EXPERTS, batch * cap_p, d), gs_p,
            xg_s.reshape(N_EXPERTS, dec_batch * cap_s, d), gs_s,
            moe_w_gate, moe_w_up, moe_w_down, layer)
        last = layer == depth - 1
        fuse_p, fuse_s = last and _combine_tile(seq, d) == d, last and _combine_tile(dec_seq, d) == d
        xp = _combine_call(y_p.reshape(N_EXPERTS, batch, cap_p, d), tok_p, xp1, mods_p[5],
                           final_norm_g[None] if fuse_p else None)
        xs = _combine_call(y_s.reshape(N_EXPERTS, dec_batch, cap_s, d), tok_s, xs1, mods_s[5],
                           final_norm_g[None] if fuse_s else None)
        ssms.append(h_l)
    y_prompt = xp if fuse_p else _norm_call(xp.reshape(batch * seq, d), final_norm_g[None]).reshape(batch, seq, d)
    y_sample = xs if fuse_s else _norm_call(
        xs.reshape(dec_batch * dec_seq, d), final_norm_g[None]).reshape(dec_batch, dec_seq, d)
    h_all = jnp.stack(ssms, axis=0).reshape(depth, SSM_NQ, 2, batch, 2, SSM_GL, P)
    h_all = h_all.transpose(4, 3, 0, 2, 1, 5, 6).reshape(2, batch, depth, 2, G, P)
    kv_shape = (batch, depth, seq, N_KV_HEADS, HEAD_DIM)
    return (y_prompt, y_sample, k_all.reshape(kv_shape), v_all.reshape(kv_shape), r_all,
            h_all[0], h_all[1])
```
